```python
import math, functools
import jax, jax.numpy as jnp
from jax import lax
import numpy as np

D_MODEL = 2048
BATCH = 2
SEQ = 4096
DEPTH = 1
DEC_BATCH = 128
DEC_SEQ = 8
PAST_LEN = 2048
PAGE_SIZE = 128

MIX_WIDTH = D_MODEL
FOX_WIDTH = MIX_WIDTH // 2
FOX_HEAD_DIM = 128
FOX_HEADS = FOX_WIDTH // FOX_HEAD_DIM
FOX_SCALE = FOX_HEAD_DIM ** -0.5
Q_BLOCK = 128
FORGET_BIAS_INIT = 3.0
SSM_WIDTH = MIX_WIDTH - FOX_WIDTH
SSM_CH_PER_GROUP = 16
SSM_GROUPS = SSM_WIDTH // SSM_CH_PER_GROUP
SSM_STATE = 64
IN_SPLITS = (FOX_WIDTH, 2 * FOX_WIDTH, 3 * FOX_WIDTH, 3 * FOX_WIDTH + FOX_HEADS)
IN_WIDTH = 3 * FOX_WIDTH + FOX_HEADS + SSM_WIDTH
PEER_HEADS = 8
PEER_N_KEYS = 128
PEER_N_EXPERTS = PEER_N_KEYS * PEER_N_KEYS
PEER_TOPK = 16
PEER_KEY_DIM = 256
PEER_HALF = PEER_KEY_DIM // 2
PEER_BLOCK = 128
N_MOD = 6
EPS = 1e-6

kernel_name = "hymba_fox_s5_peer_adaln_step"


def rms_norm(x, g):
    xf = x.astype(jnp.float32)
    y = xf * lax.rsqrt(jnp.mean(xf * xf, axis=-1, keepdims=True) + EPS)
    return (y * g.astype(jnp.float32)).astype(x.dtype)


def ada_modulation(c, w_ada, b_ada):
    return jnp.split(jax.nn.silu(c) @ w_ada + b_ada, N_MOD, axis=-1)


def modulate(h, shift, scale):
    return h * (1 + scale[:, None, :]) + shift[:, None, :]


def mixer_projections(h, w_in, b_forget):
    b, l, _ = h.shape
    q, k, v, fg, u = jnp.split(h @ w_in, list(IN_SPLITS), axis=-1)
    shape = (b, l, FOX_HEADS, FOX_HEAD_DIM)
    logf = jax.nn.log_sigmoid((fg + b_forget).astype(jnp.float32))
    return q.reshape(shape), k.reshape(shape), v.reshape(shape), logf, u


def fox_attend_prompt(q, k, v, logf):
    b, l, h, d = q.shape
    qb = min(Q_BLOCK, l)
    nb = l // qb
    fcum = jnp.cumsum(logf, axis=1)
    fk = fcum.transpose(0, 2, 1)[:, :, None, :]
    q_blocks = q.reshape(b, nb, qb, h, d).transpose(1, 0, 2, 3, 4)
    fq_blocks = fcum.reshape(b, nb, qb, h).transpose(1, 0, 3, 2)
    k_pos = jnp.arange(l)

    def block(args):
        i, q_i, fq_i = args
        s = jnp.einsum('bqhd,bkhd->bhqk', q_i, k).astype(jnp.float32) * FOX_SCALE
        s = s + fq_i[..., None] - fk
        q_pos = i * qb + jnp.arange(qb)
        s = jnp.where(k_pos[None, :] <= q_pos[:, None], s, -jnp.inf)
        p = jax.nn.softmax(s, axis=-1).astype(v.dtype)
        return jnp.einsum('bhqk,bkhd->bqhd', p, v)

    out = lax.map(block, (jnp.arange(nb), q_blocks, fq_blocks))
    return out.transpose(1, 0, 2, 3, 4).reshape(b, l, h, d)


def fox_attend_sample(q, k, v, logf, k_past, v_past, logf_past):
    n_past = k_past.shape[1]
    n_new = q.shape[1]
    fcum = jnp.cumsum(jnp.concatenate([logf_past.astype(jnp.float32), logf], axis=1), axis=1)
    fq = fcum[:, n_past:].transpose(0, 2, 1)[..., None]
    fk = fcum.transpose(0, 2, 1)[:, :, None, :]
    s = jnp.concatenate([jnp.einsum('bqhd,bkhd->bhqk', q, k_past),
                         jnp.einsum('bqhd,bkhd->bhqk', q, k)], axis=-1).astype(jnp.float32)
    s = s * FOX_SCALE + fq - fk
    mask = jnp.arange(n_past + n_new)[None, :] <= n_past + jnp.arange(n_new)[:, None]
    s = jnp.where(mask, s, -jnp.inf)
    p = jax.nn.softmax(s, axis=-1).astype(v.dtype)
    return (jnp.einsum('bhqk,bkhd->bqhd', p[..., :n_past], v_past)
            + jnp.einsum('bhqk,bkhd->bqhd', p[..., n_past:], v))


def s5_discretise(a_re, a_im, log_dt, b_re, b_im):
    a_re = jnp.minimum(a_re.astype(jnp.float32), -1e-4)
    a_im = a_im.astype(jnp.float32)
    dt = jnp.exp(log_dt.astype(jnp.float32))[:, None]
    mag = jnp.exp(a_re * dt)
    abar_re = mag * jnp.cos(a_im * dt)
    abar_im = mag * jnp.sin(a_im * dt)
    den = a_re * a_re + a_im * a_im
    nr, ni = abar_re - 1.0, abar_im
    coef_re = (nr * a_re + ni * a_im) / den
    coef_im = (ni * a_re - nr * a_im) / den
    b_re = b_re.astype(jnp.float32)
    b_im = b_im.astype(jnp.float32)
    bbar_re = coef_re[..., None] * b_re - coef_im[..., None] * b_im
    bbar_im = coef_re[..., None] * b_im + coef_im[..., None] * b_re
    return abar_re, abar_im, bbar_re, bbar_im


def s5_combine(e1, e2):
    a1r, a1i, b1r, b1i = e1
    a2r, a2i, b2r, b2i = e2
    return (a2r * a1r - a2i * a1i, a2r * a1i + a2i * a1r,
            a2r * b1r - a2i * b1i + b2r, a2r * b1i + a2i * b1r + b2i)


def s5_branch(u, h0_re, h0_im, a_re, a_im, log_dt, b_re, b_im, c_re, c_im, d_skip, w_glu, b_glu):
    bsz, l, _ = u.shape
    abr, abi, bbr, bbi = s5_discretise(a_re, a_im, log_dt, b_re, b_im)
    uf = u.astype(jnp.float32).reshape(bsz, l, SSM_GROUPS, SSM_CH_PER_GROUP)
    bu_re = jnp.einsum('blgc,gpc->blgp', uf, bbr)
    bu_im = jnp.einsum('blgc,gpc->blgp', uf, bbi)
    h0_re = h0_re.astype(jnp.float32)
    h0_im = h0_im.astype(jnp.float32)
    bu_re = bu_re.at[:, 0].add(abr * h0_re - abi * h0_im)
    bu_im = bu_im.at[:, 0].add(abr * h0_im + abi * h0_re)
    elems = (jnp.broadcast_to(abr, bu_re.shape), jnp.broadcast_to(abi, bu_re.shape), bu_re, bu_im)
    _, _, h_re, h_im = lax.associative_scan(s5_combine, elems, axis=1)
    y = (jnp.einsum('blgp,gcp->blgc', h_re, c_re.astype(jnp.float32))
         - jnp.einsum('blgp,gcp->blgc', h_im, c_im.astype(jnp.float32))
         + d_skip.astype(jnp.float32).reshape(SSM_GROUPS, SSM_CH_PER_GROUP) * uf)
    z = jax.nn.gelu(y.reshape(bsz, l, SSM_WIDTH))
    out = z * jax.nn.sigmoid(z @ w_glu.astype(jnp.float32) + b_glu.astype(jnp.float32))
    return out.astype(u.dtype), h_re[:, -1], h_im[:, -1]


def peer_ffn(h, w_q, subkeys, exp_u, exp_v):
    bsz, l, d = h.shape
    n_tok = bsz * l
    n_blk = -(-n_tok // PEER_BLOCK)
    t = jnp.pad(h.reshape(n_tok, d), ((0, n_blk * PEER_BLOCK - n_tok), (0, 0)))
    t = t.reshape(n_blk, PEER_BLOCK, d)
    n_cand = PEER_TOPK * PEER_TOPK

    def block(tb):
        q = (tb @ w_q).reshape(PEER_BLOCK, PEER_HEADS, 2, PEER_HALF)
        s = jnp.einsum('thxd,hxnd->thxn', q, subkeys).astype(jnp.float32)
        sv, si = lax.top_k(s, PEER_TOPK)
        cand_s = (sv[:, :, 0, :, None] + sv[:, :, 1, None, :]).reshape(PEER_BLOCK, PEER_HEADS, n_cand)
        cand_i = (si[:, :, 0, :, None] * PEER_N_KEYS + si[:, :, 1, None, :]).reshape(PEER_BLOCK, PEER_HEADS, n_cand)
        top_s, top_j = lax.top_k(cand_s, PEER_TOPK)
        e = jnp.take_along_axis(cand_i, top_j, axis=-1)
        g = jax.nn.softmax(top_s, axis=-1)
        a = jax.nn.gelu(jnp.einsum('thkd,td->thk', exp_u[e], tb).astype(jnp.float32))
        return jnp.einsum('thk,thkd->td', (g * a).astype(tb.dtype), exp_v[e])

    out = lax.map(block, t)
    return out.reshape(n_blk * PEER_BLOCK, d)[:n_tok].reshape(bsz, l, d)


def trunk_layer(x, c, attend, h0_re, h0_im,
                w_ada, b_ada, g_norm1, g_norm2, w_in, b_forget,
                ssm_a_re, ssm_a_im, ssm_log_dt, ssm_b_re, ssm_b_im, ssm_c_re, ssm_c_im, ssm_d,
                w_glu, b_glu, g_attn_out, g_ssm_out, w_out, peer_w_q, peer_subkeys, peer_u, peer_v):
    bsz, l, _ = x.shape
    shift1, scale1, gate1, shift2, scale2, gate2 = ada_modulation(c, w_ada, b_ada)
    h = modulate(rms_norm(x, g_norm1), shift1, scale1)
    q, k, v, logf, u = mixer_projections(h, w_in, b_forget)
    att = attend(q, k, v, logf).reshape(bsz, l, FOX_WIDTH)
    ssm, s_re, s_im = s5_branch(u, h0_re, h0_im, ssm_a_re, ssm_a_im, ssm_log_dt, ssm_b_re, ssm_b_im,
                                ssm_c_re, ssm_c_im, ssm_d, w_glu, b_glu)
    merged = jnp.concatenate([rms_norm(att, g_attn_out), rms_norm(ssm, g_ssm_out)], axis=-1) @ w_out
    x = x + gate1[:, None, :] * merged
    h = modulate(rms_norm(x, g_norm2), shift2, scale2)
    x = x + gate2[:, None, :] * peer_ffn(h, peer_w_q, peer_subkeys, peer_u, peer_v)
    return x, k, v, logf, s_re, s_im


def setup_inputs(seed: int = 0) -> dict:
    key = jax.random.key(seed)
    keys = iter(jax.random.split(key, 40))

    def nrm(shape, scale):
        return jax.random.normal(next(keys), shape, jnp.float32) * scale

    n_pages = PAST_LEN // PAGE_SIZE
    n_used = DEC_BATCH * n_pages
    n_pool = n_used + (n_used + 3) // 4
    page_table = jax.random.permutation(next(keys), n_pool)[:n_used].reshape(DEC_BATCH, n_pages).astype(jnp.int32)
    G, P, CG = SSM_GROUPS, SSM_STATE, SSM_CH_PER_GROUP
    return {
        'x_prompt': nrm((BATCH, SEQ, D_MODEL), 1.0),
        'x_sample': nrm((DEC_BATCH, DEC_SEQ, D_MODEL), 1.0),
        'cache_k': nrm((DEPTH, n_pool, PAGE_SIZE, FOX_HEADS, FOX_HEAD_DIM), 1.0),
        'cache_v': nrm((DEPTH, n_pool, PAGE_SIZE, FOX_HEADS, FOX_HEAD_DIM), 1.0),
        'cache_logf': jax.nn.log_sigmoid(FORGET_BIAS_INIT + nrm((DEPTH, n_pool, PAGE_SIZE, FOX_HEADS), 1.0)),
        'state_ssm_re': nrm((DEPTH, DEC_BATCH, G, P), 0.3),
        'state_ssm_im': nrm((DEPTH, DEC_BATCH, G, P), 0.3),
        'page_table': page_table,
        'c_prompt': nrm((BATCH, D_MODEL), 1.0),
        'c_sample': nrm((DEC_BATCH, D_MODEL), 1.0),
        'w_ada': nrm((DEPTH, D_MODEL, N_MOD * D_MODEL), 0.5 * D_MODEL ** -0.5),
        'b_ada': nrm((DEPTH, N_MOD * D_MODEL), 0.01),
        'g_norm1': 1.0 + nrm((DEPTH, D_MODEL), 0.02),
        'g_norm2': 1.0 + nrm((DEPTH, D_MODEL), 0.02),
        'w_in': nrm((DEPTH, D_MODEL, IN_WIDTH), D_MODEL ** -0.5),
        'b_forget': FORGET_BIAS_INIT + nrm((DEPTH, FOX_HEADS), 0.1),
        'ssm_a_re': -0.5 + nrm((DEPTH, G, P), 0.01),
        'ssm_a_im': jnp.pi * jnp.arange(P, dtype=jnp.float32) + nrm((DEPTH, G, P), 0.01),
        'ssm_log_dt': jax.random.uniform(next(keys), (DEPTH, G), jnp.float32, math.log(1e-3), math.log(1e-1)),
        'ssm_b_re': nrm((DEPTH, G, P, CG), (2 * CG) ** -0.5),
        'ssm_b_im': nrm((DEPTH, G, P, CG), (2 * CG) ** -0.5),
        'ssm_c_re': nrm((DEPTH, G, CG, P), (2 * P) ** -0.5),
        'ssm_c_im': nrm((DEPTH, G, CG, P), (2 * P) ** -0.5),
        'ssm_d': nrm((DEPTH, SSM_WIDTH), 1.0),
        'w_glu': nrm((DEPTH, SSM_WIDTH, SSM_WIDTH), SSM_WIDTH ** -0.5),
        'b_glu': nrm((DEPTH, SSM_WIDTH), 0.01),
        'g_attn_out': 1.0 + nrm((DEPTH, FOX_WIDTH), 0.02),
        'g_ssm_out': 1.0 + nrm((DEPTH, SSM_WIDTH), 0.02),
        'w_out': nrm((DEPTH, MIX_WIDTH, D_MODEL), MIX_WIDTH ** -0.5),
        'peer_w_q': nrm((DEPTH, D_MODEL, PEER_HEADS * PEER_KEY_DIM), D_MODEL ** -0.5),
        'peer_subkeys': nrm((DEPTH, PEER_HEADS, 2, PEER_N_KEYS, PEER_HALF), PEER_HALF ** -0.5),
        'peer_u': nrm((DEPTH, PEER_N_EXPERTS, D_MODEL), D_MODEL ** -0.5),
        'peer_v': nrm((DEPTH, PEER_N_EXPERTS, D_MODEL), 0.5),
        'g_final': 1.0 + nrm((D_MODEL,), 0.02),
    }


def reference(x_prompt, x_sample, cache_k, cache_v, cache_logf, state_ssm_re, state_ssm_im, page_table,
              c_prompt, c_sample, w_ada, b_ada, g_norm1, g_norm2, w_in, b_forget,
              ssm_a_re, ssm_a_im, ssm_log_dt, ssm_b_re, ssm_b_im, ssm_c_re, ssm_c_im, ssm_d,
              w_glu, b_glu, g_attn_out, g_ssm_out, w_out, peer_w_q, peer_subkeys, peer_u, peer_v, g_final):
    n_batch = x_prompt.shape[0]
    n_dec = x_sample.shape[0]
    past_len = page_table.shape[1] * cache_k.shape[2]
    zero_state = jnp.zeros((n_batch, SSM_GROUPS, SSM_STATE), jnp.float32)
    xp, xs = x_prompt, x_sample
    kp_l, vp_l, fp_l, rp_l, ip_l = [], [], [], [], []
    ks_l, vs_l, fs_l, rs_l, is_l = [], [], [], [], []
    for l in range(DEPTH):
        lw = (w_ada[l], b_ada[l], g_norm1[l], g_norm2[l], w_in[l], b_forget[l],
              ssm_a_re[l], ssm_a_im[l], ssm_log_dt[l], ssm_b_re[l], ssm_b_im[l], ssm_c_re[l], ssm_c_im[l], ssm_d[l],
              w_glu[l], b_glu[l], g_attn_out[l], g_ssm_out[l], w_out[l],
              peer_w_q[l], peer_subkeys[l], peer_u[l], peer_v[l])
        xp, kp, vp, fp, rp, ip = trunk_layer(xp, c_prompt, fox_attend_prompt, zero_state, zero_state, *lw)
        k_past = cache_k[l, page_table].reshape(n_dec, past_len, FOX_HEADS, FOX_HEAD_DIM)
        v_past = cache_v[l, page_table].reshape(n_dec, past_len, FOX_HEADS, FOX_HEAD_DIM)
        f_past = cache_logf[l, page_table].reshape(n_dec, past_len, FOX_HEADS)
        attend_s = functools.partial(fox_attend_sample, k_past=k_past, v_past=v_past, logf_past=f_past)
        xs, ks, vs, fs, rs, is_ = trunk_layer(xs, c_sample, attend_s, state_ssm_re[l], state_ssm_im[l], *lw)
        kp_l.append(kp); vp_l.append(vp); fp_l.append(fp.astype(cache_logf.dtype))
        rp_l.append(rp.astype(state_ssm_re.dtype)); ip_l.append(ip.astype(state_ssm_im.dtype))
        ks_l.append(ks); vs_l.append(vs); fs_l.append(fs.astype(cache_logf.dtype))
        rs_l.append(rs.astype(state_ssm_re.dtype)); is_l.append(is_.astype(state_ssm_im.dtype))
    y_prompt = rms_norm(xp, g_final)
    y_sample = rms_norm(xs, g_final)
    return (y_prompt, y_sample,
            jnp.stack(kp_l), jnp.stack(vp_l), jnp.stack(fp_l), jnp.stack(rp_l), jnp.stack(ip_l),
            jnp.stack(ks_l), jnp.stack(vs_l), jnp.stack(fs_l), jnp.stack(rs_l), jnp.stack(is_l))
```

```python
import functools
import math

import jax
import jax.numpy as jnp
from jax import lax
from jax.experimental import pallas as pl
from jax.experimental.pallas import tpu as pltpu

F32 = jnp.float32
BF16 = jnp.bfloat16
EPS = 1e-6
NEG = -1e30

SUBLANES = 8
LANES = 128
VMEM_LIMIT_BYTES = 56 * 1024 * 1024

N_MOD = 6
FOX_HEAD_DIM = 128
SSM_CH_PER_GROUP = 16
SSM_STATE = 64
SSM_CHUNK_GROUPS = 16
PEER_N_KEYS = 128
PEER_TOPK = 16
PEER_EXPERT_TILE = 1024

TOK_BLOCK = 256
ATT_BLOCK = 256
PEER_TOK_BLOCK = 512
MASK_TOK_BLOCK = 64
SCAN_LANES = 512


def _cparams(*sem):
    return pltpu.CompilerParams(dimension_semantics=sem, vmem_limit_bytes=VMEM_LIMIT_BYTES)


def _const_spec(shape):
    nd = len(shape)
    return pl.BlockSpec(shape, lambda *_: (0,) * nd)


def _split2(x):
    hi = x.astype(BF16)
    lo = (x - hi.astype(F32)).astype(BF16)
    return hi, lo


def _split3(x):
    x1 = x.astype(BF16)
    r1 = x - x1.astype(F32)
    x2 = r1.astype(BF16)
    x3 = (r1 - x2.astype(F32)).astype(BF16)
    return x1, x2, x3


def _dot(a, b):
    return jnp.dot(a, b, preferred_element_type=F32)


def _dot_nt(a, b):
    return lax.dot_general(a, b, (((1,), (1,)), ((), ())), preferred_element_type=F32)


def _rms_mod(xg, g, shift, scale):
    ms = jnp.mean(xg * xg, axis=-1, keepdims=True)
    y = xg * lax.rsqrt(ms + EPS) * g
    return y * (1.0 + scale) + shift


def _mod_spec(n_mod_rows, n_steps, tg, d):
    if n_mod_rows == n_steps * tg:
        return pl.BlockSpec((tg, 1, d), lambda i, *_: (i, 0, 0))
    assert n_steps % n_mod_rows == 0
    per = n_steps // n_mod_rows
    return pl.BlockSpec((1, 1, d), lambda i, *_: (i // per, 0, 0))


def _ada_kernel(c_ref, w_ref, b_ref, o_ref):
    a = jax.nn.silu(c_ref[...]).astype(BF16)
    o_ref[...] = _dot(a, w_ref[...].astype(BF16)) + b_ref[...]


def _ada_modulation(c_all, w_ada, b_ada):
    m, d = c_all.shape
    n = w_ada.shape[1]
    tn = 1024
    return pl.pallas_call(
        _ada_kernel,
        grid=(n // tn,),
        in_specs=[_const_spec((m, d)), pl.BlockSpec((d, tn), lambda j: (0, j)), pl.BlockSpec((1, tn), lambda j: (0, j))],
        out_specs=pl.BlockSpec((m, tn), lambda j: (0, j)),
        out_shape=jax.ShapeDtypeStruct((m, n), F32),
        compiler_params=_cparams("arbitrary"),
        name="ada_modulation",
    )(c_all, w_ada, b_ada.reshape(1, n))


def _inproj_kernel(x_ref, shift_ref, scale_ref, g_ref, w_ref, bf_ref, q_ref, k_ref, v_ref, u_ref, lf_ref):
    tg, _, d = x_ref.shape
    h = _rms_mod(x_ref[...], g_ref[...], shift_ref[...], scale_ref[...])
    r = _dot(h.reshape(tg * SUBLANES, d).astype(BF16), w_ref[...])
    w = q_ref.shape[1]
    q_ref[...] = r[:, 0:w]
    k_ref[...] = r[:, w:2 * w]
    v_ref[...] = r[:, 2 * w:3 * w]
    u_ref[...] = r[:, 3 * w:4 * w]
    lf_ref[...] = jax.nn.log_sigmoid(r[:, 4 * w:] + bf_ref[...])


def _in_projection(xg, shift, scale, g, w_cat, bf_pad, width):
    ng, _, d = xg.shape
    t = ng * SUBLANES
    tg = TOK_BLOCK // SUBLANES
    steps = ng // tg
    n = w_cat.shape[1]
    mod_spec = _mod_spec(shift.shape[0], steps, tg, d)
    tok = lambda w: pl.BlockSpec((TOK_BLOCK, w), lambda i: (i, 0))
    return pl.pallas_call(
        _inproj_kernel,
        grid=(steps,),
        in_specs=[pl.BlockSpec((tg, SUBLANES, d), lambda i: (i, 0, 0)), mod_spec, mod_spec,
                  _const_spec((1, d)), _const_spec((d, n)), _const_spec((1, LANES))],
        out_specs=[tok(width), tok(width), tok(width), tok(width), tok(LANES)],
        out_shape=[jax.ShapeDtypeStruct((t, width), F32)] * 4 + [jax.ShapeDtypeStruct((t, LANES), F32)],
        compiler_params=_cparams("arbitrary"),
        name="in_projection",
    )(xg, shift, scale, g, w_cat, bf_pad)


def _cumsum_kernel(x_ref, tri_ref, o_ref, carry_ref):
    @pl.when(pl.program_id(1) == 0)
    def _():
        carry_ref[...] = jnp.zeros_like(carry_ref)

    x1, x2, x3 = _split3(x_ref[...])
    tri = tri_ref[...]
    loc = _dot(tri, x1) + _dot(tri, x2) + _dot(tri, x3) + carry_ref[...]
    o_ref[...] = loc
    carry_ref[...] = loc[-1:, :]


def _cumsum_time(x, n_batch):
    t, w = x.shape
    l = t // n_batch
    tb = 256
    tri = (lax.broadcasted_iota(jnp.int32, (tb, tb), 1) <= lax.broadcasted_iota(jnp.int32, (tb, tb), 0)).astype(BF16)
    per = l // tb
    return pl.pallas_call(
        _cumsum_kernel,
        grid=(n_batch, per),
        in_specs=[pl.BlockSpec((tb, w), lambda b, i: (b * per + i, 0)), _const_spec((tb, tb))],
        out_specs=pl.BlockSpec((tb, w), lambda b, i: (b * per + i, 0)),
        out_shape=jax.ShapeDtypeStruct((t, w), F32),
        scratch_shapes=[pltpu.VMEM((1, w), F32)],
        compiler_params=_cparams("arbitrary", "arbitrary"),
        name="logf_cumsum",
    )(x, tri)


def _cumsum_short_kernel(x_ref, o_ref):
    run = x_ref[:, 0:1, :]
    o_ref[:, 0:1, :] = run
    for r in range(1, x_ref.shape[1]):
        run = run + x_ref[:, r:r + 1, :]
        o_ref[:, r:r + 1, :] = run


def _cumsum_short(x):
    return pl.pallas_call(_cumsum_short_kernel, out_shape=jax.ShapeDtypeStruct(x.shape, F32), name="logf_cumsum_new")(x)


def _pattn_kernel(q_ref, k_ref, v_ref, fq_ref, fk_ref, o_ref, *, scale):
    qi = pl.program_id(2)
    tq = q_ref.shape[1]
    q = q_ref[0]
    fq = fq_ref[0, 0][:, 0:1]

    def step(kj, carry, diagonal):
        m, l, acc = carry
        start = pl.multiple_of(kj * tq, tq)
        k = k_ref[0, pl.ds(start, tq), :]
        v = v_ref[0, pl.ds(start, tq), :]
        fk = fk_ref[0, 0, pl.ds(kj, 1), :]
        s = _dot_nt(q, k) * scale + (fq - fk)
        if diagonal:
            row = lax.broadcasted_iota(jnp.int32, s.shape, 0)
            col = lax.broadcasted_iota(jnp.int32, s.shape, 1)
            s = jnp.where(col <= row, s, NEG)
        m_new = jnp.maximum(m, jnp.max(s, axis=1, keepdims=True))
        alpha = jnp.exp(m - m_new)
        p = jnp.exp(s - m_new)
        l = alpha * l + jnp.sum(p, axis=1, keepdims=True)
        acc = alpha * acc + _dot(p.astype(BF16), v)
        return m_new, l, acc

    init = (jnp.full((tq, 1), NEG, F32), jnp.zeros((tq, 1), F32), jnp.zeros((tq, q.shape[1]), F32))
    carry = lax.fori_loop(0, qi, lambda kj, c: step(kj, c, False), init)
    _, l, acc = step(qi, carry, True)
    o_ref[0] = acc / l


def _prompt_attention(q, k, v, fq, fk, n_heads):
    b, l, w = q.shape
    hd = w // n_heads
    tq = ATT_BLOCK
    assert fk.shape == (b, n_heads, l // tq, tq)
    return pl.pallas_call(
        functools.partial(_pattn_kernel, scale=hd ** -0.5),
        grid=(b, n_heads, l // tq),
        in_specs=[pl.BlockSpec((1, tq, hd), lambda bi, h, i: (bi, i, h)),
                  pl.BlockSpec((1, l, hd), lambda bi, h, i: (bi, 0, h)),
                  pl.BlockSpec((1, l, hd), lambda bi, h, i: (bi, 0, h)),
                  pl.BlockSpec((1, 1, tq, LANES), lambda bi, h, i: (bi, h, i, 0)),
                  pl.BlockSpec((1, 1, l // tq, tq), lambda bi, h, i: (bi, h, 0, 0))],
        out_specs=pl.BlockSpec((1, tq, hd), lambda bi, h, i: (bi, i, h)),
        out_shape=jax.ShapeDtypeStruct((b, l, w), F32),
        compiler_params=_cparams("arbitrary", "arbitrary", "arbitrary"),
        name="prompt_attention",
    )(q, k, v, fq, fk)


def _page_sums_kernel(x_ref, m_after_ref, m_all_ref, dloc_ref, tot_ref):
    x1, x2, x3 = _split3(x_ref[...])
    ma = m_after_ref[...]
    mt = m_all_ref[...]
    dloc_ref[...] = _dot(x1, ma) + _dot(x2, ma) + _dot(x3, ma)
    tot_ref[...] = _dot(x1, mt) + _dot(x2, mt) + _dot(x3, mt)


def _page_sums(logf_pages, n_heads):
    n_pool, w = logf_pages.shape
    r = lax.broadcasted_iota(jnp.int32, (w, w), 0)
    c = lax.broadcasted_iota(jnp.int32, (w, w), 1)
    same_head = (r % n_heads) == (c % n_heads)
    m_after = (same_head & (r // n_heads > c // n_heads)).astype(BF16)
    m_all = same_head.astype(BF16)
    tp = 256
    return pl.pallas_call(
        _page_sums_kernel,
        grid=(n_pool // tp,),
        in_specs=[pl.BlockSpec((tp, w), lambda i: (i, 0)), _const_spec((w, w)), _const_spec((w, w))],
        out_specs=[pl.BlockSpec((tp, w), lambda i: (i, 0))] * 2,
        out_shape=[jax.ShapeDtypeStruct((n_pool, w), F32)] * 2,
        compiler_params=_cparams("arbitrary"),
        name="page_logf_sums",
    )(logf_pages, m_after, m_all)


def _sattn_kernel(pt_ref, q_ref, kn_ref, vn_ref, ck_ref, cv_ref, dloc_ref, tot_ref, cqc_ref, cqr_ref,
                  mb_ref, mbn_ref, o_ref, m_sc, l_sc, acc_sc, car_sc, *, scale):
    j = pl.program_id(1)

    @pl.when(j == 0)
    def _():
        m_sc[...] = jnp.full_like(m_sc, NEG)
        l_sc[...] = jnp.zeros_like(l_sc)
        acc_sc[...] = jnp.zeros_like(acc_sc)
        car_sc[...] = jnp.zeros_like(car_sc)

    q = q_ref[0].astype(BF16)
    cq = cqc_ref[0][:, 0:1]

    def update(s, vb):
        m_prev = m_sc[:, 0:1]
        m_new = jnp.maximum(m_prev, jnp.max(s, axis=1, keepdims=True))
        alpha = jnp.exp(m_prev - m_new)
        p = jnp.exp(s - m_new)
        l_sc[...] = jnp.broadcast_to(alpha * l_sc[:, 0:1] + jnp.sum(p, axis=1, keepdims=True), l_sc.shape)
        acc_sc[...] = alpha * acc_sc[...] + _dot(p.astype(BF16), vb)
        m_sc[...] = jnp.broadcast_to(m_new, m_sc.shape)

    s = _dot_nt(q, ck_ref[0].astype(BF16)) * scale
    s = s + cq + (dloc_ref[0] + car_sc[...]) + mb_ref[...]
    update(s, cv_ref[0].astype(BF16))
    car_sc[...] = car_sc[...] + tot_ref[0]

    @pl.when(j == pl.num_programs(1) - 1)
    def _():
        sn = _dot_nt(q, kn_ref[0].astype(BF16)) * scale
        sn = sn + (cq - cqr_ref[0]) + mbn_ref[...]
        update(sn, vn_ref[0].astype(BF16))
        o_ref[0] = acc_sc[...] / l_sc[:, 0:1]


def _sample_attention(page_table, q, kn, vn, cache_k, cache_v, dloc, tot, cq_col, cq_row, n_heads):
    b, rows, hd = q.shape
    n_pages = page_table.shape[1]
    cols = cache_k.shape[1]
    r = lax.broadcasted_iota(jnp.int32, (rows, cols), 0)
    c = lax.broadcasted_iota(jnp.int32, (rows, cols), 1)
    mask_past = jnp.where((r % n_heads) == (c % n_heads), 0.0, NEG).astype(F32)
    rn = lax.broadcasted_iota(jnp.int32, (rows, rows), 0)
    cn = lax.broadcasted_iota(jnp.int32, (rows, rows), 1)
    mask_new = jnp.where(((rn % n_heads) == (cn % n_heads)) & (cn // n_heads <= rn // n_heads), 0.0, NEG).astype(F32)
    last = n_pages - 1
    per_b = lambda bi, j, pt: (bi, 0, 0)
    page = lambda bi, j, pt: (pt[bi, last - j], 0, 0)
    const2 = lambda bi, j, pt: (0, 0)
    grid_spec = pltpu.PrefetchScalarGridSpec(
        num_scalar_prefetch=1,
        grid=(b, n_pages),
        in_specs=[pl.BlockSpec((1, rows, hd), per_b), pl.BlockSpec((1, rows, hd), per_b), pl.BlockSpec((1, rows, hd), per_b),
                  pl.BlockSpec((1, cols, hd), page), pl.BlockSpec((1, cols, hd), page),
                  pl.BlockSpec((1, 1, cols), page), pl.BlockSpec((1, 1, cols), page),
                  pl.BlockSpec((1, rows, LANES), per_b), pl.BlockSpec((1, 1, rows), per_b),
                  pl.BlockSpec((rows, cols), const2), pl.BlockSpec((rows, rows), const2)],
        out_specs=pl.BlockSpec((1, rows, hd), per_b),
        scratch_shapes=[pltpu.VMEM((rows, LANES), F32), pltpu.VMEM((rows, LANES), F32),
                        pltpu.VMEM((rows, hd), F32), pltpu.VMEM((1, cols), F32)],
    )
    return pl.pallas_call(
        functools.partial(_sattn_kernel, scale=hd ** -0.5),
        grid_spec=grid_spec,
        out_shape=jax.ShapeDtypeStruct((b, rows, hd), F32),
        compiler_params=_cparams("arbitrary", "arbitrary"),
        name="sample_attention",
    )(page_table, q, kn, vn, cache_k, cache_v, dloc, tot, cq_col, cq_row, mask_past, mask_new)


def _s5_tables_kernel(are_ref, aim_ref, ldt_ref, pre_ref, pim_ref, cre_ref, cim_ref):
    a_re = jnp.minimum(are_ref[...], -1e-4)
    a_im = aim_ref[...]
    dt = jnp.exp(ldt_ref[...])
    mag = jnp.exp(a_re * dt)
    abr = mag * jnp.cos(a_im * dt)
    abi = mag * jnp.sin(a_im * dt)
    den = a_re * a_re + a_im * a_im
    nr, ni = abr - 1.0, abi
    cre_ref[...] = (nr * a_re + ni * a_im) / den
    cim_ref[...] = (ni * a_re - nr * a_im) / den
    pr, pi = abr, abi
    for r in range(SUBLANES):
        pre_ref[r:r + 1, :] = pr
        pim_ref[r:r + 1, :] = pi
        pr, pi = pr * abr - pi * abi, pr * abi + pi * abr


def _s5_tables(a_re, a_im, log_dt):
    g, p = a_re.shape
    n = g * p
    flat = lambda x: x.reshape(1, n)
    ldt = jnp.broadcast_to(log_dt[:, None], (g, p))
    row = jax.ShapeDtypeStruct((1, n), F32)
    tab = jax.ShapeDtypeStruct((SUBLANES, n), F32)
    return pl.pallas_call(
        _s5_tables_kernel,
        out_shape=[tab, tab, row, row],
        name="s5_tables",
    )(flat(a_re), flat(a_im), flat(ldt))


def _s5_bbar_kernel(cre_ref, cim_ref, bre_ref, bim_ref, ore_ref, oim_ref):
    cr, ci = cre_ref[...], cim_ref[...]
    br, bi = bre_ref[...], bim_ref[...]
    ore_ref[...] = cr * br - ci * bi
    oim_ref[...] = cr * bi + ci * br


def _s5_bbar(coef_re, coef_im, b_re, b_im):
    shp = jax.ShapeDtypeStruct(b_re.shape, F32)
    return pl.pallas_call(_s5_bbar_kernel, out_shape=[shp, shp], name="s5_bbar")(
        coef_re[..., None], coef_im[..., None], b_re, b_im)


def _lane_order(re, im):
    lead = re.shape[:-2]
    g, p = re.shape[-2:]
    nch = g // SSM_CHUNK_GROUPS
    cw = SSM_CHUNK_GROUPS * p
    both = jnp.stack([re.reshape(lead + (nch, cw)), im.reshape(lead + (nch, cw))], axis=-2)
    return both.reshape(lead + (2 * g * p,))


def _lane_unorder(x, g, p):
    lead = x.shape[:-1]
    nch = g // SSM_CHUNK_GROUPS
    both = x.reshape(lead + (nch, 2, SSM_CHUNK_GROUPS, p))
    return both[..., 0, :, :].reshape(lead + (g, p)), both[..., 1, :, :].reshape(lead + (g, p))


def _block_diag_b(bbar_re, bbar_im):
    g, p, c = bbar_re.shape
    k = SSM_CHUNK_GROUPS
    nch = g // k
    bt = jnp.stack([bbar_re, bbar_im]).reshape(2, nch, k, p, c).transpose(1, 2, 4, 0, 3)
    eye = jnp.eye(k, dtype=F32)
    full = bt[:, :, :, :, None, :] * eye[None, :, None, None, :, None]
    return full.reshape(nch, k * c, 2 * k * p)


def _block_diag_c(c_re, c_im):
    g, c, p = c_re.shape
    k = SSM_CHUNK_GROUPS
    nch = g // k
    ct = jnp.stack([c_re, -c_im]).reshape(2, nch, k, c, p).transpose(1, 0, 2, 4, 3)
    eye = jnp.eye(k, dtype=F32)
    full = ct[:, :, :, :, None, :] * eye[None, None, :, None, :, None]
    return full.reshape(nch, 2 * k * p, k * c)


def _s5_scan_tiles(hbuf, pw_ref, n_tiles, carry_in, carry_ref, state_ref):
    two_n = hbuf.shape[1]
    cw2 = 2 * SSM_CHUNK_GROUPS * SSM_STATE
    cw = cw2 // 2
    rows = lax.broadcasted_iota(jnp.int32, (SUBLANES, SCAN_LANES), 0)
    for ch in range(two_n // cw2):
        for sub in range(cw // SCAN_LANES):
            re0 = ch * cw2 + sub * SCAN_LANES
            im0 = re0 + cw
            sl_re = pl.ds(re0, SCAN_LANES)
            sl_im = pl.ds(im0, SCAN_LANES)
            pr = pw_ref[:, sl_re]
            pi = pw_ref[:, sl_im]
            steps = []
            for sh in (1, 2, 4):
                ar = jnp.where(rows >= sh, pr[sh - 1:sh, :], 0.0)
                ai = jnp.where(rows >= sh, pi[sh - 1:sh, :], 0.0)
                steps.append((sh, ar, ai))

            def body(t, carry, sl_re=sl_re, sl_im=sl_im, pr=pr, pi=pi, steps=steps):
                r0 = pl.multiple_of(t * SUBLANES, SUBLANES)
                xr = hbuf[pl.ds(r0, SUBLANES), sl_re]
                xi = hbuf[pl.ds(r0, SUBLANES), sl_im]
                for sh, ar, ai in steps:
                    sr = pltpu.roll(xr, sh, 0)
                    si = pltpu.roll(xi, sh, 0)
                    xr, xi = xr + ar * sr - ai * si, xi + ar * si + ai * sr
                if carry_in is None:
                    hr, hi = carry
                else:
                    hr, hi = carry_in(t, sl_re), carry_in(t, sl_im)
                xr, xi = xr + pr * hr - pi * hi, xi + pr * hi + pi * hr
                hbuf[pl.ds(r0, SUBLANES), sl_re] = xr
                hbuf[pl.ds(r0, SUBLANES), sl_im] = xi
                if state_ref is not None:
                    state_ref[pl.ds(t, 1), sl_re] = xr[SUBLANES - 1:, :]
                    state_ref[pl.ds(t, 1), sl_im] = xi[SUBLANES - 1:, :]
                return xr[SUBLANES - 1:, :], xi[SUBLANES - 1:, :]

            if carry_in is None:
                init = (carry_ref[:, sl_re], carry_ref[:, sl_im])
            else:
                init = (jnp.zeros((1, SCAN_LANES), F32), jnp.zeros((1, SCAN_LANES), F32))
            hr, hi = lax.fori_loop(0, n_tiles, body, init)
            if carry_in is None:
                carry_ref[:, sl_re] = hr
                carry_ref[:, sl_im] = hi


def _s5_input(u_ref, bhi_ref, blo_ref, hbuf):
    u = u_ref[...]
    u_hi, u_lo = _split2(u)
    kc = bhi_ref.shape[1]
    nc = bhi_ref.shape[2]
    for ch in range(bhi_ref.shape[0]):
        uh = u_hi[:, ch * kc:(ch + 1) * kc]
        ul = u_lo[:, ch * kc:(ch + 1) * kc]
        hbuf[:, ch * nc:(ch + 1) * nc] = _dot(uh, bhi_ref[ch]) + _dot(uh, blo_ref[ch]) + _dot(ul, bhi_ref[ch])
    return u


def _s5_output(u, hbuf, cc_ref, d_ref, wg_ref, bg_ref, o_ref):
    kc = cc_ref.shape[1]
    ys = [_dot(hbuf[:, ch * kc:(ch + 1) * kc].astype(BF16), cc_ref[ch]) for ch in range(cc_ref.shape[0])]
    y = jnp.concatenate(ys, axis=1) + d_ref[...] * u
    z = jax.nn.gelu(y)
    o_ref[...] = z * jax.nn.sigmoid(_dot(z.astype(BF16), wg_ref[...]) + bg_ref[...])


def _s5_prompt_kernel(u_ref, bhi_ref, blo_ref, pw_ref, cc_ref, d_ref, wg_ref, bg_ref, o_ref, st_ref, hbuf, carry_ref):
    ti = pl.program_id(1)

    @pl.when(ti == 0)
    def _():
        carry_ref[...] = jnp.zeros_like(carry_ref)

    u = _s5_input(u_ref, bhi_ref, blo_ref, hbuf)
    _s5_scan_tiles(hbuf, pw_ref, hbuf.shape[0] // SUBLANES, None, carry_ref, None)
    _s5_output(u, hbuf, cc_ref, d_ref, wg_ref, bg_ref, o_ref)

    @pl.when(ti == pl.num_programs(1) - 1)
    def _():
        st_ref[0] = carry_ref[...]


def _s5_sample_kernel(u_ref, h0_ref, bhi_ref, blo_ref, pw_ref, cc_ref, d_ref, wg_ref, bg_ref, o_ref, st_ref, hbuf):
    u = _s5_input(u_ref, bhi_ref, blo_ref, hbuf)
    _s5_scan_tiles(hbuf, pw_ref, hbuf.shape[0] // SUBLANES, lambda t, sl: h0_ref[pl.ds(t, 1), sl], None, st_ref)
    _s5_output(u, hbuf, cc_ref, d_ref, wg_ref, bg_ref, o_ref)


def _s5_weight_specs(bhi, pw, cc, width):
    return [_const_spec(bhi.shape), _const_spec(bhi.shape), _const_spec(pw.shape), _const_spec(cc.shape),
            _const_spec((1, width)), _const_spec((width, width)), _const_spec((1, width))]


def _s5_prompt(u, n_batch, bhi, blo, pw, cc, d_skip, w_glu, b_glu):
    t, width = u.shape
    per = t // n_batch // TOK_BLOCK
    two_n = pw.shape[1]
    tok = pl.BlockSpec((TOK_BLOCK, width), lambda b, i: (b * per + i, 0))
    return pl.pallas_call(
        _s5_prompt_kernel,
        grid=(n_batch, per),
        in_specs=[tok] + _s5_weight_specs(bhi, pw, cc, width),
        out_specs=[tok, pl.BlockSpec((1, 1, two_n), lambda b, i: (b, 0, 0))],
        out_shape=[jax.ShapeDtypeStruct((t, width), F32), jax.ShapeDtypeStruct((n_batch, 1, two_n), F32)],
        scratch_shapes=[pltpu.VMEM((TOK_BLOCK, two_n), F32), pltpu.VMEM((1, two_n), F32)],
        compiler_params=_cparams("arbitrary", "arbitrary"),
        name="s5_prompt",
    )(u, bhi, blo, pw, cc, d_skip, w_glu, b_glu)


def _s5_sample(u, h0, bhi, blo, pw, cc, d_skip, w_glu, b_glu):
    t, width = u.shape
    two_n = pw.shape[1]
    tiles = TOK_BLOCK // SUBLANES
    tok = pl.BlockSpec((TOK_BLOCK, width), lambda i: (i, 0))
    st = pl.BlockSpec((tiles, two_n), lambda i: (i, 0))
    return pl.pallas_call(
        _s5_sample_kernel,
        grid=(t // TOK_BLOCK,),
        in_specs=[tok, st] + _s5_weight_specs(bhi, pw, cc, width),
        out_specs=[tok, st],
        out_shape=[jax.ShapeDtypeStruct((t, width), F32), jax.ShapeDtypeStruct(h0.shape, F32)],
        scratch_shapes=[pltpu.VMEM((TOK_BLOCK, two_n), F32)],
        compiler_params=_cparams("arbitrary"),
        name="s5_sample",
    )(u, h0, bhi, blo, pw, cc, d_skip, w_glu, b_glu)


def _outproj_kernel(att_ref, ssm_ref, ga_ref, gs_ref, wa_ref, ws_ref, x_ref, gate_ref, o_ref):
    def norm(v, g):
        return (v * lax.rsqrt(jnp.mean(v * v, axis=-1, keepdims=True) + EPS) * g).astype(BF16)

    merged = _dot(norm(att_ref[...], ga_ref[...]), wa_ref[...]) + _dot(norm(ssm_ref[...], gs_ref[...]), ws_ref[...])
    tg, _, d = x_ref.shape
    o_ref[...] = x_ref[...] + gate_ref[...] * merged.reshape(tg, SUBLANES, d)


def _out_projection(att, ssm, g_att, g_ssm, w_att, w_ssm, xg, gate):
    ng, _, d = xg.shape
    tg = TOK_BLOCK // SUBLANES
    steps = ng // tg
    wa = att.shape[1]
    ws = ssm.shape[1]
    return pl.pallas_call(
        _outproj_kernel,
        grid=(steps,),
        in_specs=[pl.BlockSpec((TOK_BLOCK, wa), lambda i: (i, 0)), pl.BlockSpec((TOK_BLOCK, ws), lambda i: (i, 0)),
                  _const_spec((1, wa)), _const_spec((1, ws)), _const_spec((wa, d)), _const_spec((ws, d)),
                  pl.BlockSpec((tg, SUBLANES, d), lambda i: (i, 0, 0)), _mod_spec(gate.shape[0], steps, tg, d)],
        out_specs=pl.BlockSpec((tg, SUBLANES, d), lambda i: (i, 0, 0)),
        out_shape=jax.ShapeDtypeStruct(xg.shape, F32),
        compiler_params=_cparams("arbitrary"),
        name="out_projection",
    )(att, ssm, g_att, g_ssm, w_att, w_ssm, xg, gate)


def _topk_rows(s, k, aux, val_ref, aux_ref):
    n = s.shape[0]
    row = lax.broadcasted_iota(jnp.int32, s.shape, 0).astype(F32)
    for i in range(k):
        m = jnp.max(s, axis=0, keepdims=True)
        j = jnp.min(jnp.where(s == m, row, float(n)), axis=0, keepdims=True)
        hit = row == j
        val_ref[i:i + 1, :] = m
        aux_ref[i:i + 1, :] = j if aux is None else jnp.sum(jnp.where(hit, aux, 0.0), axis=0, keepdims=True)
        s = jnp.where(hit, -jnp.inf, s)


def _peer_select_kernel(x_ref, shift_ref, scale_ref, g_ref, wqt_ref, sk_ref, h_ref, e_ref, gw_ref,
                        qt_sc, val_sc, idx_sc, cs_sc, ce_sc):
    tg, _, d = x_ref.shape
    h = _rms_mod(x_ref[...], g_ref[...], shift_ref[...], scale_ref[...]).reshape(tg * SUBLANES, d).astype(BF16)
    h_ref[...] = h
    qt_sc[...] = _dot_nt(wqt_ref[...], h).astype(BF16)
    n_sides, n_keys, half = sk_ref.shape
    k = PEER_TOPK
    for c in range(tg * SUBLANES // LANES):
        cols = pl.ds(c * LANES, LANES)

        def head_body(head, _, cols=cols):
            for side in range(2):
                hx = 2 * head + side
                r0 = pl.multiple_of(hx * half, half)
                s = _dot(sk_ref[hx], qt_sc[pl.ds(r0, half), cols])
                _topk_rows(s, k, None, val_sc.at[side], idx_sc.at[side])
            v2 = val_sc[1]
            i2 = idx_sc[1]
            for a in range(k):
                cs_sc[a * k:(a + 1) * k, :] = val_sc[0, a:a + 1, :] + v2
                ce_sc[a * k:(a + 1) * k, :] = idx_sc[0, a:a + 1, :] * float(n_keys) + i2
            _topk_rows(cs_sc[...], k, ce_sc[...], val_sc.at[0], idx_sc.at[0])
            ts = val_sc[0]
            p = jnp.exp(ts - ts[0:1, :])
            r_out = pl.multiple_of(head * k, k)
            gw_ref[pl.ds(r_out, k), cols] = p / jnp.sum(p, axis=0, keepdims=True)
            e_ref[pl.ds(r_out, k), cols] = idx_sc[0].astype(jnp.int32)
            return 0

        lax.fori_loop(0, n_sides // 2, head_body, 0)


def _peer_select(xg, shift, scale, g, w_q_t, subkeys):
    ng, _, d = xg.shape
    t = ng * SUBLANES
    tg = TOK_BLOCK // SUBLANES
    steps = ng // tg
    n_sides = subkeys.shape[0]
    k = PEER_TOPK
    rows = (n_sides // 2) * k
    mod_spec = _mod_spec(shift.shape[0], steps, tg, d)
    return pl.pallas_call(
        _peer_select_kernel,
        grid=(steps,),
        in_specs=[pl.BlockSpec((tg, SUBLANES, d), lambda i: (i, 0, 0)), mod_spec, mod_spec, _const_spec((1, d)),
                  _const_spec(w_q_t.shape), _const_spec(subkeys.shape)],
        out_specs=[pl.BlockSpec((TOK_BLOCK, d), lambda i: (i, 0)), pl.BlockSpec((rows, TOK_BLOCK), lambda i: (0, i)),
                   pl.BlockSpec((rows, TOK_BLOCK), lambda i: (0, i))],
        out_shape=[jax.ShapeDtypeStruct((t, d), BF16), jax.ShapeDtypeStruct((rows, t), jnp.int32),
                   jax.ShapeDtypeStruct((rows, t), F32)],
        scratch_shapes=[pltpu.VMEM((w_q_t.shape[0], TOK_BLOCK), BF16),
                        pltpu.VMEM((2, k, LANES), F32), pltpu.VMEM((2, k, LANES), F32),
                        pltpu.VMEM((k * k, LANES), F32), pltpu.VMEM((k * k, LANES), F32)],
        compiler_params=_cparams("arbitrary"),
        name="peer_select",
    )(xg, shift, scale, g, w_q_t, subkeys)


def _peer_mask_kernel(e_ref, g_ref, o_ref):
    n_tiles = o_ref.shape[0]
    n2 = o_ref.shape[2]
    n1 = n_tiles * SUBLANES
    shift = n2.bit_length() - 1
    sub1 = lax.broadcasted_iota(jnp.int32, (n1, e_ref.shape[1]), 0)
    sub2 = lax.broadcasted_iota(jnp.int32, (n2, e_ref.shape[1]), 0)

    def body(t, _):
        e = e_ref[pl.ds(t, 1), :]
        gate = g_ref[pl.ds(t, 1), :]
        first = jnp.right_shift(e, shift)
        second = jnp.bitwise_and(e, n2 - 1)
        p_hi, p_lo = _split2(jnp.where(sub1 == first, gate, 0.0))
        r = jnp.where(sub2 == second, 1.0, 0.0).astype(BF16)
        m = _dot_nt(p_hi, r) + _dot_nt(p_lo, r)
        r0 = pl.multiple_of(t * SUBLANES, SUBLANES)
        for jb in range(n_tiles):
            o_ref[jb, pl.ds(r0, SUBLANES), :] = m[jb * SUBLANES:(jb + 1) * SUBLANES, :]
        return 0

    lax.fori_loop(0, e_ref.shape[0], body, 0)


def _peer_mask(e_tok, g_tok):
    t, picks = e_tok.shape
    n = PEER_N_KEYS
    assert n & (n - 1) == 0
    tb = MASK_TOK_BLOCK
    return pl.pallas_call(
        _peer_mask_kernel,
        grid=(t // tb,),
        in_specs=[pl.BlockSpec((tb, picks), lambda i: (i, 0)), pl.BlockSpec((tb, picks), lambda i: (i, 0))],
        out_specs=pl.BlockSpec((n // SUBLANES, tb * SUBLANES, n), lambda i: (0, i, 0)),
        out_shape=jax.ShapeDtypeStruct((n // SUBLANES, t * SUBLANES, n), F32),
        compiler_params=_cparams("arbitrary"),
        name="peer_mask",
    )(e_tok, g_tok)


def _peer_ffn_kernel(h_ref, u_ref, v_ref, m_ref, x_ref, gate_ref, gf_ref, o_ref, acc_ref):
    j = pl.program_id(1)

    @pl.when(j == 0)
    def _():
        acc_ref[...] = jnp.zeros_like(acc_ref)

    tm = h_ref.shape[0]
    n2 = m_ref.shape[2]
    a = _dot_nt(h_ref[...], u_ref[...])
    m2 = m_ref.at[0]
    parts = []
    for i1 in range(SUBLANES):
        gate = m2[pl.ds(i1, tm, stride=SUBLANES), :]
        parts.append((gate * jax.nn.gelu(a[:, i1 * n2:(i1 + 1) * n2])).astype(BF16))
    acc_ref[...] += _dot(jnp.concatenate(parts, axis=1), v_ref[...])

    @pl.when(j == pl.num_programs(1) - 1)
    def _():
        tg, _, d = x_ref.shape
        x2 = x_ref[...] + gate_ref[...] * acc_ref[...].reshape(tg, SUBLANES, d)
        ms = jnp.mean(x2 * x2, axis=-1, keepdims=True)
        o_ref[...] = x2 * lax.rsqrt(ms + EPS) * gf_ref[...]


def _peer_ffn(h, exp_u, exp_v, mask, xg, gate, g_final):
    t, d = h.shape
    ne = exp_u.shape[0]
    n = PEER_N_KEYS
    tm = PEER_TOK_BLOCK
    te = PEER_EXPERT_TILE
    assert te == SUBLANES * n and mask.shape == (ne // te, t * SUBLANES, n)
    tg = tm // SUBLANES
    steps = t // tm
    return pl.pallas_call(
        _peer_ffn_kernel,
        grid=(steps, ne // te),
        in_specs=[pl.BlockSpec((tm, d), lambda i, j: (i, 0)),
                  pl.BlockSpec((te, d), lambda i, j: (j, 0)), pl.BlockSpec((te, d), lambda i, j: (j, 0)),
                  pl.BlockSpec((1, tm * SUBLANES, n), lambda i, j: (j, i, 0)),
                  pl.BlockSpec((tg, SUBLANES, d), lambda i, j: (i, 0, 0)),
                  _mod_spec(gate.shape[0], steps, tg, d), _const_spec((1, d))],
        out_specs=pl.BlockSpec((tg, SUBLANES, d), lambda i, j: (i, 0, 0)),
        out_shape=jax.ShapeDtypeStruct(xg.shape, F32),
        scratch_shapes=[pltpu.VMEM((tm, d), F32)],
        compiler_params=_cparams("arbitrary", "arbitrary"),
        name="peer_ffn",
    )(h, exp_u, exp_v, mask, xg, gate, g_final)


def _token_stages_pre(xg, mods, lw):
    shift1, scale1 = mods[0], mods[1]
    return _in_projection(xg, shift1, scale1, lw["g_norm1"], lw["w_cat"], lw["bf_pad"], lw["fox_width"])


def _token_stages_post(xg, mods, att, ssm, lw, g_final):
    gate1, shift2, scale2, gate2 = mods[2], mods[3], mods[4], mods[5]
    x1 = _out_projection(att, ssm, lw["g_attn_out"], lw["g_ssm_out"], lw["w_out_att"], lw["w_out_ssm"], xg, gate1)
    h2, e_t, g_t = _peer_select(x1, shift2, scale2, lw["g_norm2"], lw["peer_w_q_t"], lw["peer_subkeys"])
    mask = _peer_mask(e_t.T, g_t.T)
    return _peer_ffn(h2, lw["peer_u"], lw["peer_v"], mask, x1, gate2, g_final)


def kernel(x_prompt, x_sample, cache_k, cache_v, cache_logf, state_ssm_re, state_ssm_im, page_table, c_prompt, c_sample, w_ada, b_ada, g_norm1, g_norm2, w_in, b_forget, ssm_a_re, ssm_a_im, ssm_log_dt, ssm_b_re, ssm_b_im, ssm_c_re, ssm_c_im, ssm_d, w_glu, b_glu, g_attn_out, g_ssm_out, w_out, peer_w_q, peer_subkeys, peer_u, peer_v, g_final):
    depth = w_ada.shape[0]
    assert depth == 1, "single trunk layer"
    nb, seq, d = x_prompt.shape
    nd, dseq, _ = x_sample.shape
    assert dseq == SUBLANES
    n_pool, page, n_heads, hd = cache_k.shape[1:]
    fox_w = n_heads * hd
    n_groups, n_state = ssm_a_re.shape[1:]
    ssm_w = n_groups * SSM_CH_PER_GROUP
    assert fox_w == ssm_w
    n_pages = page_table.shape[1]
    layer = 0

    w_in_l = w_in[layer]
    w_cat = jnp.concatenate([w_in_l[:, :3 * fox_w], w_in_l[:, 3 * fox_w + n_heads:], w_in_l[:, 3 * fox_w:3 * fox_w + n_heads],
                             jnp.zeros((d, LANES - n_heads), F32)], axis=1).astype(BF16)
    bf_pad = jnp.concatenate([b_forget[layer], jnp.zeros((LANES - n_heads,), F32)]).reshape(1, LANES)
    w_out_l = w_out[layer].astype(BF16)
    lw = dict(
        fox_width=fox_w, w_cat=w_cat, bf_pad=bf_pad,
        g_norm1=g_norm1[layer].reshape(1, d), g_norm2=g_norm2[layer].reshape(1, d),
        g_attn_out=g_attn_out[layer].reshape(1, fox_w), g_ssm_out=g_ssm_out[layer].reshape(1, ssm_w),
        w_out_att=w_out_l[:fox_w], w_out_ssm=w_out_l[fox_w:],
        peer_w_q_t=peer_w_q[layer].T.astype(BF16),
        peer_subkeys=peer_subkeys[layer].reshape((-1,) + peer_subkeys.shape[-2:]).astype(BF16),
        peer_u=peer_u[layer].astype(BF16), peer_v=peer_v[layer].astype(BF16),
    )
    gf = g_final.reshape(1, d)

    n_c = nb + nd
    n_c_pad = -(-n_c // SUBLANES) * SUBLANES
    c_all = jnp.concatenate([c_prompt, c_sample, jnp.zeros((n_c_pad - n_c, d), F32)], axis=0)
    mod = _ada_modulation(c_all, w_ada[layer], b_ada[layer])
    mods_p = [mod[:nb, i * d:(i + 1) * d].reshape(nb, 1, d) for i in range(N_MOD)]
    mods_s = [mod[nb:n_c, i * d:(i + 1) * d].reshape(nd, 1, d) for i in range(N_MOD)]

    pw_re, pw_im, coef_re, coef_im = _s5_tables(ssm_a_re[layer], ssm_a_im[layer], ssm_log_dt[layer])
    bbar_re, bbar_im = _s5_bbar(coef_re.reshape(n_groups, n_state), coef_im.reshape(n_groups, n_state),
                                ssm_b_re[layer], ssm_b_im[layer])
    b_blk = _block_diag_b(bbar_re, bbar_im)
    b_hi = b_blk.astype(BF16)
    b_lo = (b_blk - b_hi.astype(F32)).astype(BF16)
    c_blk = _block_diag_c(ssm_c_re[layer], ssm_c_im[layer]).astype(BF16)
    pw = _lane_order(pw_re.reshape(SUBLANES, n_groups, n_state), pw_im.reshape(SUBLANES, n_groups, n_state))
    s5w = (b_hi, b_lo, pw, c_blk, ssm_d[layer].reshape(1, ssm_w), w_glu[layer].astype(BF16), b_glu[layer].reshape(1, ssm_w))

    xg_p = x_prompt.reshape(nb * seq // SUBLANES, SUBLANES, d)
    q_p, k_p, v_p, u_p, lf_p = _token_stages_pre(xg_p, mods_p, lw)
    fcum = _cumsum_time(lf_p, nb).reshape(nb, seq, LANES)[:, :, :n_heads].transpose(0, 2, 1)
    fq = jnp.broadcast_to(fcum[..., None], (nb, n_heads, seq, LANES))
    fk = fcum.reshape(nb, n_heads, seq // ATT_BLOCK, ATT_BLOCK)
    as3 = lambda a: a.astype(BF16).reshape(nb, seq, fox_w)
    att_p = _prompt_attention(as3(q_p), as3(k_p), as3(v_p), fq, fk, n_heads).reshape(nb * seq, fox_w)
    ssm_p, st_p = _s5_prompt(u_p, nb, *s5w)
    y_p = _token_stages_post(xg_p, mods_p, att_p, ssm_p, lw, gf)
    re_p, im_p = _lane_unorder(st_p.reshape(nb, -1), n_groups, n_state)

    xg_s = x_sample.reshape(nd, SUBLANES, d)
    q_s, k_s, v_s, u_s, lf_s = _token_stages_pre(xg_s, mods_s, lw)
    rows = dseq * n_heads
    heads_rows = lambda a: a.reshape(nd, dseq, n_heads, hd).reshape(nd, rows, hd)
    logf_pages = cache_logf[layer].reshape(n_pool, page * n_heads)
    dloc, tot = _page_sums(logf_pages, n_heads)
    c_new = _cumsum_short(lf_s.reshape(nd, dseq, LANES))[:, :, :n_heads].reshape(nd, rows)
    cq_col = jnp.broadcast_to(c_new[:, :, None], (nd, rows, LANES))
    cq_row = c_new.reshape(nd, 1, rows)
    att_s = _sample_attention(page_table, heads_rows(q_s), heads_rows(k_s), heads_rows(v_s),
                              cache_k[layer].reshape(n_pool, page * n_heads, hd),
                              cache_v[layer].reshape(n_pool, page * n_heads, hd),
                              dloc.reshape(n_pool, 1, page * n_heads), tot.reshape(n_pool, 1, page * n_heads),
                              cq_col, cq_row, n_heads)
    att_s = att_s.reshape(nd, dseq, n_heads, hd).reshape(nd * dseq, fox_w)
    h0 = _lane_order(state_ssm_re[layer], state_ssm_im[layer])
    ssm_s, st_s = _s5_sample(u_s, h0, *s5w)
    y_s = _token_stages_post(xg_s, mods_s, att_s, ssm_s, lw, gf)
    re_s, im_s = _lane_unorder(st_s, n_groups, n_state)

    kv5 = lambda a, b_, l_: a.reshape(1, b_, l_, n_heads, hd)
    return (y_p.reshape(nb, seq, d), y_s.reshape(nd, dseq, d),
            kv5(k_p, nb, seq), kv5(v_p, nb, seq), lf_p[:, :n_heads].reshape(1, nb, seq, n_heads),
            re_p[None], im_p[None],
            kv5(k_s, nd, dseq), kv5(v_s, nd, dseq), lf_s[:, :n_heads].reshape(1, nd, dseq, n_heads),
            re_s[None], im_s[None])
```

```python
import functools
import math

import jax
import jax.numpy as jnp
from jax import lax
from jax.experimental import pallas as pl
from jax.experimental.pallas import tpu as pltpu

F32 = jnp.float32
BF16 = jnp.bfloat16
EPS = 1e-6
NEG = -1e30

SUBLANES = 8
LANES = 128
VMEM_LIMIT_BYTES = 56 * 1024 * 1024

N_MOD = 6
FOX_HEAD_DIM = 128
SSM_CH_PER_GROUP = 16
SSM_STATE = 64
SSM_CHUNK_GROUPS = 16
PEER_N_KEYS = 128
PEER_TOPK = 16
PEER_EXPERT_TILE = 1024

TOK_BLOCK = 256
ATT_BLOCK = 256
ATT_HEAD_GROUP = 4
LOG2E = 1.4426950408889634
PEER_TOK_BLOCK = 512
MASK_TOK_BLOCK = 64
SCAN_LANES = 512


def _cparams(*sem):
    return pltpu.CompilerParams(dimension_semantics=sem, vmem_limit_bytes=VMEM_LIMIT_BYTES)


def _const_spec(shape):
    nd = len(shape)
    return pl.BlockSpec(shape, lambda *_: (0,) * nd)


def _split2(x):
    hi = x.astype(BF16)
    lo = (x - hi.astype(F32)).astype(BF16)
    return hi, lo


def _split3(x):
    x1 = x.astype(BF16)
    r1 = x - x1.astype(F32)
    x2 = r1.astype(BF16)
    x3 = (r1 - x2.astype(F32)).astype(BF16)
    return x1, x2, x3


def _dot(a, b):
    return jnp.dot(a, b, preferred_element_type=F32)


def _dot_nt(a, b):
    return lax.dot_general(a, b, (((1,), (1,)), ((), ())), preferred_element_type=F32)


def _rms_mod(xg, g, shift, scale):
    ms = jnp.mean(xg * xg, axis=-1, keepdims=True)
    y = xg * lax.rsqrt(ms + EPS) * g
    return y * (1.0 + scale) + shift


def _mod_spec(n_mod_rows, n_steps, tg, d):
    if n_mod_rows == n_steps * tg:
        return pl.BlockSpec((tg, 1, d), lambda i, *_: (i, 0, 0))
    assert n_steps % n_mod_rows == 0
    per = n_steps // n_mod_rows
    return pl.BlockSpec((1, 1, d), lambda i, *_: (i // per, 0, 0))


def _ada_kernel(c_ref, w_ref, b_ref, o_ref):
    a = jax.nn.silu(c_ref[...]).astype(BF16)
    o_ref[...] = _dot(a, w_ref[...].astype(BF16)) + b_ref[...]


def _ada_modulation(c_all, w_ada, b_ada):
    m, d = c_all.shape
    n = w_ada.shape[1]
    tn = 1024
    return pl.pallas_call(
        _ada_kernel,
        grid=(n // tn,),
        in_specs=[_const_spec((m, d)), pl.BlockSpec((d, tn), lambda j: (0, j)), pl.BlockSpec((1, tn), lambda j: (0, j))],
        out_specs=pl.BlockSpec((m, tn), lambda j: (0, j)),
        out_shape=jax.ShapeDtypeStruct((m, n), F32),
        compiler_params=_cparams("arbitrary"),
        name="ada_modulation",
    )(c_all, w_ada, b_ada.reshape(1, n))


def _inproj_kernel(x_ref, shift_ref, scale_ref, g_ref, w_ref, bf_ref, q_ref, k_ref, v_ref, u_ref, lf_ref):
    tg, _, d = x_ref.shape
    h = _rms_mod(x_ref[...], g_ref[...], shift_ref[...], scale_ref[...])
    r = _dot(h.reshape(tg * SUBLANES, d).astype(BF16), w_ref[...])
    w = q_ref.shape[1]
    q_ref[...] = r[:, 0:w]
    k_ref[...] = r[:, w:2 * w]
    v_ref[...] = r[:, 2 * w:3 * w]
    u_ref[...] = r[:, 3 * w:4 * w]
    lf_ref[...] = jax.nn.log_sigmoid(r[:, 4 * w:] + bf_ref[...])


def _in_projection(xg, shift, scale, g, w_cat, bf_pad, width):
    ng, _, d = xg.shape
    t = ng * SUBLANES
    tg = TOK_BLOCK // SUBLANES
    steps = ng // tg
    n = w_cat.shape[1]
    mod_spec = _mod_spec(shift.shape[0], steps, tg, d)
    tok = lambda w: pl.BlockSpec((TOK_BLOCK, w), lambda i: (i, 0))
    return pl.pallas_call(
        _inproj_kernel,
        grid=(steps,),
        in_specs=[pl.BlockSpec((tg, SUBLANES, d), lambda i: (i, 0, 0)), mod_spec, mod_spec,
                  _const_spec((1, d)), _const_spec((d, n)), _const_spec((1, LANES))],
        out_specs=[tok(width), tok(width), tok(width), tok(width), tok(LANES)],
        out_shape=[jax.ShapeDtypeStruct((t, width), F32)] * 4 + [jax.ShapeDtypeStruct((t, LANES), F32)],
        compiler_params=_cparams("arbitrary"),
        name="in_projection",
    )(xg, shift, scale, g, w_cat, bf_pad)


def _cumsum_kernel(x_ref, tri_ref, o_ref, carry_ref):
    @pl.when(pl.program_id(1) == 0)
    def _():
        carry_ref[...] = jnp.zeros_like(carry_ref)

    x1, x2, x3 = _split3(x_ref[...])
    tri = tri_ref[...]
    loc = _dot(tri, x1) + _dot(tri, x2) + _dot(tri, x3) + carry_ref[...]
    o_ref[...] = loc
    carry_ref[...] = loc[-1:, :]


def _cumsum_time(x, n_batch):
    t, w = x.shape
    l = t // n_batch
    tb = 256
    tri = (lax.broadcasted_iota(jnp.int32, (tb, tb), 1) <= lax.broadcasted_iota(jnp.int32, (tb, tb), 0)).astype(BF16)
    per = l // tb
    return pl.pallas_call(
        _cumsum_kernel,
        grid=(n_batch, per),
        in_specs=[pl.BlockSpec((tb, w), lambda b, i: (b * per + i, 0)), _const_spec((tb, tb))],
        out_specs=pl.BlockSpec((tb, w), lambda b, i: (b * per + i, 0)),
        out_shape=jax.ShapeDtypeStruct((t, w), F32),
        scratch_shapes=[pltpu.VMEM((1, w), F32)],
        compiler_params=_cparams("arbitrary", "arbitrary"),
        name="logf_cumsum",
    )(x, tri)


def _cumsum_short_kernel(x_ref, o_ref):
    run = x_ref[:, 0:1, :]
    o_ref[:, 0:1, :] = run
    for r in range(1, x_ref.shape[1]):
        run = run + x_ref[:, r:r + 1, :]
        o_ref[:, r:r + 1, :] = run


def _cumsum_short(x):
    return pl.pallas_call(_cumsum_short_kernel, out_shape=jax.ShapeDtypeStruct(x.shape, F32), name="logf_cumsum_new")(x)


def _pattn_kernel(q_ref, k_ref, v_ref, fq_ref, fk_ref, o_ref, m_sc, l_sc, acc_sc, *, scale):
    qi = pl.program_id(2)
    tq = q_ref.shape[1]
    n_group, _, hd = acc_sc.shape
    m_sc[...] = jnp.full_like(m_sc, NEG)
    l_sc[...] = jnp.zeros_like(l_sc)
    acc_sc[...] = jnp.zeros_like(acc_sc)
    c = scale * LOG2E

    def block(kj, diagonal):
        start = pl.multiple_of(kj * tq, tq)
        for g in range(n_group):
            hs = slice(g * hd, (g + 1) * hd)
            s = _dot_nt(q_ref[0, :, hs], k_ref[0, pl.ds(start, tq), hs])
            fq2 = fq_ref[0, g] * LOG2E
            fk2 = fk_ref[0, g, pl.ds(kj, 1), :] * LOG2E
            tiles = []
            for t in range(tq // LANES):
                ls = slice(t * LANES, (t + 1) * LANES)
                st = s[:, ls] * c + (fq2 - fk2[:, ls])
                if diagonal:
                    row = lax.broadcasted_iota(jnp.int32, st.shape, 0)
                    col = lax.broadcasted_iota(jnp.int32, st.shape, 1) + t * LANES
                    st = jnp.where(col <= row, st, NEG)
                tiles.append(st)
            m_prev = m_sc[g]
            m_new = jnp.maximum(m_prev, jnp.max(functools.reduce(jnp.maximum, tiles), axis=1, keepdims=True))
            alpha = jnp.exp2(m_prev - m_new)
            ps = [jnp.exp2(st - m_new) for st in tiles]
            l_sc[g] = alpha * l_sc[g] + functools.reduce(jnp.add, ps)
            p = jnp.concatenate([x.astype(BF16) for x in ps], axis=1)
            acc_sc[g] = alpha * acc_sc[g] + _dot(p, v_ref[0, pl.ds(start, tq), hs])
            m_sc[g] = m_new

    def off_diagonal(kj, carry):
        block(kj, False)
        return carry

    lax.fori_loop(0, qi, off_diagonal, 0)
    block(qi, True)
    for g in range(n_group):
        o_ref[0, :, g * hd:(g + 1) * hd] = acc_sc[g] / jnp.sum(l_sc[g], axis=1, keepdims=True)


def _prompt_attention(q, k, v, fq, fk, n_heads):
    b, l, w = q.shape
    hd = w // n_heads
    tq = ATT_BLOCK
    ng = ATT_HEAD_GROUP
    assert fk.shape == (b, n_heads, l // tq, tq) and n_heads % ng == 0
    return pl.pallas_call(
        functools.partial(_pattn_kernel, scale=hd ** -0.5),
        grid=(b, n_heads // ng, l // tq),
        in_specs=[pl.BlockSpec((1, tq, ng * hd), lambda bi, h, i: (bi, i, h)),
                  pl.BlockSpec((1, l, ng * hd), lambda bi, h, i: (bi, 0, h)),
                  pl.BlockSpec((1, l, ng * hd), lambda bi, h, i: (bi, 0, h)),
                  pl.BlockSpec((1, ng, tq, LANES), lambda bi, h, i: (bi, h, i, 0)),
                  pl.BlockSpec((1, ng, l // tq, tq), lambda bi, h, i: (bi, h, 0, 0))],
        out_specs=pl.BlockSpec((1, tq, ng * hd), lambda bi, h, i: (bi, i, h)),
        out_shape=jax.ShapeDtypeStruct((b, l, w), F32),
        scratch_shapes=[pltpu.VMEM((ng, tq, LANES), F32), pltpu.VMEM((ng, tq, LANES), F32), pltpu.VMEM((ng, tq, hd), F32)],
        compiler_params=_cparams("arbitrary", "arbitrary", "arbitrary"),
        name="prompt_attention",
    )(q, k, v, fq, fk)


def _page_sums_kernel(x_ref, m_after_ref, m_all_ref, dloc_ref, tot_ref):
    x1, x2, x3 = _split3(x_ref[...])
    ma = m_after_ref[...]
    mt = m_all_ref[...]
    dloc_ref[...] = _dot(x1, ma) + _dot(x2, ma) + _dot(x3, ma)
    tot_ref[...] = _dot(x1, mt) + _dot(x2, mt) + _dot(x3, mt)


def _page_sums(logf_pages, n_heads):
    n_pool, w = logf_pages.shape
    r = lax.broadcasted_iota(jnp.int32, (w, w), 0)
    c = lax.broadcasted_iota(jnp.int32, (w, w), 1)
    same_head = (r % n_heads) == (c % n_heads)
    m_after = (same_head & (r // n_heads > c // n_heads)).astype(BF16)
    m_all = same_head.astype(BF16)
    tp = 256
    return pl.pallas_call(
        _page_sums_kernel,
        grid=(n_pool // tp,),
        in_specs=[pl.BlockSpec((tp, w), lambda i: (i, 0)), _const_spec((w, w)), _const_spec((w, w))],
        out_specs=[pl.BlockSpec((tp, w), lambda i: (i, 0))] * 2,
        out_shape=[jax.ShapeDtypeStruct((n_pool, w), F32)] * 2,
        compiler_params=_cparams("arbitrary"),
        name="page_logf_sums",
    )(logf_pages, m_after, m_all)


def _lane_fold(x, op):
    out = x[:, 0:LANES]
    for c in range(1, x.shape[1] // LANES):
        out = op(out, x[:, c * LANES:(c + 1) * LANES])
    return out


def _sattn_kernel(pt_ref, q_ref, kn_ref, vn_ref, cqc_ref, cqr_ref, mb_ref, mbn_ref, ck_hbm, cv_hbm, dloc_hbm, tot_hbm,
                  o_ref, kbuf, vbuf, dbuf, tbuf, sbuf, sems, *, scale):
    b = pl.program_id(0)
    n_pages = kbuf.shape[1]
    slot = lax.rem(b, 2)

    def page_copies(bb, sl, i):
        p = pt_ref[bb, i]
        return (pltpu.make_async_copy(ck_hbm.at[p], kbuf.at[sl, i], sems.at[sl, 0]),
                pltpu.make_async_copy(cv_hbm.at[p], vbuf.at[sl, i], sems.at[sl, 1]),
                pltpu.make_async_copy(dloc_hbm.at[p], dbuf.at[sl, i], sems.at[sl, 2]),
                pltpu.make_async_copy(tot_hbm.at[p], tbuf.at[sl, i], sems.at[sl, 3]))

    def fetch(bb, sl):
        for i in range(n_pages):
            for cp in page_copies(bb, sl, i):
                cp.start()

    @pl.when(b == 0)
    def _():
        fetch(0, 0)

    @pl.when(b + 1 < pl.num_programs(0))
    def _():
        fetch(b + 1, 1 - slot)

    for i in range(n_pages):
        for cp in page_copies(b, slot, i):
            cp.wait()

    q = q_ref[0].astype(BF16)
    cq = cqc_ref[0][:, 0:1]
    base = mb_ref[...] + cq

    def scores(k, carry):
        after, mrun = carry
        i = n_pages - 1 - k
        s = _dot_nt(q, kbuf[slot, i].astype(BF16)) * scale + base + (dbuf[slot, i] + after)
        sbuf[i] = s
        return after + tbuf[slot, i], jnp.maximum(mrun, _lane_fold(s, jnp.maximum))

    rows = q.shape[0]
    init = (jnp.zeros(dbuf.shape[2:], F32), jnp.full((rows, LANES), NEG, F32))
    _, mrun = lax.fori_loop(0, n_pages, scores, init, unroll=2)
    sn = _dot_nt(q, kn_ref[0].astype(BF16)) * scale + (cq - cqr_ref[0]) + mbn_ref[...]
    m = jnp.maximum(jnp.max(mrun, axis=1, keepdims=True), jnp.max(sn, axis=1, keepdims=True))

    def values(i, carry):
        lrun, acc = carry
        p = jnp.exp(sbuf[i] - m)
        return lrun + _lane_fold(p, jnp.add), acc + _dot(p.astype(BF16), vbuf[slot, i].astype(BF16))

    init = (jnp.zeros((rows, LANES), F32), jnp.zeros((rows, q.shape[1]), F32))
    lrun, acc = lax.fori_loop(0, n_pages, values, init, unroll=2)
    pn = jnp.exp(sn - m)
    l = jnp.sum(lrun, axis=1, keepdims=True) + jnp.sum(pn, axis=1, keepdims=True)
    acc = acc + _dot(pn.astype(BF16), vn_ref[0].astype(BF16))
    o_ref[0] = acc / l


def _sample_attention(page_table, q, kn, vn, cache_k, cache_v, dloc, tot, cq_col, cq_row, n_heads):
    b, rows, hd = q.shape
    n_pages = page_table.shape[1]
    cols = cache_k.shape[1]
    r = lax.broadcasted_iota(jnp.int32, (rows, cols), 0)
    c = lax.broadcasted_iota(jnp.int32, (rows, cols), 1)
    mask_past = jnp.where((r % n_heads) == (c % n_heads), 0.0, NEG).astype(F32)
    rn = lax.broadcasted_iota(jnp.int32, (rows, rows), 0)
    cn = lax.broadcasted_iota(jnp.int32, (rows, rows), 1)
    mask_new = jnp.where(((rn % n_heads) == (cn % n_heads)) & (cn // n_heads <= rn // n_heads), 0.0, NEG).astype(F32)
    per_b = lambda bi, pt: (bi, 0, 0)
    const2 = lambda bi, pt: (0, 0)
    hbm = pl.BlockSpec(memory_space=pltpu.HBM)
    grid_spec = pltpu.PrefetchScalarGridSpec(
        num_scalar_prefetch=1,
        grid=(b,),
        in_specs=[pl.BlockSpec((1, rows, hd), per_b), pl.BlockSpec((1, rows, hd), per_b), pl.BlockSpec((1, rows, hd), per_b),
                  pl.BlockSpec((1, rows, LANES), per_b), pl.BlockSpec((1, 1, rows), per_b),
                  pl.BlockSpec((rows, cols), const2), pl.BlockSpec((rows, rows), const2),
                  hbm, hbm, hbm, hbm],
        out_specs=pl.BlockSpec((1, rows, hd), per_b),
        scratch_shapes=[pltpu.VMEM((2, n_pages, cols, hd), F32), pltpu.VMEM((2, n_pages, cols, hd), F32),
                        pltpu.VMEM((2, n_pages, 1, cols), F32), pltpu.VMEM((2, n_pages, 1, cols), F32),
                        pltpu.VMEM((n_pages, rows, cols), F32), pltpu.SemaphoreType.DMA((2, 4))],
    )
    return pl.pallas_call(
        functools.partial(_sattn_kernel, scale=hd ** -0.5),
        grid_spec=grid_spec,
        out_shape=jax.ShapeDtypeStruct((b, rows, hd), F32),
        compiler_params=_cparams("arbitrary"),
        name="sample_attention",
    )(page_table, q, kn, vn, cq_col, cq_row, mask_past, mask_new, cache_k, cache_v, dloc, tot)


def _s5_tables_kernel(are_ref, aim_ref, ldt_ref, pre_ref, pim_ref, cre_ref, cim_ref):
    a_re = jnp.minimum(are_ref[...], -1e-4)
    a_im = aim_ref[...]
    dt = jnp.exp(ldt_ref[...])
    mag = jnp.exp(a_re * dt)
    abr = mag * jnp.cos(a_im * dt)
    abi = mag * jnp.sin(a_im * dt)
    den = a_re * a_re + a_im * a_im
    nr, ni = abr - 1.0, abi
    cre_ref[...] = (nr * a_re + ni * a_im) / den
    cim_ref[...] = (ni * a_re - nr * a_im) / den
    pr, pi = abr, abi
    for r in range(SUBLANES):
        pre_ref[r:r + 1, :] = pr
        pim_ref[r:r + 1, :] = pi
        pr, pi = pr * abr - pi * abi, pr * abi + pi * abr


def _s5_tables(a_re, a_im, log_dt):
    g, p = a_re.shape
    n = g * p
    flat = lambda x: x.reshape(1, n)
    ldt = jnp.broadcast_to(log_dt[:, None], (g, p))
    row = jax.ShapeDtypeStruct((1, n), F32)
    tab = jax.ShapeDtypeStruct((SUBLANES, n), F32)
    return pl.pallas_call(
        _s5_tables_kernel,
        out_shape=[tab, tab, row, row],
        name="s5_tables",
    )(flat(a_re), flat(a_im), flat(ldt))


def _s5_bbar_kernel(cre_ref, cim_ref, bre_ref, bim_ref, ore_ref, oim_ref):
    cr, ci = cre_ref[...], cim_ref[...]
    br, bi = bre_ref[...], bim_ref[...]
    ore_ref[...] = cr * br - ci * bi
    oim_ref[...] = cr * bi + ci * br


def _s5_bbar(coef_re, coef_im, b_re, b_im):
    shp = jax.ShapeDtypeStruct(b_re.shape, F32)
    return pl.pallas_call(_s5_bbar_kernel, out_shape=[shp, shp], name="s5_bbar")(
        coef_re[..., None], coef_im[..., None], b_re, b_im)


def _lane_order(re, im):
    lead = re.shape[:-2]
    g, p = re.shape[-2:]
    nch = g // SSM_CHUNK_GROUPS
    cw = SSM_CHUNK_GROUPS * p
    both = jnp.stack([re.reshape(lead + (nch, cw)), im.reshape(lead + (nch, cw))], axis=-2)
    return both.reshape(lead + (2 * g * p,))


def _lane_unorder(x, g, p):
    lead = x.shape[:-1]
    nch = g // SSM_CHUNK_GROUPS
    both = x.reshape(lead + (nch, 2, SSM_CHUNK_GROUPS, p))
    return both[..., 0, :, :].reshape(lead + (g, p)), both[..., 1, :, :].reshape(lead + (g, p))


def _block_diag_b(bbar_re, bbar_im):
    g, p, c = bbar_re.shape
    k = SSM_CHUNK_GROUPS
    nch = g // k
    bt = jnp.stack([bbar_re, bbar_im]).reshape(2, nch, k, p, c).transpose(1, 2, 4, 0, 3)
    eye = jnp.eye(k, dtype=F32)
    full = bt[:, :, :, :, None, :] * eye[None, :, None, None, :, None]
    return full.reshape(nch, k * c, 2 * k * p)


def _block_diag_c(c_re, c_im):
    g, c, p = c_re.shape
    k = SSM_CHUNK_GROUPS
    nch = g // k
    ct = jnp.stack([c_re, -c_im]).reshape(2, nch, k, c, p).transpose(1, 0, 2, 4, 3)
    eye = jnp.eye(k, dtype=F32)
    full = ct[:, :, :, :, None, :] * eye[None, None, :, None, :, None]
    return full.reshape(nch, 2 * k * p, k * c)


def _s5_scan_tiles(hbuf, pw_ref, n_tiles, carry_in, carry_ref, state_ref):
    two_n = hbuf.shape[1]
    cw2 = 2 * SSM_CHUNK_GROUPS * SSM_STATE
    cw = cw2 // 2
    rows = lax.broadcasted_iota(jnp.int32, (SUBLANES, SCAN_LANES), 0)
    for ch in range(two_n // cw2):
        for sub in range(cw // SCAN_LANES):
            re0 = ch * cw2 + sub * SCAN_LANES
            im0 = re0 + cw
            sl_re = pl.ds(re0, SCAN_LANES)
            sl_im = pl.ds(im0, SCAN_LANES)
            pr = pw_ref[:, sl_re]
            pi = pw_ref[:, sl_im]
            steps = []
            for sh in (1, 2, 4):
                ar = jnp.where(rows >= sh, pr[sh - 1:sh, :], 0.0)
                ai = jnp.where(rows >= sh, pi[sh - 1:sh, :], 0.0)
                steps.append((sh, ar, ai))

            def body(t, carry, sl_re=sl_re, sl_im=sl_im, pr=pr, pi=pi, steps=steps):
                r0 = pl.multiple_of(t * SUBLANES, SUBLANES)
                xr = hbuf[pl.ds(r0, SUBLANES), sl_re]
                xi = hbuf[pl.ds(r0, SUBLANES), sl_im]
                for sh, ar, ai in steps:
                    sr = pltpu.roll(xr, sh, 0)
                    si = pltpu.roll(xi, sh, 0)
                    xr, xi = xr + ar * sr - ai * si, xi + ar * si + ai * sr
                if carry_in is None:
                    hr, hi = carry
                else:
                    hr, hi = carry_in(t, sl_re), carry_in(t, sl_im)
                xr, xi = xr + pr * hr - pi * hi, xi + pr * hi + pi * hr
                hbuf[pl.ds(r0, SUBLANES), sl_re] = xr
                hbuf[pl.ds(r0, SUBLANES), sl_im] = xi
                if state_ref is not None:
                    state_ref[pl.ds(t, 1), sl_re] = xr[SUBLANES - 1:, :]
                    state_ref[pl.ds(t, 1), sl_im] = xi[SUBLANES - 1:, :]
                return xr[SUBLANES - 1:, :], xi[SUBLANES - 1:, :]

            if carry_in is None:
                init = (carry_ref[:, sl_re], carry_ref[:, sl_im])
            else:
                init = (jnp.zeros((1, SCAN_LANES), F32), jnp.zeros((1, SCAN_LANES), F32))
            hr, hi = lax.fori_loop(0, n_tiles, body, init)
            if carry_in is None:
                carry_ref[:, sl_re] = hr
                carry_ref[:, sl_im] = hi


def _s5_input(u_ref, bhi_ref, blo_ref, hbuf):
    u = u_ref[...]
    u_hi, u_lo = _split2(u)
    kc = bhi_ref.shape[1]
    nc = bhi_ref.shape[2]
    for ch in range(bhi_ref.shape[0]):
        uh = u_hi[:, ch * kc:(ch + 1) * kc]
        ul = u_lo[:, ch * kc:(ch + 1) * kc]
        hbuf[:, ch * nc:(ch + 1) * nc] = _dot(uh, bhi_ref[ch]) + _dot(uh, blo_ref[ch]) + _dot(ul, bhi_ref[ch])
    return u


def _s5_output(u, hbuf, cc_ref, d_ref, wg_ref, bg_ref, o_ref):
    kc = cc_ref.shape[1]
    ys = [_dot(hbuf[:, ch * kc:(ch + 1) * kc].astype(BF16), cc_ref[ch]) for ch in range(cc_ref.shape[0])]
    y = jnp.concatenate(ys, axis=1) + d_ref[...] * u
    z = jax.nn.gelu(y)
    o_ref[...] = z * jax.nn.sigmoid(_dot(z.astype(BF16), wg_ref[...]) + bg_ref[...])


def _s5_prompt_kernel(u_ref, bhi_ref, blo_ref, pw_ref, cc_ref, d_ref, wg_ref, bg_ref, o_ref, st_ref, hbuf, carry_ref):
    ti = pl.program_id(1)

    @pl.when(ti == 0)
    def _():
        carry_ref[...] = jnp.zeros_like(carry_ref)

    u = _s5_input(u_ref, bhi_ref, blo_ref, hbuf)
    _s5_scan_tiles(hbuf, pw_ref, hbuf.shape[0] // SUBLANES, None, carry_ref, None)
    _s5_output(u, hbuf, cc_ref, d_ref, wg_ref, bg_ref, o_ref)

    @pl.when(ti == pl.num_programs(1) - 1)
    def _():
        st_ref[0] = carry_ref[...]


def _s5_sample_kernel(u_ref, h0_ref, bhi_ref, blo_ref, pw_ref, cc_ref, d_ref, wg_ref, bg_ref, o_ref, st_ref, hbuf):
    u = _s5_input(u_ref, bhi_ref, blo_ref, hbuf)
    _s5_scan_tiles(hbuf, pw_ref, hbuf.shape[0] // SUBLANES, lambda t, sl: h0_ref[pl.ds(t, 1), sl], None, st_ref)
    _s5_output(u, hbuf, cc_ref, d_ref, wg_ref, bg_ref, o_ref)


def _s5_weight_specs(bhi, pw, cc, width):
    return [_const_spec(bhi.shape), _const_spec(bhi.shape), _const_spec(pw.shape), _const_spec(cc.shape),
            _const_spec((1, width)), _const_spec((width, width)), _const_spec((1, width))]


def _s5_prompt(u, n_batch, bhi, blo, pw, cc, d_skip, w_glu, b_glu):
    t, width = u.shape
    per = t // n_batch // TOK_BLOCK
    two_n = pw.shape[1]
    tok = pl.BlockSpec((TOK_BLOCK, width), lambda b, i: (b * per + i, 0))
    return pl.pallas_call(
        _s5_prompt_kernel,
        grid=(n_batch, per),
        in_specs=[tok] + _s5_weight_specs(bhi, pw, cc, width),
        out_specs=[tok, pl.BlockSpec((1, 1, two_n), lambda b, i: (b, 0, 0))],
        out_shape=[jax.ShapeDtypeStruct((t, width), F32), jax.ShapeDtypeStruct((n_batch, 1, two_n), F32)],
        scratch_shapes=[pltpu.VMEM((TOK_BLOCK, two_n), F32), pltpu.VMEM((1, two_n), F32)],
        compiler_params=_cparams("arbitrary", "arbitrary"),
        name="s5_prompt",
    )(u, bhi, blo, pw, cc, d_skip, w_glu, b_glu)


def _s5_sample(u, h0, bhi, blo, pw, cc, d_skip, w_glu, b_glu):
    t, width = u.shape
    two_n = pw.shape[1]
    tiles = TOK_BLOCK // SUBLANES
    tok = pl.BlockSpec((TOK_BLOCK, width), lambda i: (i, 0))
    st = pl.BlockSpec((tiles, two_n), lambda i: (i, 0))
    return pl.pallas_call(
        _s5_sample_kernel,
        grid=(t // TOK_BLOCK,),
        in_specs=[tok, st] + _s5_weight_specs(bhi, pw, cc, width),
        out_specs=[tok, st],
        out_shape=[jax.ShapeDtypeStruct((t, width), F32), jax.ShapeDtypeStruct(h0.shape, F32)],
        scratch_shapes=[pltpu.VMEM((TOK_BLOCK, two_n), F32)],
        compiler_params=_cparams("arbitrary"),
        name="s5_sample",
    )(u, h0, bhi, blo, pw, cc, d_skip, w_glu, b_glu)


def _outproj_kernel(att_ref, ssm_ref, ga_ref, gs_ref, wa_ref, ws_ref, x_ref, gate_ref, o_ref):
    def norm(v, g):
        return (v * lax.rsqrt(jnp.mean(v * v, axis=-1, keepdims=True) + EPS) * g).astype(BF16)

    merged = _dot(norm(att_ref[...], ga_ref[...]), wa_ref[...]) + _dot(norm(ssm_ref[...], gs_ref[...]), ws_ref[...])
    tg, _, d = x_ref.shape
    o_ref[...] = x_ref[...] + gate_ref[...] * merged.reshape(tg, SUBLANES, d)


def _out_projection(att, ssm, g_att, g_ssm, w_att, w_ssm, xg, gate):
    ng, _, d = xg.shape
    tg = TOK_BLOCK // SUBLANES
    steps = ng // tg
    wa = att.shape[1]
    ws = ssm.shape[1]
    return pl.pallas_call(
        _outproj_kernel,
        grid=(steps,),
        in_specs=[pl.BlockSpec((TOK_BLOCK, wa), lambda i: (i, 0)), pl.BlockSpec((TOK_BLOCK, ws), lambda i: (i, 0)),
                  _const_spec((1, wa)), _const_spec((1, ws)), _const_spec((wa, d)), _const_spec((ws, d)),
                  pl.BlockSpec((tg, SUBLANES, d), lambda i: (i, 0, 0)), _mod_spec(gate.shape[0], steps, tg, d)],
        out_specs=pl.BlockSpec((tg, SUBLANES, d), lambda i: (i, 0, 0)),
        out_shape=jax.ShapeDtypeStruct(xg.shape, F32),
        compiler_params=_cparams("arbitrary"),
        name="out_projection",
    )(att, ssm, g_att, g_ssm, w_att, w_ssm, xg, gate)


def _topk_rows(problems, k):
    scores = [p[0] for p in problems]
    n = scores[0].shape[0]
    row = lax.broadcasted_iota(jnp.int32, scores[0].shape, 0).astype(F32)
    for i in range(k):
        for z, (_, val_ref, idx_ref) in enumerate(problems):
            s = scores[z]
            m = jnp.max(s, axis=0, keepdims=True)
            j = jnp.min(jnp.where(s == m, row, float(n)), axis=0, keepdims=True)
            val_ref[i:i + 1, :] = m
            idx_ref[i:i + 1, :] = j
            scores[z] = jnp.where(row == j, -jnp.inf, s)


def _take_rows(table_ref, idx, n):
    out = jnp.zeros_like(idx)
    for r in range(n):
        out = jnp.where(idx == float(r), table_ref[r:r + 1, :], out)
    return out


def _peer_select_kernel(x_ref, shift_ref, scale_ref, g_ref, wqt_ref, sk_ref, h_ref, e_ref, gw_ref,
                        qt_sc, val_sc, idx_sc, cs_sc, top_sc, pick_sc):
    tg, _, d = x_ref.shape
    h = _rms_mod(x_ref[...], g_ref[...], shift_ref[...], scale_ref[...]).reshape(tg * SUBLANES, d).astype(BF16)
    h_ref[...] = h
    qt_sc[...] = _dot_nt(wqt_ref[...], h).astype(BF16)
    n_sides, n_keys, half = sk_ref.shape
    k = PEER_TOPK
    n_chunks = tg * SUBLANES // LANES

    def head_body(head, carry):
        first = []
        for c in range(n_chunks):
            for side in range(2):
                hx = 2 * head + side
                r0 = pl.multiple_of(hx * half, half)
                s = _dot(sk_ref[hx], qt_sc[pl.ds(r0, half), pl.ds(c * LANES, LANES)])
                first.append((s, val_sc.at[c, side], idx_sc.at[c, side]))
        _topk_rows(first, k)
        second = []
        for c in range(n_chunks):
            v2 = val_sc[c, 1]
            for a in range(k):
                cs_sc[c, a * k:(a + 1) * k, :] = val_sc[c, 0, a:a + 1, :] + v2
            second.append((cs_sc[c], top_sc.at[c], pick_sc.at[c]))
        _topk_rows(second, k)
        r_out = pl.multiple_of(head * k, k)
        for c in range(n_chunks):
            cols = pl.ds(c * LANES, LANES)
            ts = top_sc[c]
            p = jnp.exp(ts - ts[0:1, :])
            gw_ref[pl.ds(r_out, k), cols] = p / jnp.sum(p, axis=0, keepdims=True)
            pick = pick_sc[c]
            a = jnp.floor(pick * (1.0 / k))
            first_key = _take_rows(idx_sc.at[c, 0], a, k)
            second_key = _take_rows(idx_sc.at[c, 1], pick - a * float(k), k)
            e_ref[pl.ds(r_out, k), cols] = (first_key * float(n_keys) + second_key).astype(jnp.int32)
        return carry

    lax.fori_loop(0, n_sides // 2, head_body, 0)


def _peer_select(xg, shift, scale, g, w_q_t, subkeys):
    ng, _, d = xg.shape
    t = ng * SUBLANES
    tg = TOK_BLOCK // SUBLANES
    steps = ng // tg
    n_sides = subkeys.shape[0]
    k = PEER_TOPK
    rows = (n_sides // 2) * k
    nch = TOK_BLOCK // LANES
    mod_spec = _mod_spec(shift.shape[0], steps, tg, d)
    return pl.pallas_call(
        _peer_select_kernel,
        grid=(steps,),
        in_specs=[pl.BlockSpec((tg, SUBLANES, d), lambda i: (i, 0, 0)), mod_spec, mod_spec, _const_spec((1, d)),
                  _const_spec(w_q_t.shape), _const_spec(subkeys.shape)],
        out_specs=[pl.BlockSpec((TOK_BLOCK, d), lambda i: (i, 0)), pl.BlockSpec((rows, TOK_BLOCK), lambda i: (0, i)),
                   pl.BlockSpec((rows, TOK_BLOCK), lambda i: (0, i))],
        out_shape=[jax.ShapeDtypeStruct((t, d), BF16), jax.ShapeDtypeStruct((rows, t), jnp.int32),
                   jax.ShapeDtypeStruct((rows, t), F32)],
        scratch_shapes=[pltpu.VMEM((w_q_t.shape[0], TOK_BLOCK), BF16),
                        pltpu.VMEM((nch, 2, k, LANES), F32), pltpu.VMEM((nch, 2, k, LANES), F32),
                        pltpu.VMEM((nch, k * k, LANES), F32),
                        pltpu.VMEM((nch, k, LANES), F32), pltpu.VMEM((nch, k, LANES), F32)],
        compiler_params=_cparams("arbitrary"),
        name="peer_select",
    )(xg, shift, scale, g, w_q_t, subkeys)


def _peer_mask_kernel(e_ref, g_ref, o_ref):
    n_tiles = o_ref.shape[0]
    n2 = o_ref.shape[2]
    n1 = n_tiles * SUBLANES
    shift = n2.bit_length() - 1
    sub1 = lax.broadcasted_iota(jnp.int32, (n1, e_ref.shape[1]), 0)
    sub2 = lax.broadcasted_iota(jnp.int32, (n2, e_ref.shape[1]), 0)

    def body(t, _):
        e = e_ref[pl.ds(t, 1), :]
        gate = g_ref[pl.ds(t, 1), :]
        first = jnp.right_shift(e, shift)
        second = jnp.bitwise_and(e, n2 - 1)
        p_hi, p_lo = _split2(jnp.where(sub1 == first, gate, 0.0))
        r = jnp.where(sub2 == second, 1.0, 0.0).astype(BF16)
        m = _dot_nt(p_hi, r) + _dot_nt(p_lo, r)
        r0 = pl.multiple_of(t * SUBLANES, SUBLANES)
        for jb in range(n_tiles):
            o_ref[jb, pl.ds(r0, SUBLANES), :] = m[jb * SUBLANES:(jb + 1) * SUBLANES, :]
        return 0

    lax.fori_loop(0, e_ref.shape[0], body, 0, unroll=8)


def _peer_mask(e_tok, g_tok):
    t, picks = e_tok.shape
    n = PEER_N_KEYS
    assert n & (n - 1) == 0
    tb = MASK_TOK_BLOCK
    return pl.pallas_call(
        _peer_mask_kernel,
        grid=(t // tb,),
        in_specs=[pl.BlockSpec((tb, picks), lambda i: (i, 0)), pl.BlockSpec((tb, picks), lambda i: (i, 0))],
        out_specs=pl.BlockSpec((n // SUBLANES, tb * SUBLANES, n), lambda i: (0, i, 0)),
        out_shape=jax.ShapeDtypeStruct((n // SUBLANES, t * SUBLANES, n), F32),
        compiler_params=_cparams("arbitrary"),
        name="peer_mask",
    )(e_tok, g_tok)


def _peer_ffn_kernel(h_ref, u_ref, v_ref, m_ref, x_ref, gate_ref, gf_ref, o_ref, acc_ref):
    j = pl.program_id(1)

    @pl.when(j == 0)
    def _():
        acc_ref[...] = jnp.zeros_like(acc_ref)

    tm = h_ref.shape[0]
    n2 = m_ref.shape[2]
    a = _dot_nt(h_ref[...], u_ref[...])
    m2 = m_ref.at[0]
    parts = []
    for i1 in range(SUBLANES):
        gate = m2[pl.ds(i1, tm, stride=SUBLANES), :]
        parts.append((gate * jax.nn.gelu(a[:, i1 * n2:(i1 + 1) * n2])).astype(BF16))
    acc_ref[...] += _dot(jnp.concatenate(parts, axis=1), v_ref[...])

    @pl.when(j == pl.num_programs(1) - 1)
    def _():
        tg, _, d = x_ref.shape
        x2 = x_ref[...] + gate_ref[...] * acc_ref[...].reshape(tg, SUBLANES, d)
        ms = jnp.mean(x2 * x2, axis=-1, keepdims=True)
        o_ref[...] = x2 * lax.rsqrt(ms + EPS) * gf_ref[...]


def _peer_ffn(h, exp_u, exp_v, mask, xg, gate, g_final):
    t, d = h.shape
    ne = exp_u.shape[0]
    n = PEER_N_KEYS
    tm = PEER_TOK_BLOCK
    te = PEER_EXPERT_TILE
    assert te == SUBLANES * n and mask.shape == (ne // te, t * SUBLANES, n)
    tg = tm // SUBLANES
    steps = t // tm
    return pl.pallas_call(
        _peer_ffn_kernel,
        grid=(steps, ne // te),
        in_specs=[pl.BlockSpec((tm, d), lambda i, j: (i, 0)),
                  pl.BlockSpec((te, d), lambda i, j: (j, 0)), pl.BlockSpec((te, d), lambda i, j: (j, 0)),
                  pl.BlockSpec((1, tm * SUBLANES, n), lambda i, j: (j, i, 0)),
                  pl.BlockSpec((tg, SUBLANES, d), lambda i, j: (i, 0, 0)),
                  _mod_spec(gate.shape[0], steps, tg, d), _const_spec((1, d))],
        out_specs=pl.BlockSpec((tg, SUBLANES, d), lambda i, j: (i, 0, 0)),
        out_shape=jax.ShapeDtypeStruct(xg.shape, F32),
        scratch_shapes=[pltpu.VMEM((tm, d), F32)],
        compiler_params=_cparams("arbitrary", "arbitrary"),
        name="peer_ffn",
    )(h, exp_u, exp_v, mask, xg, gate, g_final)


def _token_stages_pre(xg, mods, lw):
    shift1, scale1 = mods[0], mods[1]
    return _in_projection(xg, shift1, scale1, lw["g_norm1"], lw["w_cat"], lw["bf_pad"], lw["fox_width"])


def _token_stages_post(xg, mods, att, ssm, lw, g_final):
    gate1, shift2, scale2, gate2 = mods[2], mods[3], mods[4], mods[5]
    x1 = _out_projection(att, ssm, lw["g_attn_out"], lw["g_ssm_out"], lw["w_out_att"], lw["w_out_ssm"], xg, gate1)
    h2, e_t, g_t = _peer_select(x1, shift2, scale2, lw["g_norm2"], lw["peer_w_q_t"], lw["peer_subkeys"])
    mask = _peer_mask(e_t.T, g_t.T)
    return _peer_ffn(h2, lw["peer_u"], lw["peer_v"], mask, x1, gate2, g_final)


def kernel(x_prompt, x_sample, cache_k, cache_v, cache_logf, state_ssm_re, state_ssm_im, page_table, c_prompt, c_sample, w_ada, b_ada, g_norm1, g_norm2, w_in, b_forget, ssm_a_re, ssm_a_im, ssm_log_dt, ssm_b_re, ssm_b_im, ssm_c_re, ssm_c_im, ssm_d, w_glu, b_glu, g_attn_out, g_ssm_out, w_out, peer_w_q, peer_subkeys, peer_u, peer_v, g_final):
    depth = w_ada.shape[0]
    assert depth == 1, "single trunk layer"
    nb, seq, d = x_prompt.shape
    nd, dseq, _ = x_sample.shape
    assert dseq == SUBLANES
    n_pool, page, n_heads, hd = cache_k.shape[1:]
    fox_w = n_heads * hd
    n_groups, n_state = ssm_a_re.shape[1:]
    ssm_w = n_groups * SSM_CH_PER_GROUP
    assert fox_w == ssm_w
    n_pages = page_table.shape[1]
    layer = 0

    w_in_l = w_in[layer]
    w_cat = jnp.concatenate([w_in_l[:, :3 * fox_w], w_in_l[:, 3 * fox_w + n_heads:], w_in_l[:, 3 * fox_w:3 * fox_w + n_heads],
                             jnp.zeros((d, LANES - n_heads), F32)], axis=1).astype(BF16)
    bf_pad = jnp.concatenate([b_forget[layer], jnp.zeros((LANES - n_heads,), F32)]).reshape(1, LANES)
    w_out_l = w_out[layer].astype(BF16)
    lw = dict(
        fox_width=fox_w, w_cat=w_cat, bf_pad=bf_pad,
        g_norm1=g_norm1[layer].reshape(1, d), g_norm2=g_norm2[layer].reshape(1, d),
        g_attn_out=g_attn_out[layer].reshape(1, fox_w), g_ssm_out=g_ssm_out[layer].reshape(1, ssm_w),
        w_out_att=w_out_l[:fox_w], w_out_ssm=w_out_l[fox_w:],
        peer_w_q_t=peer_w_q[layer].T.astype(BF16),
        peer_subkeys=peer_subkeys[layer].reshape((-1,) + peer_subkeys.shape[-2:]).astype(BF16),
        peer_u=peer_u[layer].astype(BF16), peer_v=peer_v[layer].astype(BF16),
    )
    gf = g_final.reshape(1, d)

    n_c = nb + nd
    n_c_pad = -(-n_c // SUBLANES) * SUBLANES
    c_all = jnp.concatenate([c_prompt, c_sample, jnp.zeros((n_c_pad - n_c, d), F32)], axis=0)
    mod = _ada_modulation(c_all, w_ada[layer], b_ada[layer])
    mods_p = [mod[:nb, i * d:(i + 1) * d].reshape(nb, 1, d) for i in range(N_MOD)]
    mods_s = [mod[nb:n_c, i * d:(i + 1) * d].reshape(nd, 1, d) for i in range(N_MOD)]

    pw_re, pw_im, coef_re, coef_im = _s5_tables(ssm_a_re[layer], ssm_a_im[layer], ssm_log_dt[layer])
    bbar_re, bbar_im = _s5_bbar(coef_re.reshape(n_groups, n_state), coef_im.reshape(n_groups, n_state),
                                ssm_b_re[layer], ssm_b_im[layer])
    b_blk = _block_diag_b(bbar_re, bbar_im)
    b_hi = b_blk.astype(BF16)
    b_lo = (b_blk - b_hi.astype(F32)).astype(BF16)
    c_blk = _block_diag_c(ssm_c_re[layer], ssm_c_im[layer]).astype(BF16)
    pw = _lane_order(pw_re.reshape(SUBLANES, n_groups, n_state), pw_im.reshape(SUBLANES, n_groups, n_state))
    s5w = (b_hi, b_lo, pw, c_blk, ssm_d[layer].reshape(1, ssm_w), w_glu[layer].astype(BF16), b_glu[layer].reshape(1, ssm_w))

    xg_p = x_prompt.reshape(nb * seq // SUBLANES, SUBLANES, d)
    q_p, k_p, v_p, u_p, lf_p = _token_stages_pre(xg_p, mods_p, lw)
    fcum = _cumsum_time(lf_p, nb).reshape(nb, seq, LANES)[:, :, :n_heads].transpose(0, 2, 1)
    fq = jnp.broadcast_to(fcum[..., None], (nb, n_heads, seq, LANES))
    fk = fcum.reshape(nb, n_heads, seq // ATT_BLOCK, ATT_BLOCK)
    as3 = lambda a: a.astype(BF16).reshape(nb, seq, fox_w)
    att_p = _prompt_attention(as3(q_p), as3(k_p), as3(v_p), fq, fk, n_heads).reshape(nb * seq, fox_w)
    ssm_p, st_p = _s5_prompt(u_p, nb, *s5w)
    y_p = _token_stages_post(xg_p, mods_p, att_p, ssm_p, lw, gf)
    re_p, im_p = _lane_unorder(st_p.reshape(nb, -1), n_groups, n_state)

    xg_s = x_sample.reshape(nd, SUBLANES, d)
    q_s, k_s, v_s, u_s, lf_s = _token_stages_pre(xg_s, mods_s, lw)
    rows = dseq * n_heads
    heads_rows = lambda a: a.reshape(nd, dseq, n_heads, hd).reshape(nd, rows, hd)
    logf_pages = cache_logf[layer].reshape(n_pool, page * n_heads)
    dloc, tot = _page_sums(logf_pages, n_heads)
    c_new = _cumsum_short(lf_s.reshape(nd, dseq, LANES))[:, :, :n_heads].reshape(nd, rows)
    cq_col = jnp.broadcast_to(c_new[:, :, None], (nd, rows, LANES))
    cq_row = c_new.reshape(nd, 1, rows)
    att_s = _sample_attention(page_table, heads_rows(q_s), heads_rows(k_s), heads_rows(v_s),
                              cache_k[layer].reshape(n_pool, page * n_heads, hd),
                              cache_v[layer].reshape(n_pool, page * n_heads, hd),
                              dloc.reshape(n_pool, 1, page * n_heads), tot.reshape(n_pool, 1, page * n_heads),
                              cq_col, cq_row, n_heads)
    att_s = att_s.reshape(nd, dseq, n_heads, hd).reshape(nd * dseq, fox_w)
    h0 = _lane_order(state_ssm_re[layer], state_ssm_im[layer])
    ssm_s, st_s = _s5_sample(u_s, h0, *s5w)
    y_s = _token_stages_post(xg_s, mods_s, att_s, ssm_s, lw, gf)
    re_s, im_s = _lane_unorder(st_s, n_groups, n_state)

    kv5 = lambda a, b_, l_: a.reshape(1, b_, l_, n_heads, hd)
    return (y_p.reshape(nb, seq, d), y_s.reshape(nd, dseq, d),
            kv5(k_p, nb, seq), kv5(v_p, nb, seq), lf_p[:, :n_heads].reshape(1, nb, seq, n_heads),
            re_p[None], im_p[None],
            kv5(k_s, nd, dseq), kv5(v_s, nd, dseq), lf_s[:, :n_heads].reshape(1, nd, dseq, n_heads),
            re_s[None], im_s[None])
```

```python
import functools
import math

import jax
import jax.numpy as jnp
from jax import lax
from jax.experimental import pallas as pl
from jax.experimental.pallas import tpu as pltpu

F32 = jnp.float32
BF16 = jnp.bfloat16
EPS = 1e-6
NEG = -1e30

SUBLANES = 8
LANES = 128
VMEM_LIMIT_BYTES = 56 * 1024 * 1024

N_MOD = 6
FOX_HEAD_DIM = 128
SSM_CH_PER_GROUP = 16
SSM_STATE = 64
SSM_CHUNK_GROUPS = 16
PEER_N_KEYS = 128
PEER_TOPK = 16
PEER_EXPERT_TILE = 1024
PEER_HEAD_GROUP = 2
PAIR_ROWS = 64
ROW_SENTINEL = 1024.0

TOK_BLOCK = 256
ATT_BLOCK = 512
ATT_HEAD_GROUP = 4
LOG2E = 1.4426950408889634
PEER_TOK_BLOCK = 512
MASK_TOK_BLOCK = 64
SCAN_LANES = 512


def _cparams(*sem):
    return pltpu.CompilerParams(dimension_semantics=sem, vmem_limit_bytes=VMEM_LIMIT_BYTES)


def _const_spec(shape):
    nd = len(shape)
    return pl.BlockSpec(shape, lambda *_: (0,) * nd)


def _split2(x):
    hi = x.astype(BF16)
    lo = (x - hi.astype(F32)).astype(BF16)
    return hi, lo


def _split3(x):
    x1 = x.astype(BF16)
    r1 = x - x1.astype(F32)
    x2 = r1.astype(BF16)
    x3 = (r1 - x2.astype(F32)).astype(BF16)
    return x1, x2, x3


def _dot(a, b):
    return jnp.dot(a, b, preferred_element_type=F32)


def _dot_nt(a, b):
    return lax.dot_general(a, b, (((1,), (1,)), ((), ())), preferred_element_type=F32)


def _rms_mod(xg, g, shift, scale):
    ms = jnp.mean(xg * xg, axis=-1, keepdims=True)
    y = xg * lax.rsqrt(ms + EPS) * g
    return y * (1.0 + scale) + shift


def _mod_spec(n_mod_rows, n_steps, tg, d):
    if n_mod_rows == n_steps * tg:
        return pl.BlockSpec((tg, 1, d), lambda i, *_: (i, 0, 0))
    assert n_steps % n_mod_rows == 0
    per = n_steps // n_mod_rows
    return pl.BlockSpec((1, 1, d), lambda i, *_: (i // per, 0, 0))


def _ada_kernel(c_ref, w_ref, b_ref, o_ref):
    a = jax.nn.silu(c_ref[...]).astype(BF16)
    o_ref[...] = _dot(a, w_ref[...].astype(BF16)) + b_ref[...]


def _ada_modulation(c_all, w_ada, b_ada):
    m, d = c_all.shape
    n = w_ada.shape[1]
    tn = 1024
    return pl.pallas_call(
        _ada_kernel,
        grid=(n // tn,),
        in_specs=[_const_spec((m, d)), pl.BlockSpec((d, tn), lambda j: (0, j)), pl.BlockSpec((1, tn), lambda j: (0, j))],
        out_specs=pl.BlockSpec((m, tn), lambda j: (0, j)),
        out_shape=jax.ShapeDtypeStruct((m, n), F32),
        compiler_params=_cparams("arbitrary"),
        name="ada_modulation",
    )(c_all, w_ada, b_ada.reshape(1, n))


def _inproj_kernel(x_ref, shift_ref, scale_ref, g_ref, w_ref, bf_ref, q_ref, k_ref, v_ref, u_ref, lf_ref, *bf16_refs):
    tg, _, d = x_ref.shape
    h = _rms_mod(x_ref[...], g_ref[...], shift_ref[...], scale_ref[...])
    r = _dot(h.reshape(tg * SUBLANES, d).astype(BF16), w_ref[...])
    w = q_ref.shape[1]
    q_ref[...] = r[:, 0:w]
    k_ref[...] = r[:, w:2 * w]
    v_ref[...] = r[:, 2 * w:3 * w]
    u_ref[...] = r[:, 3 * w:4 * w]
    lf_ref[...] = jax.nn.log_sigmoid(r[:, 4 * w:] + bf_ref[...])
    for i, ref in enumerate(bf16_refs):
        ref[...] = r[:, i * w:(i + 1) * w].astype(BF16)


def _in_projection(xg, shift, scale, g, w_cat, bf_pad, width, emit_bf16):
    ng, _, d = xg.shape
    t = ng * SUBLANES
    tg = TOK_BLOCK // SUBLANES
    steps = ng // tg
    n = w_cat.shape[1]
    mod_spec = _mod_spec(shift.shape[0], steps, tg, d)
    tok = lambda w: pl.BlockSpec((TOK_BLOCK, w), lambda i: (i, 0))
    n_bf16 = 3 if emit_bf16 else 0
    return pl.pallas_call(
        _inproj_kernel,
        grid=(steps,),
        in_specs=[pl.BlockSpec((tg, SUBLANES, d), lambda i: (i, 0, 0)), mod_spec, mod_spec,
                  _const_spec((1, d)), _const_spec((d, n)), _const_spec((1, LANES))],
        out_specs=[tok(width)] * 4 + [tok(LANES)] + [tok(width)] * n_bf16,
        out_shape=([jax.ShapeDtypeStruct((t, width), F32)] * 4 + [jax.ShapeDtypeStruct((t, LANES), F32)]
                   + [jax.ShapeDtypeStruct((t, width), BF16)] * n_bf16),
        compiler_params=_cparams("arbitrary"),
        name="in_projection",
    )(xg, shift, scale, g, w_cat, bf_pad)


def _cumsum_kernel(x_ref, tri_ref, o_ref, carry_ref):
    @pl.when(pl.program_id(1) == 0)
    def _():
        carry_ref[...] = jnp.zeros_like(carry_ref)

    x1, x2, x3 = _split3(x_ref[...])
    tri = tri_ref[...]
    loc = _dot(tri, x1) + _dot(tri, x2) + _dot(tri, x3) + carry_ref[...]
    o_ref[...] = loc
    carry_ref[...] = loc[-1:, :]


def _cumsum_time(x, n_batch):
    t, w = x.shape
    l = t // n_batch
    tb = 256
    tri = (lax.broadcasted_iota(jnp.int32, (tb, tb), 1) <= lax.broadcasted_iota(jnp.int32, (tb, tb), 0)).astype(BF16)
    per = l // tb
    return pl.pallas_call(
        _cumsum_kernel,
        grid=(n_batch, per),
        in_specs=[pl.BlockSpec((tb, w), lambda b, i: (b * per + i, 0)), _const_spec((tb, tb))],
        out_specs=pl.BlockSpec((tb, w), lambda b, i: (b * per + i, 0)),
        out_shape=jax.ShapeDtypeStruct((t, w), F32),
        scratch_shapes=[pltpu.VMEM((1, w), F32)],
        compiler_params=_cparams("arbitrary", "arbitrary"),
        name="logf_cumsum",
    )(x, tri)


def _cumsum_short_kernel(x_ref, o_ref):
    run = x_ref[:, 0:1, :]
    o_ref[:, 0:1, :] = run
    for r in range(1, x_ref.shape[1]):
        run = run + x_ref[:, r:r + 1, :]
        o_ref[:, r:r + 1, :] = run


def _cumsum_short(x):
    return pl.pallas_call(_cumsum_short_kernel, out_shape=jax.ShapeDtypeStruct(x.shape, F32), name="logf_cumsum_new")(x)


def _pattn_kernel(q_ref, k_ref, v_ref, fq_ref, fk_ref, o_ref, m_sc, l_sc, acc_sc, fq_sc, *, scale):
    qi = pl.program_id(2)
    tq = q_ref.shape[1]
    n_group, _, hd = acc_sc.shape
    m_sc[...] = jnp.full_like(m_sc, NEG)
    l_sc[...] = jnp.zeros_like(l_sc)
    acc_sc[...] = jnp.zeros_like(acc_sc)
    fq_sc[...] = fq_ref[0] * LOG2E
    c = scale * LOG2E

    def block(kj, diagonal):
        start = pl.multiple_of(kj * tq, tq)
        for g in range(n_group):
            hs = slice(g * hd, (g + 1) * hd)
            s = _dot_nt(q_ref[0, :, hs], k_ref[0, pl.ds(start, tq), hs])
            fk2 = fk_ref[0, g, pl.ds(kj, 1), :] * LOG2E
            tiles = []
            for t in range(tq // LANES):
                ls = slice(t * LANES, (t + 1) * LANES)
                st = s[:, ls] * c - fk2[:, ls]
                if diagonal:
                    row = lax.broadcasted_iota(jnp.int32, st.shape, 0)
                    col = lax.broadcasted_iota(jnp.int32, st.shape, 1) + t * LANES
                    st = jnp.where(col <= row, st, NEG)
                tiles.append(st)
            m_prev = m_sc[g]
            fq2 = fq_sc[g]
            m_new = jnp.maximum(m_prev, jnp.max(functools.reduce(jnp.maximum, tiles), axis=1, keepdims=True) + fq2)
            alpha = jnp.exp2(m_prev - m_new)
            shift = fq2 - m_new
            ps = [jnp.exp2(st + shift) for st in tiles]
            l_sc[g] = alpha * l_sc[g] + functools.reduce(jnp.add, ps)
            p = jnp.concatenate([x.astype(BF16) for x in ps], axis=1)
            acc_sc[g] = alpha * acc_sc[g] + _dot(p, v_ref[0, pl.ds(start, tq), hs])
            m_sc[g] = m_new

    def off_diagonal(kj, carry):
        block(kj, False)
        return carry

    lax.fori_loop(0, qi, off_diagonal, 0)
    block(qi, True)
    for g in range(n_group):
        o_ref[0, :, g * hd:(g + 1) * hd] = acc_sc[g] / jnp.sum(l_sc[g], axis=1, keepdims=True)


def _prompt_attention(q, k, v, fq, fk, n_heads):
    b, l, w = q.shape
    hd = w // n_heads
    tq = ATT_BLOCK
    ng = ATT_HEAD_GROUP
    assert fk.shape == (b, n_heads, l // tq, tq) and n_heads % ng == 0
    return pl.pallas_call(
        functools.partial(_pattn_kernel, scale=hd ** -0.5),
        grid=(b, n_heads // ng, l // tq),
        in_specs=[pl.BlockSpec((1, tq, ng * hd), lambda bi, h, i: (bi, i, h)),
                  pl.BlockSpec((1, l, ng * hd), lambda bi, h, i: (bi, 0, h)),
                  pl.BlockSpec((1, l, ng * hd), lambda bi, h, i: (bi, 0, h)),
                  pl.BlockSpec((1, ng, tq, LANES), lambda bi, h, i: (bi, h, i, 0)),
                  pl.BlockSpec((1, ng, l // tq, tq), lambda bi, h, i: (bi, h, 0, 0))],
        out_specs=pl.BlockSpec((1, tq, ng * hd), lambda bi, h, i: (bi, i, h)),
        out_shape=jax.ShapeDtypeStruct((b, l, w), F32),
        scratch_shapes=[pltpu.VMEM((ng, tq, LANES), F32), pltpu.VMEM((ng, tq, LANES), F32), pltpu.VMEM((ng, tq, hd), F32),
                        pltpu.VMEM((ng, tq, LANES), F32)],
        compiler_params=_cparams("arbitrary", "arbitrary", "arbitrary"),
        name="prompt_attention",
    )(q, k, v, fq, fk)


def _page_sums_kernel(x_ref, m_after_ref, m_all_ref, dloc_ref, tot_ref):
    x1, x2, x3 = _split3(x_ref[...])
    ma = m_after_ref[...]
    mt = m_all_ref[...]
    dloc_ref[...] = _dot(x1, ma) + _dot(x2, ma) + _dot(x3, ma)
    tot_ref[...] = _dot(x1, mt) + _dot(x2, mt) + _dot(x3, mt)


def _page_sums(logf_pages, n_heads):
    n_pool, w = logf_pages.shape
    r = lax.broadcasted_iota(jnp.int32, (w, w), 0)
    c = lax.broadcasted_iota(jnp.int32, (w, w), 1)
    same_head = (r % n_heads) == (c % n_heads)
    m_after = (same_head & (r // n_heads > c // n_heads)).astype(BF16)
    m_all = same_head.astype(BF16)
    tp = 256
    return pl.pallas_call(
        _page_sums_kernel,
        grid=(n_pool // tp,),
        in_specs=[pl.BlockSpec((tp, w), lambda i: (i, 0)), _const_spec((w, w)), _const_spec((w, w))],
        out_specs=[pl.BlockSpec((tp, w), lambda i: (i, 0))] * 2,
        out_shape=[jax.ShapeDtypeStruct((n_pool, w), F32)] * 2,
        compiler_params=_cparams("arbitrary"),
        name="page_logf_sums",
    )(logf_pages, m_after, m_all)


def _lane_fold(x, op):
    out = x[:, 0:LANES]
    for c in range(1, x.shape[1] // LANES):
        out = op(out, x[:, c * LANES:(c + 1) * LANES])
    return out


def _sattn_kernel(pt_ref, q_ref, kn_ref, vn_ref, cqc_ref, cqr_ref, mb_ref, mbn_ref, ck_hbm, cv_hbm, dloc_hbm, tot_hbm,
                  o_ref, kbuf, vbuf, dbuf, tbuf, sbuf, sems, *, scale):
    b = pl.program_id(0)
    n_pages = kbuf.shape[1]
    slot = lax.rem(b, 2)

    def page_copies(bb, sl, i):
        p = pt_ref[bb, i]
        return (pltpu.make_async_copy(ck_hbm.at[p], kbuf.at[sl, i], sems.at[sl, 0]),
                pltpu.make_async_copy(cv_hbm.at[p], vbuf.at[sl, i], sems.at[sl, 1]),
                pltpu.make_async_copy(dloc_hbm.at[p], dbuf.at[sl, i], sems.at[sl, 2]),
                pltpu.make_async_copy(tot_hbm.at[p], tbuf.at[sl, i], sems.at[sl, 3]))

    def fetch(bb, sl):
        for i in range(n_pages):
            for cp in page_copies(bb, sl, i):
                cp.start()

    @pl.when(b == 0)
    def _():
        fetch(0, 0)

    @pl.when(b + 1 < pl.num_programs(0))
    def _():
        fetch(b + 1, 1 - slot)

    for i in range(n_pages):
        for cp in page_copies(b, slot, i):
            cp.wait()

    q = q_ref[0].astype(BF16)
    cq = cqc_ref[0][:, 0:1]
    base = mb_ref[...] + cq

    def scores(k, carry):
        after, mrun = carry
        i = n_pages - 1 - k
        s = _dot_nt(q, kbuf[slot, i].astype(BF16)) * scale + base + (dbuf[slot, i] + after)
        sbuf[i] = s
        return after + tbuf[slot, i], jnp.maximum(mrun, _lane_fold(s, jnp.maximum))

    rows = q.shape[0]
    init = (jnp.zeros(dbuf.shape[2:], F32), jnp.full((rows, LANES), NEG, F32))
    _, mrun = lax.fori_loop(0, n_pages, scores, init, unroll=2)
    sn = _dot_nt(q, kn_ref[0].astype(BF16)) * scale + (cq - cqr_ref[0]) + mbn_ref[...]
    m = jnp.maximum(jnp.max(mrun, axis=1, keepdims=True), jnp.max(sn, axis=1, keepdims=True))

    def values(i, carry):
        lrun, acc = carry
        p = jnp.exp(sbuf[i] - m)
        return lrun + _lane_fold(p, jnp.add), acc + _dot(p.astype(BF16), vbuf[slot, i].astype(BF16))

    init = (jnp.zeros((rows, LANES), F32), jnp.zeros((rows, q.shape[1]), F32))
    lrun, acc = lax.fori_loop(0, n_pages, values, init, unroll=2)
    pn = jnp.exp(sn - m)
    l = jnp.sum(lrun, axis=1, keepdims=True) + jnp.sum(pn, axis=1, keepdims=True)
    acc = acc + _dot(pn.astype(BF16), vn_ref[0].astype(BF16))
    o_ref[0] = acc / l


def _sample_attention(page_table, q, kn, vn, cache_k, cache_v, dloc, tot, cq_col, cq_row, n_heads):
    b, rows, hd = q.shape
    n_pages = page_table.shape[1]
    cols = cache_k.shape[1]
    r = lax.broadcasted_iota(jnp.int32, (rows, cols), 0)
    c = lax.broadcasted_iota(jnp.int32, (rows, cols), 1)
    mask_past = jnp.where((r % n_heads) == (c % n_heads), 0.0, NEG).astype(F32)
    rn = lax.broadcasted_iota(jnp.int32, (rows, rows), 0)
    cn = lax.broadcasted_iota(jnp.int32, (rows, rows), 1)
    mask_new = jnp.where(((rn % n_heads) == (cn % n_heads)) & (cn // n_heads <= rn // n_heads), 0.0, NEG).astype(F32)
    per_b = lambda bi, pt: (bi, 0, 0)
    const2 = lambda bi, pt: (0, 0)
    hbm = pl.BlockSpec(memory_space=pltpu.HBM)
    grid_spec = pltpu.PrefetchScalarGridSpec(
        num_scalar_prefetch=1,
        grid=(b,),
        in_specs=[pl.BlockSpec((1, rows, hd), per_b), pl.BlockSpec((1, rows, hd), per_b), pl.BlockSpec((1, rows, hd), per_b),
                  pl.BlockSpec((1, rows, LANES), per_b), pl.BlockSpec((1, 1, rows), per_b),
                  pl.BlockSpec((rows, cols), const2), pl.BlockSpec((rows, rows), const2),
                  hbm, hbm, hbm, hbm],
        out_specs=pl.BlockSpec((1, rows, hd), per_b),
        scratch_shapes=[pltpu.VMEM((2, n_pages, cols, hd), F32), pltpu.VMEM((2, n_pages, cols, hd), F32),
                        pltpu.VMEM((2, n_pages, 1, cols), F32), pltpu.VMEM((2, n_pages, 1, cols), F32),
                        pltpu.VMEM((n_pages, rows, cols), F32), pltpu.SemaphoreType.DMA((2, 4))],
    )
    return pl.pallas_call(
        functools.partial(_sattn_kernel, scale=hd ** -0.5),
        grid_spec=grid_spec,
        out_shape=jax.ShapeDtypeStruct((b, rows, hd), F32),
        compiler_params=_cparams("arbitrary"),
        name="sample_attention",
    )(page_table, q, kn, vn, cq_col, cq_row, mask_past, mask_new, cache_k, cache_v, dloc, tot)


def _s5_tables_kernel(are_ref, aim_ref, ldt_ref, pre_ref, pim_ref, cre_ref, cim_ref):
    a_re = jnp.minimum(are_ref[...], -1e-4)
    a_im = aim_ref[...]
    dt = jnp.exp(ldt_ref[...])
    mag = jnp.exp(a_re * dt)
    abr = mag * jnp.cos(a_im * dt)
    abi = mag * jnp.sin(a_im * dt)
    den = a_re * a_re + a_im * a_im
    nr, ni = abr - 1.0, abi
    cre_ref[...] = (nr * a_re + ni * a_im) / den
    cim_ref[...] = (ni * a_re - nr * a_im) / den
    pr, pi = abr, abi
    for r in range(SUBLANES):
        pre_ref[r:r + 1, :] = pr
        pim_ref[r:r + 1, :] = pi
        pr, pi = pr * abr - pi * abi, pr * abi + pi * abr


def _s5_tables(a_re, a_im, log_dt):
    g, p = a_re.shape
    n = g * p
    flat = lambda x: x.reshape(1, n)
    ldt = jnp.broadcast_to(log_dt[:, None], (g, p))
    row = jax.ShapeDtypeStruct((1, n), F32)
    tab = jax.ShapeDtypeStruct((SUBLANES, n), F32)
    return pl.pallas_call(
        _s5_tables_kernel,
        out_shape=[tab, tab, row, row],
        name="s5_tables",
    )(flat(a_re), flat(a_im), flat(ldt))


def _s5_bbar_kernel(cre_ref, cim_ref, bre_ref, bim_ref, ore_ref, oim_ref):
    cr, ci = cre_ref[...], cim_ref[...]
    br, bi = bre_ref[...], bim_ref[...]
    ore_ref[...] = cr * br - ci * bi
    oim_ref[...] = cr * bi + ci * br


def _s5_bbar(coef_re, coef_im, b_re, b_im):
    shp = jax.ShapeDtypeStruct(b_re.shape, F32)
    return pl.pallas_call(_s5_bbar_kernel, out_shape=[shp, shp], name="s5_bbar")(
        coef_re[..., None], coef_im[..., None], b_re, b_im)


def _lane_order(re, im):
    lead = re.shape[:-2]
    g, p = re.shape[-2:]
    nch = g // SSM_CHUNK_GROUPS
    cw = SSM_CHUNK_GROUPS * p
    both = jnp.stack([re.reshape(lead + (nch, cw)), im.reshape(lead + (nch, cw))], axis=-2)
    return both.reshape(lead + (2 * g * p,))


def _lane_unorder(x, g, p):
    lead = x.shape[:-1]
    nch = g // SSM_CHUNK_GROUPS
    both = x.reshape(lead + (nch, 2, SSM_CHUNK_GROUPS, p))
    return both[..., 0, :, :].reshape(lead + (g, p)), both[..., 1, :, :].reshape(lead + (g, p))


def _block_diag_b(bbar_re, bbar_im):
    g, p, c = bbar_re.shape
    k = SSM_CHUNK_GROUPS
    nch = g // k
    bt = jnp.stack([bbar_re, bbar_im]).reshape(2, nch, k, p, c).transpose(1, 2, 4, 0, 3)
    eye = jnp.eye(k, dtype=F32)
    full = bt[:, :, :, :, None, :] * eye[None, :, None, None, :, None]
    return full.reshape(nch, k * c, 2 * k * p)


def _block_diag_c(c_re, c_im):
    g, c, p = c_re.shape
    k = SSM_CHUNK_GROUPS
    nch = g // k
    ct = jnp.stack([c_re, -c_im]).reshape(2, nch, k, c, p).transpose(1, 0, 2, 4, 3)
    eye = jnp.eye(k, dtype=F32)
    full = ct[:, :, :, :, None, :] * eye[None, None, :, None, :, None]
    return full.reshape(nch, 2 * k * p, k * c)


def _s5_scan_tiles(hbuf, pw_ref, n_tiles, carry_in, carry_ref, state_ref):
    two_n = hbuf.shape[1]
    cw2 = 2 * SSM_CHUNK_GROUPS * SSM_STATE
    cw = cw2 // 2
    rows = lax.broadcasted_iota(jnp.int32, (SUBLANES, SCAN_LANES), 0)
    for ch in range(two_n // cw2):
        for sub in range(cw // SCAN_LANES):
            re0 = ch * cw2 + sub * SCAN_LANES
            im0 = re0 + cw
            sl_re = pl.ds(re0, SCAN_LANES)
            sl_im = pl.ds(im0, SCAN_LANES)
            pr = pw_ref[:, sl_re]
            pi = pw_ref[:, sl_im]
            steps = []
            for sh in (1, 2, 4):
                ar = jnp.where(rows >= sh, pr[sh - 1:sh, :], 0.0)
                ai = jnp.where(rows >= sh, pi[sh - 1:sh, :], 0.0)
                steps.append((sh, ar, ai))

            def body(t, carry, sl_re=sl_re, sl_im=sl_im, pr=pr, pi=pi, steps=steps):
                r0 = pl.multiple_of(t * SUBLANES, SUBLANES)
                xr = hbuf[pl.ds(r0, SUBLANES), sl_re]
                xi = hbuf[pl.ds(r0, SUBLANES), sl_im]
                for sh, ar, ai in steps:
                    sr = pltpu.roll(xr, sh, 0)
                    si = pltpu.roll(xi, sh, 0)
                    xr, xi = xr + ar * sr - ai * si, xi + ar * si + ai * sr
                if carry_in is None:
                    hr, hi = carry
                else:
                    hr, hi = carry_in(t, sl_re), carry_in(t, sl_im)
                xr, xi = xr + pr * hr - pi * hi, xi + pr * hi + pi * hr
                hbuf[pl.ds(r0, SUBLANES), sl_re] = xr
                hbuf[pl.ds(r0, SUBLANES), sl_im] = xi
                if state_ref is not None:
                    state_ref[pl.ds(t, 1), sl_re] = xr[SUBLANES - 1:, :]
                    state_ref[pl.ds(t, 1), sl_im] = xi[SUBLANES - 1:, :]
                return xr[SUBLANES - 1:, :], xi[SUBLANES - 1:, :]

            if carry_in is None:
                init = (carry_ref[:, sl_re], carry_ref[:, sl_im])
            else:
                init = (jnp.zeros((1, SCAN_LANES), F32), jnp.zeros((1, SCAN_LANES), F32))
            hr, hi = lax.fori_loop(0, n_tiles, body, init)
            if carry_in is None:
                carry_ref[:, sl_re] = hr
                carry_ref[:, sl_im] = hi


def _s5_input(u_ref, bhi_ref, blo_ref, hbuf):
    u = u_ref[...]
    u_hi, u_lo = _split2(u)
    kc = bhi_ref.shape[1]
    nc = bhi_ref.shape[2]
    for ch in range(bhi_ref.shape[0]):
        uh = u_hi[:, ch * kc:(ch + 1) * kc]
        ul = u_lo[:, ch * kc:(ch + 1) * kc]
        hbuf[:, ch * nc:(ch + 1) * nc] = _dot(uh, bhi_ref[ch]) + _dot(uh, blo_ref[ch]) + _dot(ul, bhi_ref[ch])
    return u


def _s5_output(u, hbuf, cc_ref, d_ref, wg_ref, bg_ref, o_ref):
    kc = cc_ref.shape[1]
    ys = [_dot(hbuf[:, ch * kc:(ch + 1) * kc].astype(BF16), cc_ref[ch]) for ch in range(cc_ref.shape[0])]
    y = jnp.concatenate(ys, axis=1) + d_ref[...] * u
    z = jax.nn.gelu(y)
    o_ref[...] = z * jax.nn.sigmoid(_dot(z.astype(BF16), wg_ref[...]) + bg_ref[...])


def _s5_prompt_kernel(u_ref, bhi_ref, blo_ref, pw_ref, cc_ref, d_ref, wg_ref, bg_ref, o_ref, st_ref, hbuf, carry_ref):
    ti = pl.program_id(1)

    @pl.when(ti == 0)
    def _():
        carry_ref[...] = jnp.zeros_like(carry_ref)

    u = _s5_input(u_ref, bhi_ref, blo_ref, hbuf)
    _s5_scan_tiles(hbuf, pw_ref, hbuf.shape[0] // SUBLANES, None, carry_ref, None)
    _s5_output(u, hbuf, cc_ref, d_ref, wg_ref, bg_ref, o_ref)

    @pl.when(ti == pl.num_programs(1) - 1)
    def _():
        st_ref[0] = carry_ref[...]


def _s5_sample_kernel(u_ref, h0_ref, bhi_ref, blo_ref, pw_ref, cc_ref, d_ref, wg_ref, bg_ref, o_ref, st_ref, hbuf):
    u = _s5_input(u_ref, bhi_ref, blo_ref, hbuf)
    _s5_scan_tiles(hbuf, pw_ref, hbuf.shape[0] // SUBLANES, lambda t, sl: h0_ref[pl.ds(t, 1), sl], None, st_ref)
    _s5_output(u, hbuf, cc_ref, d_ref, wg_ref, bg_ref, o_ref)


def _s5_weight_specs(bhi, pw, cc, width):
    return [_const_spec(bhi.shape), _const_spec(bhi.shape), _const_spec(pw.shape), _const_spec(cc.shape),
            _const_spec((1, width)), _const_spec((width, width)), _const_spec((1, width))]


def _s5_prompt(u, n_batch, bhi, blo, pw, cc, d_skip, w_glu, b_glu):
    t, width = u.shape
    per = t // n_batch // TOK_BLOCK
    two_n = pw.shape[1]
    tok = pl.BlockSpec((TOK_BLOCK, width), lambda b, i: (b * per + i, 0))
    return pl.pallas_call(
        _s5_prompt_kernel,
        grid=(n_batch, per),
        in_specs=[tok] + _s5_weight_specs(bhi, pw, cc, width),
        out_specs=[tok, pl.BlockSpec((1, 1, two_n), lambda b, i: (b, 0, 0))],
        out_shape=[jax.ShapeDtypeStruct((t, width), F32), jax.ShapeDtypeStruct((n_batch, 1, two_n), F32)],
        scratch_shapes=[pltpu.VMEM((TOK_BLOCK, two_n), F32), pltpu.VMEM((1, two_n), F32)],
        compiler_params=_cparams("arbitrary", "arbitrary"),
        name="s5_prompt",
    )(u, bhi, blo, pw, cc, d_skip, w_glu, b_glu)


def _s5_sample(u, h0, bhi, blo, pw, cc, d_skip, w_glu, b_glu):
    t, width = u.shape
    two_n = pw.shape[1]
    tiles = TOK_BLOCK // SUBLANES
    tok = pl.BlockSpec((TOK_BLOCK, width), lambda i: (i, 0))
    st = pl.BlockSpec((tiles, two_n), lambda i: (i, 0))
    return pl.pallas_call(
        _s5_sample_kernel,
        grid=(t // TOK_BLOCK,),
        in_specs=[tok, st] + _s5_weight_specs(bhi, pw, cc, width),
        out_specs=[tok, st],
        out_shape=[jax.ShapeDtypeStruct((t, width), F32), jax.ShapeDtypeStruct(h0.shape, F32)],
        scratch_shapes=[pltpu.VMEM((TOK_BLOCK, two_n), F32)],
        compiler_params=_cparams("arbitrary"),
        name="s5_sample",
    )(u, h0, bhi, blo, pw, cc, d_skip, w_glu, b_glu)


def _outproj_kernel(att_ref, ssm_ref, ga_ref, gs_ref, wa_ref, ws_ref, x_ref, gate_ref, o_ref):
    def norm(v, g):
        return (v * lax.rsqrt(jnp.mean(v * v, axis=-1, keepdims=True) + EPS) * g).astype(BF16)

    merged = _dot(norm(att_ref[...], ga_ref[...]), wa_ref[...]) + _dot(norm(ssm_ref[...], gs_ref[...]), ws_ref[...])
    tg, _, d = x_ref.shape
    o_ref[...] = x_ref[...] + gate_ref[...] * merged.reshape(tg, SUBLANES, d)


def _out_projection(att, ssm, g_att, g_ssm, w_att, w_ssm, xg, gate):
    ng, _, d = xg.shape
    tg = TOK_BLOCK // SUBLANES
    steps = ng // tg
    wa = att.shape[1]
    ws = ssm.shape[1]
    return pl.pallas_call(
        _outproj_kernel,
        grid=(steps,),
        in_specs=[pl.BlockSpec((TOK_BLOCK, wa), lambda i: (i, 0)), pl.BlockSpec((TOK_BLOCK, ws), lambda i: (i, 0)),
                  _const_spec((1, wa)), _const_spec((1, ws)), _const_spec((wa, d)), _const_spec((ws, d)),
                  pl.BlockSpec((tg, SUBLANES, d), lambda i: (i, 0, 0)), _mod_spec(gate.shape[0], steps, tg, d)],
        out_specs=pl.BlockSpec((tg, SUBLANES, d), lambda i: (i, 0, 0)),
        out_shape=jax.ShapeDtypeStruct(xg.shape, F32),
        compiler_params=_cparams("arbitrary"),
        name="out_projection",
    )(att, ssm, g_att, g_ssm, w_att, w_ssm, xg, gate)


def _topk_rows(problems, k, row=None):
    scores = [p[0] for p in problems]
    if row is None:
        row = lax.broadcasted_iota(jnp.int32, scores[0].shape, 0).astype(F32)
    for i in range(k):
        for z, (_, val_ref, idx_ref) in enumerate(problems):
            s = scores[z]
            m = jnp.max(s, axis=0, keepdims=True)
            j = jnp.min(jnp.where(s == m, row, ROW_SENTINEL), axis=0, keepdims=True)
            val_ref[i:i + 1, :] = m
            idx_ref[i:i + 1, :] = j
            scores[z] = jnp.where(row == j, -jnp.inf, s)


def _pair_candidates():
    k = PEER_TOPK
    runs = [(0, 0, 1, 0, k), (16, 1, 1, 0, 8), (24, 2, 1, 0, 8), (32, 3, 1, 0, 8),
            (40, 8, 8, 0, 1), (48, 4, 4, 0, 1), (52, 4, 4, 1, 1), (56, 4, 4, 2, 1)]
    covered = set()
    for _, a0, na, b0, nb in runs:
        for a in range(a0, a0 + na):
            for b in range(b0, b0 + nb):
                assert (a, b) not in covered
                covered.add((a, b))
    assert all((a, b) in covered for a in range(k) for b in range(k) if (a + 1) * (b + 1) <= k)
    return runs


def _pair_flat_index():
    k = PEER_TOPK
    r = lax.broadcasted_iota(jnp.int32, (PAIR_ROWS, LANES), 0)
    out = (ROW_SENTINEL / 2 + r).astype(F32)
    for row0, a0, na, b0, nb in _pair_candidates():
        i = r - row0
        flat = (a0 * k + b0 + i) if na == 1 else ((a0 + i) * k + b0)
        out = jnp.where((i >= 0) & (i < na * nb), flat.astype(F32), out)
    return out


def _take_rows(table_ref, idx, n):
    out = jnp.zeros_like(idx)
    for r in range(n):
        out = jnp.where(idx == float(r), table_ref[r:r + 1, :], out)
    return out


def _peer_select_kernel(x_ref, shift_ref, scale_ref, g_ref, wqt_ref, sk_ref, h_ref, e_ref, gw_ref,
                        qt_sc, val_sc, idx_sc, cs_sc, top_sc, pick_sc):
    tg, _, d = x_ref.shape
    h = _rms_mod(x_ref[...], g_ref[...], shift_ref[...], scale_ref[...]).reshape(tg * SUBLANES, d).astype(BF16)
    h_ref[...] = h
    qt_sc[...] = _dot_nt(wqt_ref[...], h).astype(BF16)
    n_sides, n_keys, half = sk_ref.shape
    k = PEER_TOPK
    n_chunks = tg * SUBLANES // LANES
    flat = _pair_flat_index()
    runs = _pair_candidates()
    n_used = max(row0 + na * nb for row0, _, na, _, nb in runs)
    units = [(hg, c) for hg in range(PEER_HEAD_GROUP) for c in range(n_chunks)]
    for z in range(len(units)):
        cs_sc[z, n_used:, :] = jnp.full((PAIR_ROWS - n_used, LANES), -jnp.inf, F32)

    def group_body(group, carry):
        first = []
        for z, (hg, c) in enumerate(units):
            for side in range(2):
                hx = 2 * (group * PEER_HEAD_GROUP + hg) + side
                r0 = pl.multiple_of(hx * half, half)
                s = _dot(sk_ref[hx], qt_sc[pl.ds(r0, half), pl.ds(c * LANES, LANES)])
                first.append((s, val_sc.at[z, side], idx_sc.at[z, side]))
        _topk_rows(first, k)
        second = []
        for z in range(len(units)):
            for row0, a0, na, b0, nb in runs:
                cs_sc[z, row0:row0 + na * nb, :] = val_sc[z, 0, a0:a0 + na, :] + val_sc[z, 1, b0:b0 + nb, :]
            second.append((cs_sc[z], top_sc.at[z], pick_sc.at[z]))
        _topk_rows(second, k, flat)
        for z, (hg, c) in enumerate(units):
            r_out = pl.multiple_of((group * PEER_HEAD_GROUP + hg) * k, k)
            cols = pl.ds(c * LANES, LANES)
            ts = top_sc[z]
            p = jnp.exp(ts - ts[0:1, :])
            gw_ref[pl.ds(r_out, k), cols] = p / jnp.sum(p, axis=0, keepdims=True)
            pick = pick_sc[z]
            a = jnp.floor(pick * (1.0 / k))
            first_key = _take_rows(idx_sc.at[z, 0], a, k)
            second_key = _take_rows(idx_sc.at[z, 1], pick - a * float(k), k)
            e_ref[pl.ds(r_out, k), cols] = (first_key * float(n_keys) + second_key).astype(jnp.int32)
        return carry

    lax.fori_loop(0, n_sides // 2 // PEER_HEAD_GROUP, group_body, 0)


def _peer_select(xg, shift, scale, g, w_q_t, subkeys):
    ng, _, d = xg.shape
    t = ng * SUBLANES
    tg = TOK_BLOCK // SUBLANES
    steps = ng // tg
    n_sides = subkeys.shape[0]
    k = PEER_TOPK
    rows = (n_sides // 2) * k
    nch = PEER_HEAD_GROUP * TOK_BLOCK // LANES
    assert (n_sides // 2) % PEER_HEAD_GROUP == 0
    mod_spec = _mod_spec(shift.shape[0], steps, tg, d)
    return pl.pallas_call(
        _peer_select_kernel,
        grid=(steps,),
        in_specs=[pl.BlockSpec((tg, SUBLANES, d), lambda i: (i, 0, 0)), mod_spec, mod_spec, _const_spec((1, d)),
                  _const_spec(w_q_t.shape), _const_spec(subkeys.shape)],
        out_specs=[pl.BlockSpec((TOK_BLOCK, d), lambda i: (i, 0)), pl.BlockSpec((rows, TOK_BLOCK), lambda i: (0, i)),
                   pl.BlockSpec((rows, TOK_BLOCK), lambda i: (0, i))],
        out_shape=[jax.ShapeDtypeStruct((t, d), BF16), jax.ShapeDtypeStruct((rows, t), jnp.int32),
                   jax.ShapeDtypeStruct((rows, t), F32)],
        scratch_shapes=[pltpu.VMEM((w_q_t.shape[0], TOK_BLOCK), BF16),
                        pltpu.VMEM((nch, 2, k, LANES), F32), pltpu.VMEM((nch, 2, k, LANES), F32),
                        pltpu.VMEM((nch, PAIR_ROWS, LANES), F32),
                        pltpu.VMEM((nch, k, LANES), F32), pltpu.VMEM((nch, k, LANES), F32)],
        compiler_params=_cparams("arbitrary"),
        name="peer_select",
    )(xg, shift, scale, g, w_q_t, subkeys)


def _peer_mask_kernel(e_ref, g_ref, o_ref):
    n_tiles = o_ref.shape[0]
    n2 = o_ref.shape[2]
    n1 = n_tiles * SUBLANES
    shift = n2.bit_length() - 1
    sub1 = lax.broadcasted_iota(jnp.int32, (n1, e_ref.shape[1]), 0)
    sub2 = lax.broadcasted_iota(jnp.int32, (n2, e_ref.shape[1]), 0)

    def body(t, _):
        e = e_ref[pl.ds(t, 1), :]
        gate = g_ref[pl.ds(t, 1), :]
        first = jnp.right_shift(e, shift)
        second = jnp.bitwise_and(e, n2 - 1)
        p_hi, p_lo = _split2(jnp.where(sub1 == first, gate, 0.0))
        r = jnp.where(sub2 == second, 1.0, 0.0).astype(BF16)
        m = _dot_nt(jnp.concatenate([p_hi, p_lo], axis=1), jnp.concatenate([r, r], axis=1))
        r0 = pl.multiple_of(t * SUBLANES, SUBLANES)
        for jb in range(n_tiles):
            o_ref[jb, pl.ds(r0, SUBLANES), :] = m[jb * SUBLANES:(jb + 1) * SUBLANES, :]
        return 0

    lax.fori_loop(0, e_ref.shape[0], body, 0, unroll=8)


def _peer_mask(e_tok, g_tok):
    t, picks = e_tok.shape
    n = PEER_N_KEYS
    assert n & (n - 1) == 0
    tb = MASK_TOK_BLOCK
    return pl.pallas_call(
        _peer_mask_kernel,
        grid=(t // tb,),
        in_specs=[pl.BlockSpec((tb, picks), lambda i: (i, 0)), pl.BlockSpec((tb, picks), lambda i: (i, 0))],
        out_specs=pl.BlockSpec((n // SUBLANES, tb * SUBLANES, n), lambda i: (0, i, 0)),
        out_shape=jax.ShapeDtypeStruct((n // SUBLANES, t * SUBLANES, n), F32),
        compiler_params=_cparams("arbitrary"),
        name="peer_mask",
    )(e_tok, g_tok)


def _peer_ffn_kernel(h_ref, u_ref, v_ref, m_ref, x_ref, gate_ref, gf_ref, o_ref, acc_ref):
    j = pl.program_id(1)

    @pl.when(j == 0)
    def _():
        acc_ref[...] = jnp.zeros_like(acc_ref)

    tm = h_ref.shape[0]
    n2 = m_ref.shape[2]
    a = _dot_nt(h_ref[...], u_ref[...])
    m2 = m_ref.at[0]
    parts = []
    for i1 in range(SUBLANES):
        gate = m2[pl.ds(i1, tm, stride=SUBLANES), :]
        parts.append((gate * jax.nn.gelu(a[:, i1 * n2:(i1 + 1) * n2])).astype(BF16))
    acc_ref[...] += _dot(jnp.concatenate(parts, axis=1), v_ref[...])

    @pl.when(j == pl.num_programs(1) - 1)
    def _():
        tg, _, d = x_ref.shape
        x2 = x_ref[...] + gate_ref[...] * acc_ref[...].reshape(tg, SUBLANES, d)
        ms = jnp.mean(x2 * x2, axis=-1, keepdims=True)
        o_ref[...] = x2 * lax.rsqrt(ms + EPS) * gf_ref[...]


def _peer_ffn(h, exp_u, exp_v, mask, xg, gate, g_final):
    t, d = h.shape
    ne = exp_u.shape[0]
    n = PEER_N_KEYS
    tm = PEER_TOK_BLOCK
    te = PEER_EXPERT_TILE
    assert te == SUBLANES * n and mask.shape == (ne // te, t * SUBLANES, n)
    tg = tm // SUBLANES
    steps = t // tm
    return pl.pallas_call(
        _peer_ffn_kernel,
        grid=(steps, ne // te),
        in_specs=[pl.BlockSpec((tm, d), lambda i, j: (i, 0)),
                  pl.BlockSpec((te, d), lambda i, j: (j, 0)), pl.BlockSpec((te, d), lambda i, j: (j, 0)),
                  pl.BlockSpec((1, tm * SUBLANES, n), lambda i, j: (j, i, 0)),
                  pl.BlockSpec((tg, SUBLANES, d), lambda i, j: (i, 0, 0)),
                  _mod_spec(gate.shape[0], steps, tg, d), _const_spec((1, d))],
        out_specs=pl.BlockSpec((tg, SUBLANES, d), lambda i, j: (i, 0, 0)),
        out_shape=jax.ShapeDtypeStruct(xg.shape, F32),
        scratch_shapes=[pltpu.VMEM((tm, d), F32)],
        compiler_params=_cparams("arbitrary", "arbitrary"),
        name="peer_ffn",
    )(h, exp_u, exp_v, mask, xg, gate, g_final)


def _token_stages_pre(xg, mods, lw, emit_bf16):
    shift1, scale1 = mods[0], mods[1]
    return _in_projection(xg, shift1, scale1, lw["g_norm1"], lw["w_cat"], lw["bf_pad"], lw["fox_width"], emit_bf16)


def _token_stages_post(xg, mods, att, ssm, lw, g_final):
    gate1, shift2, scale2, gate2 = mods[2], mods[3], mods[4], mods[5]
    x1 = _out_projection(att, ssm, lw["g_attn_out"], lw["g_ssm_out"], lw["w_out_att"], lw["w_out_ssm"], xg, gate1)
    h2, e_t, g_t = _peer_select(x1, shift2, scale2, lw["g_norm2"], lw["peer_w_q_t"], lw["peer_subkeys"])
    mask = _peer_mask(e_t.T, g_t.T)
    return _peer_ffn(h2, lw["peer_u"], lw["peer_v"], mask, x1, gate2, g_final)


def kernel(x_prompt, x_sample, cache_k, cache_v, cache_logf, state_ssm_re, state_ssm_im, page_table, c_prompt, c_sample, w_ada, b_ada, g_norm1, g_norm2, w_in, b_forget, ssm_a_re, ssm_a_im, ssm_log_dt, ssm_b_re, ssm_b_im, ssm_c_re, ssm_c_im, ssm_d, w_glu, b_glu, g_attn_out, g_ssm_out, w_out, peer_w_q, peer_subkeys, peer_u, peer_v, g_final):
    depth = w_ada.shape[0]
    assert depth == 1, "single trunk layer"
    nb, seq, d = x_prompt.shape
    nd, dseq, _ = x_sample.shape
    assert dseq == SUBLANES
    n_pool, page, n_heads, hd = cache_k.shape[1:]
    fox_w = n_heads * hd
    n_groups, n_state = ssm_a_re.shape[1:]
    ssm_w = n_groups * SSM_CH_PER_GROUP
    assert fox_w == ssm_w
    n_pages = page_table.shape[1]
    layer = 0

    w_in_l = w_in[layer]
    w_cat = jnp.concatenate([w_in_l[:, :3 * fox_w], w_in_l[:, 3 * fox_w + n_heads:], w_in_l[:, 3 * fox_w:3 * fox_w + n_heads],
                             jnp.zeros((d, LANES - n_heads), F32)], axis=1).astype(BF16)
    bf_pad = jnp.concatenate([b_forget[layer], jnp.zeros((LANES - n_heads,), F32)]).reshape(1, LANES)
    w_out_l = w_out[layer].astype(BF16)
    lw = dict(
        fox_width=fox_w, w_cat=w_cat, bf_pad=bf_pad,
        g_norm1=g_norm1[layer].reshape(1, d), g_norm2=g_norm2[layer].reshape(1, d),
        g_attn_out=g_attn_out[layer].reshape(1, fox_w), g_ssm_out=g_ssm_out[layer].reshape(1, ssm_w),
        w_out_att=w_out_l[:fox_w], w_out_ssm=w_out_l[fox_w:],
        peer_w_q_t=peer_w_q[layer].T.astype(BF16),
        peer_subkeys=peer_subkeys[layer].reshape((-1,) + peer_subkeys.shape[-2:]).astype(BF16),
        peer_u=peer_u[layer].astype(BF16), peer_v=peer_v[layer].astype(BF16),
    )
    gf = g_final.reshape(1, d)

    n_c = nb + nd
    n_c_pad = -(-n_c // SUBLANES) * SUBLANES
    c_all = jnp.concatenate([c_prompt, c_sample, jnp.zeros((n_c_pad - n_c, d), F32)], axis=0)
    mod = _ada_modulation(c_all, w_ada[layer], b_ada[layer])
    mods_p = [mod[:nb, i * d:(i + 1) * d].reshape(nb, 1, d) for i in range(N_MOD)]
    mods_s = [mod[nb:n_c, i * d:(i + 1) * d].reshape(nd, 1, d) for i in range(N_MOD)]

    pw_re, pw_im, coef_re, coef_im = _s5_tables(ssm_a_re[layer], ssm_a_im[layer], ssm_log_dt[layer])
    bbar_re, bbar_im = _s5_bbar(coef_re.reshape(n_groups, n_state), coef_im.reshape(n_groups, n_state),
                                ssm_b_re[layer], ssm_b_im[layer])
    b_blk = _block_diag_b(bbar_re, bbar_im)
    b_hi = b_blk.astype(BF16)
    b_lo = (b_blk - b_hi.astype(F32)).astype(BF16)
    c_blk = _block_diag_c(ssm_c_re[layer], ssm_c_im[layer]).astype(BF16)
    pw = _lane_order(pw_re.reshape(SUBLANES, n_groups, n_state), pw_im.reshape(SUBLANES, n_groups, n_state))
    s5w = (b_hi, b_lo, pw, c_blk, ssm_d[layer].reshape(1, ssm_w), w_glu[layer].astype(BF16), b_glu[layer].reshape(1, ssm_w))

    xg_p = x_prompt.reshape(nb * seq // SUBLANES, SUBLANES, d)
    q_p, k_p, v_p, u_p, lf_p, qb_p, kb_p, vb_p = _token_stages_pre(xg_p, mods_p, lw, True)
    fcum = _cumsum_time(lf_p, nb).reshape(nb, seq, LANES)[:, :, :n_heads].transpose(0, 2, 1)
    fq = jnp.broadcast_to(fcum[..., None], (nb, n_heads, seq, LANES))
    fk = fcum.reshape(nb, n_heads, seq // ATT_BLOCK, ATT_BLOCK)
    as3 = lambda a: a.reshape(nb, seq, fox_w)
    att_p = _prompt_attention(as3(qb_p), as3(kb_p), as3(vb_p), fq, fk, n_heads).reshape(nb * seq, fox_w)
    ssm_p, st_p = _s5_prompt(u_p, nb, *s5w)
    y_p = _token_stages_post(xg_p, mods_p, att_p, ssm_p, lw, gf)
    re_p, im_p = _lane_unorder(st_p.reshape(nb, -1), n_groups, n_state)

    xg_s = x_sample.reshape(nd, SUBLANES, d)
    q_s, k_s, v_s, u_s, lf_s = _token_stages_pre(xg_s, mods_s, lw, False)
    rows = dseq * n_heads
    heads_rows = lambda a: a.reshape(nd, dseq, n_heads, hd).reshape(nd, rows, hd)
    logf_pages = cache_logf[layer].reshape(n_pool, page * n_heads)
    dloc, tot = _page_sums(logf_pages, n_heads)
    c_new = _cumsum_short(lf_s.reshape(nd, dseq, LANES))[:, :, :n_heads].reshape(nd, rows)
    cq_col = jnp.broadcast_to(c_new[:, :, None], (nd, rows, LANES))
    cq_row = c_new.reshape(nd, 1, rows)
    att_s = _sample_attention(page_table, heads_rows(q_s), heads_rows(k_s), heads_rows(v_s),
                              cache_k[layer].reshape(n_pool, page * n_heads, hd),
                              cache_v[layer].reshape(n_pool, page * n_heads, hd),
                              dloc.reshape(n_pool, 1, page * n_heads), tot.reshape(n_pool, 1, page * n_heads),
                              cq_col, cq_row, n_heads)
    att_s = att_s.reshape(nd, dseq, n_heads, hd).reshape(nd * dseq, fox_w)
    h0 = _lane_order(state_ssm_re[layer], state_ssm_im[layer])
    ssm_s, st_s = _s5_sample(u_s, h0, *s5w)
    y_s = _token_stages_post(xg_s, mods_s, att_s, ssm_s, lw, gf)
    re_s, im_s = _lane_unorder(st_s, n_groups, n_state)

    kv5 = lambda a, b_, l_: a.reshape(1, b_, l_, n_heads, hd)
    return (y_p.reshape(nb, seq, d), y_s.reshape(nd, dseq, d),
            kv5(k_p, nb, seq), kv5(v_p, nb, seq), lf_p[:, :n_heads].reshape(1, nb, seq, n_heads),
            re_p[None], im_p[None],
            kv5(k_s, nd, dseq), kv5(v_s, nd, dseq), lf_s[:, :n_heads].reshape(1, nd, dseq, n_heads),
            re_s[None], im_s[None])
```

```python
import functools
import math

import jax
import jax.numpy as jnp
from jax import lax
from jax.experimental import pallas as pl
from jax.experimental.pallas import tpu as pltpu

F32 = jnp.float32
BF16 = jnp.bfloat16
EPS = 1e-6
NEG = -1e30

SUBLANES = 8
LANES = 128
VMEM_LIMIT_BYTES = 56 * 1024 * 1024

N_MOD = 6
FOX_HEAD_DIM = 128
SSM_CH_PER_GROUP = 16
SSM_STATE = 64
SSM_CHUNK_GROUPS = 16
PEER_N_KEYS = 128
PEER_TOPK = 16
PEER_EXPERT_TILE = 1024
PEER_HEAD_GROUP = 2
PAIR_ROWS = 64
ROW_SENTINEL = 1024.0

TOK_BLOCK = 256
ATT_BLOCK = 512
ATT_HEAD_GROUP = 4
LOG2E = 1.4426950408889634
PEER_TOK_BLOCK = 512
MASK_TOK_BLOCK = 64
SCAN_LANES = 512


def _cparams(*sem):
    return pltpu.CompilerParams(dimension_semantics=sem, vmem_limit_bytes=VMEM_LIMIT_BYTES)


def _const_spec(shape):
    nd = len(shape)
    return pl.BlockSpec(shape, lambda *_: (0,) * nd)


def _split2(x):
    hi = x.astype(BF16)
    lo = (x - hi.astype(F32)).astype(BF16)
    return hi, lo


def _split3(x):
    x1 = x.astype(BF16)
    r1 = x - x1.astype(F32)
    x2 = r1.astype(BF16)
    x3 = (r1 - x2.astype(F32)).astype(BF16)
    return x1, x2, x3


def _dot(a, b):
    return jnp.dot(a, b, preferred_element_type=F32)


def _dot_nt(a, b):
    return lax.dot_general(a, b, (((1,), (1,)), ((), ())), preferred_element_type=F32)


def _rms_mod(xg, g, shift, scale):
    ms = jnp.mean(xg * xg, axis=-1, keepdims=True)
    y = xg * lax.rsqrt(ms + EPS) * g
    return y * (1.0 + scale) + shift


def _mod_spec(n_mod_rows, n_steps, tg, d, block_of=lambda i: i):
    if n_mod_rows == n_steps * tg:
        return pl.BlockSpec((tg, 1, d), lambda i, *_: (block_of(i), 0, 0))
    assert n_steps % n_mod_rows == 0
    per = n_steps // n_mod_rows
    return pl.BlockSpec((1, 1, d), lambda i, *_: (block_of(i) // per, 0, 0))


def _ada_kernel(c_ref, w_ref, b_ref, o_ref):
    a = jax.nn.silu(c_ref[...]).astype(BF16)
    o_ref[...] = _dot(a, w_ref[...].astype(BF16)) + b_ref[...]


def _ada_modulation(c_all, w_ada, b_ada):
    m, d = c_all.shape
    n = w_ada.shape[1]
    tn = 1024
    return pl.pallas_call(
        _ada_kernel,
        grid=(n // tn,),
        in_specs=[_const_spec((m, d)), pl.BlockSpec((d, tn), lambda j: (0, j)), pl.BlockSpec((1, tn), lambda j: (0, j))],
        out_specs=pl.BlockSpec((m, tn), lambda j: (0, j)),
        out_shape=jax.ShapeDtypeStruct((m, n), F32),
        compiler_params=_cparams("arbitrary"),
        name="ada_modulation",
    )(c_all, w_ada, b_ada.reshape(1, n))


def _inproj_kernel(x_ref, shift_ref, scale_ref, g_ref, wqkv_ref, wu_ref, wfg_ref, bf_ref,
                   q_ref, k_ref, v_ref, u_ref, lf_ref, *bf16_refs):
    tg, _, d = x_ref.shape
    h = _rms_mod(x_ref[...], g_ref[...], shift_ref[...], scale_ref[...]).reshape(tg * SUBLANES, d).astype(BF16)
    r = _dot(h, wqkv_ref[...])
    w = q_ref.shape[1]
    q_ref[...] = r[:, 0:w]
    k_ref[...] = r[:, w:2 * w]
    v_ref[...] = r[:, 2 * w:3 * w]
    u_ref[...] = _dot(h, wu_ref[...])
    lf_ref[...] = jax.nn.log_sigmoid(_dot(h, wfg_ref[...]) + bf_ref[...])
    for i, ref in enumerate(bf16_refs):
        ref[...] = r[:, i * w:(i + 1) * w].astype(BF16)


def _in_projection(xg, shift, scale, g, w_qkv, w_u, w_fg, bf_pad, emit_bf16):
    ng, _, d = xg.shape
    t = ng * SUBLANES
    tg = TOK_BLOCK // SUBLANES
    steps = ng // tg
    width = w_u.shape[1]
    mod_spec = _mod_spec(shift.shape[0], steps, tg, d)
    tok = lambda w: pl.BlockSpec((TOK_BLOCK, w), lambda i: (i, 0))
    n_bf16 = 3 if emit_bf16 else 0
    return pl.pallas_call(
        _inproj_kernel,
        grid=(steps,),
        in_specs=[pl.BlockSpec((tg, SUBLANES, d), lambda i: (i, 0, 0)), mod_spec, mod_spec,
                  _const_spec((1, d)), _const_spec(w_qkv.shape), _const_spec(w_u.shape), _const_spec(w_fg.shape),
                  _const_spec((1, LANES))],
        out_specs=[tok(width)] * 4 + [tok(LANES)] + [tok(width)] * n_bf16,
        out_shape=([jax.ShapeDtypeStruct((t, width), F32)] * 4 + [jax.ShapeDtypeStruct((t, LANES), F32)]
                   + [jax.ShapeDtypeStruct((t, width), BF16)] * n_bf16),
        compiler_params=_cparams("arbitrary"),
        name="in_projection",
    )(xg, shift, scale, g, w_qkv, w_u, w_fg, bf_pad)


def _cumsum_kernel(x_ref, tri_ref, o_ref, carry_ref):
    @pl.when(pl.program_id(1) == 0)
    def _():
        carry_ref[...] = jnp.zeros_like(carry_ref)

    x1, x2, x3 = _split3(x_ref[...])
    tri = tri_ref[...]
    loc = _dot(tri, x1) + _dot(tri, x2) + _dot(tri, x3) + carry_ref[...]
    o_ref[...] = loc
    carry_ref[...] = loc[-1:, :]


def _cumsum_time(x, n_batch):
    t, w = x.shape
    l = t // n_batch
    tb = 256
    tri = (lax.broadcasted_iota(jnp.int32, (tb, tb), 1) <= lax.broadcasted_iota(jnp.int32, (tb, tb), 0)).astype(BF16)
    per = l // tb
    return pl.pallas_call(
        _cumsum_kernel,
        grid=(n_batch, per),
        in_specs=[pl.BlockSpec((tb, w), lambda b, i: (b * per + i, 0)), _const_spec((tb, tb))],
        out_specs=pl.BlockSpec((tb, w), lambda b, i: (b * per + i, 0)),
        out_shape=jax.ShapeDtypeStruct((t, w), F32),
        scratch_shapes=[pltpu.VMEM((1, w), F32)],
        compiler_params=_cparams("arbitrary", "arbitrary"),
        name="logf_cumsum",
    )(x, tri)


def _cumsum_short_kernel(x_ref, o_ref):
    run = x_ref[:, 0:1, :]
    o_ref[:, 0:1, :] = run
    for r in range(1, x_ref.shape[1]):
        run = run + x_ref[:, r:r + 1, :]
        o_ref[:, r:r + 1, :] = run


def _cumsum_short(x):
    return pl.pallas_call(_cumsum_short_kernel, out_shape=jax.ShapeDtypeStruct(x.shape, F32), name="logf_cumsum_new")(x)


def _pattn_kernel(q_ref, k_ref, v_ref, fq_ref, fk_ref, o_ref, m_sc, l_sc, acc_sc, fq_sc, *, scale):
    qi = pl.program_id(2)
    tq = q_ref.shape[1]
    n_group, _, hd = acc_sc.shape
    m_sc[...] = jnp.full_like(m_sc, NEG)
    l_sc[...] = jnp.zeros_like(l_sc)
    acc_sc[...] = jnp.zeros_like(acc_sc)
    fq_sc[...] = fq_ref[0] * LOG2E
    c = scale * LOG2E

    def block(kj, diagonal):
        start = pl.multiple_of(kj * tq, tq)
        for g in range(n_group):
            hs = slice(g * hd, (g + 1) * hd)
            s = _dot_nt(q_ref[0, :, hs], k_ref[0, pl.ds(start, tq), hs])
            fk2 = fk_ref[0, g, pl.ds(kj, 1), :] * LOG2E
            tiles = []
            for t in range(tq // LANES):
                ls = slice(t * LANES, (t + 1) * LANES)
                st = s[:, ls] * c - fk2[:, ls]
                if diagonal:
                    row = lax.broadcasted_iota(jnp.int32, st.shape, 0)
                    col = lax.broadcasted_iota(jnp.int32, st.shape, 1) + t * LANES
                    st = jnp.where(col <= row, st, NEG)
                tiles.append(st)
            m_prev = m_sc[g]
            fq2 = fq_sc[g]
            m_new = jnp.maximum(m_prev, jnp.max(functools.reduce(jnp.maximum, tiles), axis=1, keepdims=True) + fq2)
            alpha = jnp.exp2(m_prev - m_new)
            shift = fq2 - m_new
            ps = [jnp.exp2(st + shift) for st in tiles]
            l_sc[g] = alpha * l_sc[g] + functools.reduce(jnp.add, ps)
            p = jnp.concatenate([x.astype(BF16) for x in ps], axis=1)
            acc_sc[g] = alpha * acc_sc[g] + _dot(p, v_ref[0, pl.ds(start, tq), hs])
            m_sc[g] = m_new

    def off_diagonal(kj, carry):
        block(kj, False)
        return carry

    lax.fori_loop(0, qi, off_diagonal, 0)
    block(qi, True)
    for g in range(n_group):
        o_ref[0, :, g * hd:(g + 1) * hd] = acc_sc[g] / jnp.sum(l_sc[g], axis=1, keepdims=True)


def _prompt_attention(q, k, v, fq, fk, n_heads):
    b, l, w = q.shape
    hd = w // n_heads
    tq = ATT_BLOCK
    ng = ATT_HEAD_GROUP
    assert fk.shape == (b, n_heads, l // tq, tq) and n_heads % ng == 0
    return pl.pallas_call(
        functools.partial(_pattn_kernel, scale=hd ** -0.5),
        grid=(b, n_heads // ng, l // tq),
        in_specs=[pl.BlockSpec((1, tq, ng * hd), lambda bi, h, i: (bi, i, h)),
                  pl.BlockSpec((1, l, ng * hd), lambda bi, h, i: (bi, 0, h)),
                  pl.BlockSpec((1, l, ng * hd), lambda bi, h, i: (bi, 0, h)),
                  pl.BlockSpec((1, ng, tq, LANES), lambda bi, h, i: (bi, h, i, 0)),
                  pl.BlockSpec((1, ng, l // tq, tq), lambda bi, h, i: (bi, h, 0, 0))],
        out_specs=pl.BlockSpec((1, tq, ng * hd), lambda bi, h, i: (bi, i, h)),
        out_shape=jax.ShapeDtypeStruct((b, l, w), F32),
        scratch_shapes=[pltpu.VMEM((ng, tq, LANES), F32), pltpu.VMEM((ng, tq, LANES), F32), pltpu.VMEM((ng, tq, hd), F32),
                        pltpu.VMEM((ng, tq, LANES), F32)],
        compiler_params=_cparams("arbitrary", "arbitrary", "arbitrary"),
        name="prompt_attention",
    )(q, k, v, fq, fk)


def _page_sums_kernel(x_ref, m_after_ref, m_all_ref, dloc_ref, tot_ref):
    x1, x2, x3 = _split3(x_ref[...])
    ma = m_after_ref[...]
    mt = m_all_ref[...]
    dloc_ref[...] = _dot(x1, ma) + _dot(x2, ma) + _dot(x3, ma)
    tot_ref[...] = _dot(x1, mt) + _dot(x2, mt) + _dot(x3, mt)


def _page_sums(logf_pages, n_heads):
    n_pool, w = logf_pages.shape
    r = lax.broadcasted_iota(jnp.int32, (w, w), 0)
    c = lax.broadcasted_iota(jnp.int32, (w, w), 1)
    same_head = (r % n_heads) == (c % n_heads)
    m_after = (same_head & (r // n_heads > c // n_heads)).astype(BF16)
    m_all = same_head.astype(BF16)
    tp = 256
    return pl.pallas_call(
        _page_sums_kernel,
        grid=(n_pool // tp,),
        in_specs=[pl.BlockSpec((tp, w), lambda i: (i, 0)), _const_spec((w, w)), _const_spec((w, w))],
        out_specs=[pl.BlockSpec((tp, w), lambda i: (i, 0))] * 2,
        out_shape=[jax.ShapeDtypeStruct((n_pool, w), F32)] * 2,
        compiler_params=_cparams("arbitrary"),
        name="page_logf_sums",
    )(logf_pages, m_after, m_all)


def _lane_fold(x, op):
    out = x[:, 0:LANES]
    for c in range(1, x.shape[1] // LANES):
        out = op(out, x[:, c * LANES:(c + 1) * LANES])
    return out


def _sattn_kernel(pt_ref, q_ref, kn_ref, vn_ref, cqc_ref, cqr_ref, mb_ref, mbn_ref, ck_hbm, cv_hbm, dloc_hbm, tot_hbm,
                  o_ref, kbuf, vbuf, dbuf, tbuf, sbuf, sems, *, scale):
    b = pl.program_id(0)
    n_pages = kbuf.shape[1]
    slot = lax.rem(b, 2)

    def page_copies(bb, sl, i):
        p = pt_ref[bb, i]
        return (pltpu.make_async_copy(ck_hbm.at[p], kbuf.at[sl, i], sems.at[sl, 0]),
                pltpu.make_async_copy(cv_hbm.at[p], vbuf.at[sl, i], sems.at[sl, 1]),
                pltpu.make_async_copy(dloc_hbm.at[pl.ds(p, 1)], dbuf.at[sl, i], sems.at[sl, 2]),
                pltpu.make_async_copy(tot_hbm.at[pl.ds(p, 1)], tbuf.at[sl, i], sems.at[sl, 3]))

    def fetch(bb, sl):
        for i in range(n_pages):
            for cp in page_copies(bb, sl, i):
                cp.start()

    @pl.when(b == 0)
    def _():
        fetch(0, 0)

    @pl.when(b + 1 < pl.num_programs(0))
    def _():
        fetch(b + 1, 1 - slot)

    for i in range(n_pages):
        for cp in page_copies(b, slot, i):
            cp.wait()

    q = q_ref[0].astype(BF16)
    cq = cqc_ref[0][:, 0:1]
    base = mb_ref[...] + cq

    def scores(k, carry):
        after, mrun = carry
        i = n_pages - 1 - k
        s = _dot_nt(q, kbuf[slot, i].astype(BF16)) * scale + base + (dbuf[slot, i] + after)
        sbuf[i] = s
        return after + tbuf[slot, i], jnp.maximum(mrun, _lane_fold(s, jnp.maximum))

    rows = q.shape[0]
    init = (jnp.zeros(dbuf.shape[2:], F32), jnp.full((rows, LANES), NEG, F32))
    _, mrun = lax.fori_loop(0, n_pages, scores, init, unroll=True)
    sn = _dot_nt(q, kn_ref[0].astype(BF16)) * scale + (cq - cqr_ref[0]) + mbn_ref[...]
    m = jnp.maximum(jnp.max(mrun, axis=1, keepdims=True), jnp.max(sn, axis=1, keepdims=True))

    def values(i, carry):
        lrun, acc = carry
        p = jnp.exp(sbuf[i] - m)
        return lrun + _lane_fold(p, jnp.add), acc + _dot(p.astype(BF16), vbuf[slot, i].astype(BF16))

    init = (jnp.zeros((rows, LANES), F32), jnp.zeros((rows, q.shape[1]), F32))
    lrun, acc = lax.fori_loop(0, n_pages, values, init, unroll=True)
    pn = jnp.exp(sn - m)
    l = jnp.sum(lrun, axis=1, keepdims=True) + jnp.sum(pn, axis=1, keepdims=True)
    acc = acc + _dot(pn.astype(BF16), vn_ref[0].astype(BF16))
    o_ref[0] = acc / l


def _sample_attention(page_table, q, kn, vn, cache_k, cache_v, dloc, tot, cq_col, cq_row, n_heads):
    b, rows, hd = q.shape
    n_pages = page_table.shape[1]
    cols = cache_k.shape[1]
    r = lax.broadcasted_iota(jnp.int32, (rows, cols), 0)
    c = lax.broadcasted_iota(jnp.int32, (rows, cols), 1)
    mask_past = jnp.where((r % n_heads) == (c % n_heads), 0.0, NEG).astype(F32)
    rn = lax.broadcasted_iota(jnp.int32, (rows, rows), 0)
    cn = lax.broadcasted_iota(jnp.int32, (rows, rows), 1)
    mask_new = jnp.where(((rn % n_heads) == (cn % n_heads)) & (cn // n_heads <= rn // n_heads), 0.0, NEG).astype(F32)
    per_b = lambda bi, pt: (bi, 0, 0)
    const2 = lambda bi, pt: (0, 0)
    hbm = pl.BlockSpec(memory_space=pltpu.HBM)
    grid_spec = pltpu.PrefetchScalarGridSpec(
        num_scalar_prefetch=1,
        grid=(b,),
        in_specs=[pl.BlockSpec((1, rows, hd), per_b), pl.BlockSpec((1, rows, hd), per_b), pl.BlockSpec((1, rows, hd), per_b),
                  pl.BlockSpec((1, rows, LANES), per_b), pl.BlockSpec((1, 1, rows), per_b),
                  pl.BlockSpec((rows, cols), const2), pl.BlockSpec((rows, rows), const2),
                  hbm, hbm, hbm, hbm],
        out_specs=pl.BlockSpec((1, rows, hd), per_b),
        scratch_shapes=[pltpu.VMEM((2, n_pages, cols, hd), F32), pltpu.VMEM((2, n_pages, cols, hd), F32),
                        pltpu.VMEM((2, n_pages, 1, cols), F32), pltpu.VMEM((2, n_pages, 1, cols), F32),
                        pltpu.VMEM((n_pages, rows, cols), F32), pltpu.SemaphoreType.DMA((2, 4))],
    )
    return pl.pallas_call(
        functools.partial(_sattn_kernel, scale=hd ** -0.5),
        grid_spec=grid_spec,
        out_shape=jax.ShapeDtypeStruct((b, rows, hd), F32),
        compiler_params=_cparams("arbitrary"),
        name="sample_attention",
    )(page_table, q, kn, vn, cq_col, cq_row, mask_past, mask_new, cache_k, cache_v, dloc, tot)


def _s5_tables_kernel(are_ref, aim_ref, ldt_ref, pre_ref, pim_ref, cre_ref, cim_ref):
    a_re = jnp.minimum(are_ref[...], -1e-4)
    a_im = aim_ref[...]
    dt = jnp.exp(ldt_ref[...])
    mag = jnp.exp(a_re * dt)
    abr = mag * jnp.cos(a_im * dt)
    abi = mag * jnp.sin(a_im * dt)
    den = a_re * a_re + a_im * a_im
    nr, ni = abr - 1.0, abi
    cre_ref[...] = (nr * a_re + ni * a_im) / den
    cim_ref[...] = (ni * a_re - nr * a_im) / den
    pr, pi = abr, abi
    for r in range(SUBLANES):
        pre_ref[r:r + 1, :] = pr
        pim_ref[r:r + 1, :] = pi
        pr, pi = pr * abr - pi * abi, pr * abi + pi * abr


def _s5_tables(a_re, a_im, log_dt):
    g, p = a_re.shape
    n = g * p
    flat = lambda x: x.reshape(1, n)
    ldt = jnp.broadcast_to(log_dt[:, None], (g, p))
    row = jax.ShapeDtypeStruct((1, n), F32)
    tab = jax.ShapeDtypeStruct((SUBLANES, n), F32)
    return pl.pallas_call(
        _s5_tables_kernel,
        out_shape=[tab, tab, row, row],
        name="s5_tables",
    )(flat(a_re), flat(a_im), flat(ldt))


def _s5_bbar_kernel(cre_ref, cim_ref, bre_ref, bim_ref, ore_ref, oim_ref):
    cr, ci = cre_ref[...], cim_ref[...]
    br, bi = bre_ref[...], bim_ref[...]
    ore_ref[...] = cr * br - ci * bi
    oim_ref[...] = cr * bi + ci * br


def _s5_bbar(coef_re, coef_im, b_re, b_im):
    shp = jax.ShapeDtypeStruct(b_re.shape, F32)
    return pl.pallas_call(_s5_bbar_kernel, out_shape=[shp, shp], name="s5_bbar")(
        coef_re[..., None], coef_im[..., None], b_re, b_im)


def _lane_order(re, im):
    lead = re.shape[:-2]
    g, p = re.shape[-2:]
    nch = g // SSM_CHUNK_GROUPS
    cw = SSM_CHUNK_GROUPS * p
    both = jnp.stack([re.reshape(lead + (nch, cw)), im.reshape(lead + (nch, cw))], axis=-2)
    return both.reshape(lead + (2 * g * p,))


def _lane_unorder(x, g, p):
    lead = x.shape[:-1]
    nch = g // SSM_CHUNK_GROUPS
    both = x.reshape(lead + (nch, 2, SSM_CHUNK_GROUPS, p))
    return both[..., 0, :, :].reshape(lead + (g, p)), both[..., 1, :, :].reshape(lead + (g, p))


def _block_diag_b(bbar_re, bbar_im):
    g, p, c = bbar_re.shape
    k = SSM_CHUNK_GROUPS
    nch = g // k
    cols = 2 * k * p
    src = jnp.stack([bbar_re, bbar_im]).reshape(2, nch, k, p, c).transpose(1, 4, 0, 2, 3).reshape(nch, 1, c, cols)
    col_group = (lax.broadcasted_iota(jnp.int32, (k, 1, cols), 2) % (k * p)) // p
    keep = col_group == lax.broadcasted_iota(jnp.int32, (k, 1, cols), 0)
    return jnp.where(keep[None], src, 0.0).reshape(nch, k * c, cols)


def _block_diag_c(c_re, c_im):
    g, c, p = c_re.shape
    k = SSM_CHUNK_GROUPS
    nch = g // k
    rows = 2 * k * p
    src = jnp.stack([c_re, -c_im]).reshape(2, nch, k, c, p).transpose(1, 0, 2, 4, 3).reshape(nch, rows, c)
    row_group = (lax.broadcasted_iota(jnp.int32, (rows, k * c), 0) % (k * p)) // p
    keep = row_group == lax.broadcasted_iota(jnp.int32, (rows, k * c), 1) // c
    return jnp.where(keep[None], jnp.tile(src, (1, 1, k)), 0.0)


def _s5_scan_tiles(hbuf, pw_ref, n_tiles, carry_in, carry_ref, state_ref):
    two_n = hbuf.shape[1]
    cw2 = 2 * SSM_CHUNK_GROUPS * SSM_STATE
    cw = cw2 // 2
    rows = lax.broadcasted_iota(jnp.int32, (SUBLANES, SCAN_LANES), 0)
    for ch in range(two_n // cw2):
        for sub in range(cw // SCAN_LANES):
            re0 = ch * cw2 + sub * SCAN_LANES
            im0 = re0 + cw
            sl_re = pl.ds(re0, SCAN_LANES)
            sl_im = pl.ds(im0, SCAN_LANES)
            pr = pw_ref[:, sl_re]
            pi = pw_ref[:, sl_im]
            steps = []
            for sh in (1, 2, 4):
                ar = jnp.where(rows >= sh, pr[sh - 1:sh, :], 0.0)
                ai = jnp.where(rows >= sh, pi[sh - 1:sh, :], 0.0)
                steps.append((sh, ar, ai))

            def body(t, carry, sl_re=sl_re, sl_im=sl_im, pr=pr, pi=pi, steps=steps):
                r0 = pl.multiple_of(t * SUBLANES, SUBLANES)
                xr = hbuf[pl.ds(r0, SUBLANES), sl_re]
                xi = hbuf[pl.ds(r0, SUBLANES), sl_im]
                for sh, ar, ai in steps:
                    sr = pltpu.roll(xr, sh, 0)
                    si = pltpu.roll(xi, sh, 0)
                    xr, xi = xr + ar * sr - ai * si, xi + ar * si + ai * sr
                if carry_in is None:
                    hr, hi = carry
                else:
                    hr, hi = carry_in(t, sl_re), carry_in(t, sl_im)
                xr, xi = xr + pr * hr - pi * hi, xi + pr * hi + pi * hr
                hbuf[pl.ds(r0, SUBLANES), sl_re] = xr
                hbuf[pl.ds(r0, SUBLANES), sl_im] = xi
                if state_ref is not None:
                    state_ref[pl.ds(t, 1), sl_re] = xr[SUBLANES - 1:, :]
                    state_ref[pl.ds(t, 1), sl_im] = xi[SUBLANES - 1:, :]
                return xr[SUBLANES - 1:, :], xi[SUBLANES - 1:, :]

            if carry_in is None:
                init = (carry_ref[:, sl_re], carry_ref[:, sl_im])
            else:
                init = (jnp.zeros((1, SCAN_LANES), F32), jnp.zeros((1, SCAN_LANES), F32))
            hr, hi = lax.fori_loop(0, n_tiles, body, init)
            if carry_in is None:
                carry_ref[:, sl_re] = hr
                carry_ref[:, sl_im] = hi


def _s5_input(u_ref, bhi_ref, blo_ref, hbuf):
    u = u_ref[...]
    u_hi, u_lo = _split2(u)
    kc = bhi_ref.shape[1]
    nc = bhi_ref.shape[2]
    for ch in range(bhi_ref.shape[0]):
        uh = u_hi[:, ch * kc:(ch + 1) * kc]
        ul = u_lo[:, ch * kc:(ch + 1) * kc]
        hbuf[:, ch * nc:(ch + 1) * nc] = _dot(uh, bhi_ref[ch]) + _dot(uh, blo_ref[ch]) + _dot(ul, bhi_ref[ch])
    return u


def _s5_output(u, hbuf, cc_ref, d_ref, wg_ref, bg_ref, o_ref):
    kc = cc_ref.shape[1]
    ys = [_dot(hbuf[:, ch * kc:(ch + 1) * kc].astype(BF16), cc_ref[ch]) for ch in range(cc_ref.shape[0])]
    y = jnp.concatenate(ys, axis=1) + d_ref[...] * u
    z = jax.nn.gelu(y)
    o_ref[...] = z * jax.nn.sigmoid(_dot(z.astype(BF16), wg_ref[...]) + bg_ref[...])


def _s5_prompt_kernel(u_ref, bhi_ref, blo_ref, pw_ref, cc_ref, d_ref, wg_ref, bg_ref, o_ref, st_ref, hbuf, carry_ref):
    ti = pl.program_id(1)

    @pl.when(ti == 0)
    def _():
        carry_ref[...] = jnp.zeros_like(carry_ref)

    u = _s5_input(u_ref, bhi_ref, blo_ref, hbuf)
    _s5_scan_tiles(hbuf, pw_ref, hbuf.shape[0] // SUBLANES, None, carry_ref, None)
    _s5_output(u, hbuf, cc_ref, d_ref, wg_ref, bg_ref, o_ref)

    @pl.when(ti == pl.num_programs(1) - 1)
    def _():
        st_ref[0] = carry_ref[...]


def _s5_sample_kernel(u_ref, h0_ref, bhi_ref, blo_ref, pw_ref, cc_ref, d_ref, wg_ref, bg_ref, o_ref, st_ref, hbuf):
    u = _s5_input(u_ref, bhi_ref, blo_ref, hbuf)
    _s5_scan_tiles(hbuf, pw_ref, hbuf.shape[0] // SUBLANES, lambda t, sl: h0_ref[pl.ds(t, 1), sl], None, st_ref)
    _s5_output(u, hbuf, cc_ref, d_ref, wg_ref, bg_ref, o_ref)


def _s5_weight_specs(bhi, pw, cc, width):
    return [_const_spec(bhi.shape), _const_spec(bhi.shape), _const_spec(pw.shape), _const_spec(cc.shape),
            _const_spec((1, width)), _const_spec((width, width)), _const_spec((1, width))]


def _s5_prompt(u, n_batch, bhi, blo, pw, cc, d_skip, w_glu, b_glu):
    t, width = u.shape
    per = t // n_batch // TOK_BLOCK
    two_n = pw.shape[1]
    tok = pl.BlockSpec((TOK_BLOCK, width), lambda b, i: (b * per + i, 0))
    return pl.pallas_call(
        _s5_prompt_kernel,
        grid=(n_batch, per),
        in_specs=[tok] + _s5_weight_specs(bhi, pw, cc, width),
        out_specs=[tok, pl.BlockSpec((1, 1, two_n), lambda b, i: (b, 0, 0))],
        out_shape=[jax.ShapeDtypeStruct((t, width), F32), jax.ShapeDtypeStruct((n_batch, 1, two_n), F32)],
        scratch_shapes=[pltpu.VMEM((TOK_BLOCK, two_n), F32), pltpu.VMEM((1, two_n), F32)],
        compiler_params=_cparams("arbitrary", "arbitrary"),
        name="s5_prompt",
    )(u, bhi, blo, pw, cc, d_skip, w_glu, b_glu)


def _s5_sample(u, h0, bhi, blo, pw, cc, d_skip, w_glu, b_glu):
    t, width = u.shape
    two_n = pw.shape[1]
    tiles = TOK_BLOCK // SUBLANES
    tok = pl.BlockSpec((TOK_BLOCK, width), lambda i: (i, 0))
    st = pl.BlockSpec((tiles, two_n), lambda i: (i, 0))
    return pl.pallas_call(
        _s5_sample_kernel,
        grid=(t // TOK_BLOCK,),
        in_specs=[tok, st] + _s5_weight_specs(bhi, pw, cc, width),
        out_specs=[tok, st],
        out_shape=[jax.ShapeDtypeStruct((t, width), F32), jax.ShapeDtypeStruct(h0.shape, F32)],
        scratch_shapes=[pltpu.VMEM((TOK_BLOCK, two_n), F32)],
        compiler_params=_cparams("arbitrary"),
        name="s5_sample",
    )(u, h0, bhi, blo, pw, cc, d_skip, w_glu, b_glu)


def _outproj_kernel(att_ref, ssm_ref, ga_ref, gs_ref, wa_ref, ws_ref, x_ref, gate_ref, o_ref):
    def norm(v, g):
        return (v * lax.rsqrt(jnp.mean(v * v, axis=-1, keepdims=True) + EPS) * g).astype(BF16)

    merged = _dot(norm(att_ref[...], ga_ref[...]), wa_ref[...]) + _dot(norm(ssm_ref[...], gs_ref[...]), ws_ref[...])
    tg, _, d = x_ref.shape
    o_ref[...] = x_ref[...] + gate_ref[...] * merged.reshape(tg, SUBLANES, d)


def _out_projection(att, ssm, g_att, g_ssm, w_att, w_ssm, xg, gate):
    ng, _, d = xg.shape
    tg = TOK_BLOCK // SUBLANES
    steps = ng // tg
    wa = att.shape[1]
    ws = ssm.shape[1]
    return pl.pallas_call(
        _outproj_kernel,
        grid=(steps,),
        in_specs=[pl.BlockSpec((TOK_BLOCK, wa), lambda i: (i, 0)), pl.BlockSpec((TOK_BLOCK, ws), lambda i: (i, 0)),
                  _const_spec((1, wa)), _const_spec((1, ws)), _const_spec((wa, d)), _const_spec((ws, d)),
                  pl.BlockSpec((tg, SUBLANES, d), lambda i: (i, 0, 0)), _mod_spec(gate.shape[0], steps, tg, d)],
        out_specs=pl.BlockSpec((tg, SUBLANES, d), lambda i: (i, 0, 0)),
        out_shape=jax.ShapeDtypeStruct(xg.shape, F32),
        compiler_params=_cparams("arbitrary"),
        name="out_projection",
    )(att, ssm, g_att, g_ssm, w_att, w_ssm, xg, gate)


def _topk_rows(problems, k, row=None):
    scores = [p[0] for p in problems]
    if row is None:
        row = lax.broadcasted_iota(jnp.int32, scores[0].shape, 0).astype(F32)
    for i in range(k):
        for z, (_, val_ref, idx_ref) in enumerate(problems):
            s = scores[z]
            m = jnp.max(s, axis=0, keepdims=True)
            j = jnp.min(jnp.where(s == m, row, ROW_SENTINEL), axis=0, keepdims=True)
            val_ref[i:i + 1, :] = m
            idx_ref[i:i + 1, :] = j
            scores[z] = jnp.where(row == j, -jnp.inf, s)


def _pair_candidates():
    k = PEER_TOPK
    runs = [(0, 0, 1, 0, k), (16, 1, 1, 0, 8), (24, 2, 1, 0, 8), (32, 3, 1, 0, 8),
            (40, 8, 8, 0, 1), (48, 4, 4, 0, 1), (52, 4, 4, 1, 1), (56, 4, 4, 2, 1)]
    covered = set()
    for _, a0, na, b0, nb in runs:
        for a in range(a0, a0 + na):
            for b in range(b0, b0 + nb):
                assert (a, b) not in covered
                covered.add((a, b))
    assert all((a, b) in covered for a in range(k) for b in range(k) if (a + 1) * (b + 1) <= k)
    return runs


def _pair_flat_index():
    k = PEER_TOPK
    r = lax.broadcasted_iota(jnp.int32, (PAIR_ROWS, LANES), 0)
    out = (ROW_SENTINEL / 2 + r).astype(F32)
    for row0, a0, na, b0, nb in _pair_candidates():
        i = r - row0
        flat = (a0 * k + b0 + i) if na == 1 else ((a0 + i) * k + b0)
        out = jnp.where((i >= 0) & (i < na * nb), flat.astype(F32), out)
    return out


def _take_rows(table_ref, idx, n):
    out = jnp.zeros_like(idx)
    for r in range(n):
        out = jnp.where(idx == float(r), table_ref[r:r + 1, :], out)
    return out


def _peer_select_kernel(x_ref, shift_ref, scale_ref, xn_ref, shiftn_ref, scalen_ref, g_ref, wqt_ref, sk_ref,
                        h_ref, e_ref, gw_ref, qt_sc, qn_sc, hc_sc, hn_sc, val_sc, idx_sc, cs_sc, top_sc, pick_sc):
    tg, _, d = x_ref.shape
    n_sides, n_keys, half = sk_ref.shape
    n_groups = n_sides // 2 // PEER_HEAD_GROUP
    q_rows = qt_sc.shape[0] // n_groups

    def normed(x, shift, scale):
        return _rms_mod(x[...], g_ref[...], shift[...], scale[...]).reshape(tg * SUBLANES, d).astype(BF16)

    @pl.when(pl.program_id(0) == 0)
    def _():
        h0 = normed(x_ref, shift_ref, scale_ref)
        hc_sc[...] = h0
        qt_sc[...] = _dot_nt(wqt_ref[...], h0).astype(BF16)

    h_ref[...] = hc_sc[...]
    hn_sc[...] = normed(xn_ref, shiftn_ref, scalen_ref)
    k = PEER_TOPK
    n_chunks = tg * SUBLANES // LANES
    flat = _pair_flat_index()
    runs = _pair_candidates()
    n_used = max(row0 + na * nb for row0, _, na, _, nb in runs)
    units = [(hg, c) for hg in range(PEER_HEAD_GROUP) for c in range(n_chunks)]
    for z in range(len(units)):
        cs_sc[z, n_used:, :] = jnp.full((PAIR_ROWS - n_used, LANES), -jnp.inf, F32)

    def group_body(group, carry):
        first = []
        for z, (hg, c) in enumerate(units):
            for side in range(2):
                hx = 2 * (group * PEER_HEAD_GROUP + hg) + side
                r0 = pl.multiple_of(hx * half, half)
                s = _dot(sk_ref[hx], qt_sc[pl.ds(r0, half), pl.ds(c * LANES, LANES)])
                first.append((s, val_sc.at[z, side], idx_sc.at[z, side]))
        _topk_rows(first, k)
        second = []
        for z in range(len(units)):
            for row0, a0, na, b0, nb in runs:
                cs_sc[z, row0:row0 + na * nb, :] = val_sc[z, 0, a0:a0 + na, :] + val_sc[z, 1, b0:b0 + nb, :]
            second.append((cs_sc[z], top_sc.at[z], pick_sc.at[z]))
        _topk_rows(second, k, flat)
        for z, (hg, c) in enumerate(units):
            r_out = pl.multiple_of((group * PEER_HEAD_GROUP + hg) * k, k)
            cols = pl.ds(c * LANES, LANES)
            ts = top_sc[z]
            p = jnp.exp(ts - ts[0:1, :])
            gw_ref[pl.ds(r_out, k), cols] = p / jnp.sum(p, axis=0, keepdims=True)
            pick = pick_sc[z]
            a = jnp.floor(pick * (1.0 / k))
            first_key = _take_rows(idx_sc.at[z, 0], a, k)
            second_key = _take_rows(idx_sc.at[z, 1], pick - a * float(k), k)
            e_ref[pl.ds(r_out, k), cols] = (first_key * float(n_keys) + second_key).astype(jnp.int32)
        q_slab = pl.ds(pl.multiple_of(group * q_rows, q_rows), q_rows)
        qn_sc[q_slab, :] = _dot_nt(wqt_ref[q_slab, :], hn_sc[...]).astype(BF16)
        return carry

    lax.fori_loop(0, n_groups, group_body, 0)
    qt_sc[...] = qn_sc[...]
    hc_sc[...] = hn_sc[...]


def _peer_select(xg, shift, scale, g, w_q_t, subkeys):
    ng, _, d = xg.shape
    t = ng * SUBLANES
    tg = TOK_BLOCK // SUBLANES
    steps = ng // tg
    n_sides = subkeys.shape[0]
    k = PEER_TOPK
    rows = (n_sides // 2) * k
    nch = PEER_HEAD_GROUP * TOK_BLOCK // LANES
    assert (n_sides // 2) % PEER_HEAD_GROUP == 0
    mod_spec = _mod_spec(shift.shape[0], steps, tg, d)
    following = lambda i: jnp.minimum(i + 1, steps - 1)
    mod_next = _mod_spec(shift.shape[0], steps, tg, d, following)
    x_spec = pl.BlockSpec((tg, SUBLANES, d), lambda i: (i, 0, 0))
    x_next = pl.BlockSpec((tg, SUBLANES, d), lambda i: (following(i), 0, 0))
    q_tile = pltpu.VMEM((w_q_t.shape[0], TOK_BLOCK), BF16)
    h_tile = pltpu.VMEM((TOK_BLOCK, d), BF16)
    return pl.pallas_call(
        _peer_select_kernel,
        grid=(steps,),
        in_specs=[x_spec, mod_spec, mod_spec, x_next, mod_next, mod_next, _const_spec((1, d)),
                  _const_spec(w_q_t.shape), _const_spec(subkeys.shape)],
        out_specs=[pl.BlockSpec((TOK_BLOCK, d), lambda i: (i, 0)), pl.BlockSpec((rows, TOK_BLOCK), lambda i: (0, i)),
                   pl.BlockSpec((rows, TOK_BLOCK), lambda i: (0, i))],
        out_shape=[jax.ShapeDtypeStruct((t, d), BF16), jax.ShapeDtypeStruct((rows, t), jnp.int32),
                   jax.ShapeDtypeStruct((rows, t), F32)],
        scratch_shapes=[q_tile, q_tile, h_tile, h_tile,
                        pltpu.VMEM((nch, 2, k, LANES), F32), pltpu.VMEM((nch, 2, k, LANES), F32),
                        pltpu.VMEM((nch, PAIR_ROWS, LANES), F32),
                        pltpu.VMEM((nch, k, LANES), F32), pltpu.VMEM((nch, k, LANES), F32)],
        compiler_params=_cparams("arbitrary"),
        name="peer_select",
    )(xg, shift, scale, xg, shift, scale, g, w_q_t, subkeys)


def _peer_mask_kernel(e_ref, g_ref, o_ref):
    n_tiles = o_ref.shape[0]
    n2 = o_ref.shape[2]
    n1 = n_tiles * SUBLANES
    shift = n2.bit_length() - 1
    as_bf16 = lambda i: i.astype(F32).astype(BF16)
    sub1 = as_bf16(lax.broadcasted_iota(jnp.int32, (n1, e_ref.shape[1]), 0))
    sub2 = as_bf16(lax.broadcasted_iota(jnp.int32, (n2, e_ref.shape[1]), 0))
    zero = jnp.zeros((), BF16)

    def body(t, _):
        e = e_ref[pl.ds(t, 1), :]
        g_hi, g_lo = _split2(g_ref[pl.ds(t, 1), :])
        hit1 = sub1 == as_bf16(jnp.right_shift(e, shift))
        hit2 = sub2 == as_bf16(jnp.bitwise_and(e, n2 - 1))
        r = jnp.where(hit2, jnp.ones((), BF16), zero)
        lhs = jnp.concatenate([jnp.where(hit1, g_hi, zero), jnp.where(hit1, g_lo, zero)], axis=1)
        m = _dot_nt(lhs, jnp.concatenate([r, r], axis=1))
        r0 = pl.multiple_of(t * SUBLANES, SUBLANES)
        for jb in range(n_tiles):
            o_ref[jb, pl.ds(r0, SUBLANES), :] = m[jb * SUBLANES:(jb + 1) * SUBLANES, :]
        return 0

    lax.fori_loop(0, e_ref.shape[0], body, 0, unroll=32)


def _peer_mask(e_tok, g_tok):
    t, picks = e_tok.shape
    n = PEER_N_KEYS
    assert n & (n - 1) == 0 and n <= 256
    tb = MASK_TOK_BLOCK
    return pl.pallas_call(
        _peer_mask_kernel,
        grid=(t // tb,),
        in_specs=[pl.BlockSpec((tb, picks), lambda i: (i, 0)), pl.BlockSpec((tb, picks), lambda i: (i, 0))],
        out_specs=pl.BlockSpec((n // SUBLANES, tb * SUBLANES, n), lambda i: (0, i, 0)),
        out_shape=jax.ShapeDtypeStruct((n // SUBLANES, t * SUBLANES, n), F32),
        compiler_params=_cparams("arbitrary"),
        name="peer_mask",
    )(e_tok, g_tok)


def _peer_ffn_kernel(h_ref, u_ref, v_ref, m_ref, x_ref, gate_ref, gf_ref, o_ref, acc_ref):
    j = pl.program_id(1)

    @pl.when(j == 0)
    def _():
        acc_ref[...] = jnp.zeros_like(acc_ref)

    tm = h_ref.shape[0]
    n2 = m_ref.shape[2]
    a = _dot_nt(h_ref[...], u_ref[...])
    m2 = m_ref.at[0]
    parts = []
    for i1 in range(SUBLANES):
        gate = m2[pl.ds(i1, tm, stride=SUBLANES), :]
        parts.append((gate * jax.nn.gelu(a[:, i1 * n2:(i1 + 1) * n2])).astype(BF16))
    acc_ref[...] += _dot(jnp.concatenate(parts, axis=1), v_ref[...])

    @pl.when(j == pl.num_programs(1) - 1)
    def _():
        tg, _, d = x_ref.shape
        x2 = x_ref[...] + gate_ref[...] * acc_ref[...].reshape(tg, SUBLANES, d)
        ms = jnp.mean(x2 * x2, axis=-1, keepdims=True)
        o_ref[...] = x2 * lax.rsqrt(ms + EPS) * gf_ref[...]


def _peer_ffn(h, exp_u, exp_v, mask, xg, gate, g_final):
    t, d = h.shape
    ne = exp_u.shape[0]
    n = PEER_N_KEYS
    tm = PEER_TOK_BLOCK
    te = PEER_EXPERT_TILE
    assert te == SUBLANES * n and mask.shape == (ne // te, t * SUBLANES, n)
    tg = tm // SUBLANES
    steps = t // tm
    return pl.pallas_call(
        _peer_ffn_kernel,
        grid=(steps, ne // te),
        in_specs=[pl.BlockSpec((tm, d), lambda i, j: (i, 0)),
                  pl.BlockSpec((te, d), lambda i, j: (j, 0)), pl.BlockSpec((te, d), lambda i, j: (j, 0)),
                  pl.BlockSpec((1, tm * SUBLANES, n), lambda i, j: (j, i, 0)),
                  pl.BlockSpec((tg, SUBLANES, d), lambda i, j: (i, 0, 0)),
                  _mod_spec(gate.shape[0], steps, tg, d), _const_spec((1, d))],
        out_specs=pl.BlockSpec((tg, SUBLANES, d), lambda i, j: (i, 0, 0)),
        out_shape=jax.ShapeDtypeStruct(xg.shape, F32),
        scratch_shapes=[pltpu.VMEM((tm, d), F32)],
        compiler_params=_cparams("arbitrary", "arbitrary"),
        name="peer_ffn",
    )(h, exp_u, exp_v, mask, xg, gate, g_final)


def _token_stages_pre(xg, mods, lw, emit_bf16):
    shift1, scale1 = mods[0], mods[1]
    return _in_projection(xg, shift1, scale1, lw["g_norm1"], lw["w_qkv"], lw["w_u"], lw["w_fg"], lw["bf_pad"], emit_bf16)


def _token_stages_post(xg, mods, att, ssm, lw, g_final):
    gate1, shift2, scale2, gate2 = mods[2], mods[3], mods[4], mods[5]
    x1 = _out_projection(att, ssm, lw["g_attn_out"], lw["g_ssm_out"], lw["w_out_att"], lw["w_out_ssm"], xg, gate1)
    h2, e_t, g_t = _peer_select(x1, shift2, scale2, lw["g_norm2"], lw["peer_w_q_t"], lw["peer_subkeys"])
    mask = _peer_mask(e_t.T, g_t.T)
    return _peer_ffn(h2, lw["peer_u"], lw["peer_v"], mask, x1, gate2, g_final)


def kernel(x_prompt, x_sample, cache_k, cache_v, cache_logf, state_ssm_re, state_ssm_im, page_table, c_prompt, c_sample, w_ada, b_ada, g_norm1, g_norm2, w_in, b_forget, ssm_a_re, ssm_a_im, ssm_log_dt, ssm_b_re, ssm_b_im, ssm_c_re, ssm_c_im, ssm_d, w_glu, b_glu, g_attn_out, g_ssm_out, w_out, peer_w_q, peer_subkeys, peer_u, peer_v, g_final):
    depth = w_ada.shape[0]
    assert depth == 1, "single trunk layer"
    nb, seq, d = x_prompt.shape
    nd, dseq, _ = x_sample.shape
    assert dseq == SUBLANES
    n_pool, page, n_heads, hd = cache_k.shape[1:]
    fox_w = n_heads * hd
    n_groups, n_state = ssm_a_re.shape[1:]
    ssm_w = n_groups * SSM_CH_PER_GROUP
    assert fox_w == ssm_w
    n_pages = page_table.shape[1]
    layer = 0

    w_in_l = w_in[layer]
    w_fg = jnp.pad(w_in_l[:, 3 * fox_w:3 * fox_w + n_heads], ((0, 0), (0, LANES - n_heads))).astype(BF16)
    bf_pad = jnp.concatenate([b_forget[layer], jnp.zeros((LANES - n_heads,), F32)]).reshape(1, LANES)
    w_out_l = w_out[layer].astype(BF16)
    lw = dict(
        w_qkv=w_in_l[:, :3 * fox_w].astype(BF16), w_u=w_in_l[:, 3 * fox_w + n_heads:].astype(BF16), w_fg=w_fg, bf_pad=bf_pad,
        g_norm1=g_norm1[layer].reshape(1, d), g_norm2=g_norm2[layer].reshape(1, d),
        g_attn_out=g_attn_out[layer].reshape(1, fox_w), g_ssm_out=g_ssm_out[layer].reshape(1, ssm_w),
        w_out_att=w_out_l[:fox_w], w_out_ssm=w_out_l[fox_w:],
        peer_w_q_t=peer_w_q[layer].T.astype(BF16),
        peer_subkeys=peer_subkeys[layer].reshape((-1,) + peer_subkeys.shape[-2:]).astype(BF16),
        peer_u=peer_u[layer].astype(BF16), peer_v=peer_v[layer].astype(BF16),
    )
    gf = g_final.reshape(1, d)

    n_c = nb + nd
    n_c_pad = -(-n_c // SUBLANES) * SUBLANES
    c_all = jnp.concatenate([c_prompt, c_sample, jnp.zeros((n_c_pad - n_c, d), F32)], axis=0)
    mod = _ada_modulation(c_all, w_ada[layer], b_ada[layer])
    mods_p = [mod[:nb, i * d:(i + 1) * d].reshape(nb, 1, d) for i in range(N_MOD)]
    mods_s = [mod[nb:n_c, i * d:(i + 1) * d].reshape(nd, 1, d) for i in range(N_MOD)]

    pw_re, pw_im, coef_re, coef_im = _s5_tables(ssm_a_re[layer], ssm_a_im[layer], ssm_log_dt[layer])
    bbar_re, bbar_im = _s5_bbar(coef_re.reshape(n_groups, n_state), coef_im.reshape(n_groups, n_state),
                                ssm_b_re[layer], ssm_b_im[layer])
    b_blk = _block_diag_b(bbar_re, bbar_im)
    b_hi = b_blk.astype(BF16)
    b_lo = (b_blk - b_hi.astype(F32)).astype(BF16)
    c_blk = _block_diag_c(ssm_c_re[layer], ssm_c_im[layer]).astype(BF16)
    pw = _lane_order(pw_re.reshape(SUBLANES, n_groups, n_state), pw_im.reshape(SUBLANES, n_groups, n_state))
    s5w = (b_hi, b_lo, pw, c_blk, ssm_d[layer].reshape(1, ssm_w), w_glu[layer].astype(BF16), b_glu[layer].reshape(1, ssm_w))

    xg_p = x_prompt.reshape(nb * seq // SUBLANES, SUBLANES, d)
    q_p, k_p, v_p, u_p, lf_p, qb_p, kb_p, vb_p = _token_stages_pre(xg_p, mods_p, lw, True)
    fcum = _cumsum_time(lf_p, nb).reshape(nb, seq, LANES)[:, :, :n_heads].transpose(0, 2, 1)
    fq = jnp.broadcast_to(fcum[..., None], (nb, n_heads, seq, LANES))
    fk = fcum.reshape(nb, n_heads, seq // ATT_BLOCK, ATT_BLOCK)
    as3 = lambda a: a.reshape(nb, seq, fox_w)
    att_p = _prompt_attention(as3(qb_p), as3(kb_p), as3(vb_p), fq, fk, n_heads).reshape(nb * seq, fox_w)
    ssm_p, st_p = _s5_prompt(u_p, nb, *s5w)
    y_p = _token_stages_post(xg_p, mods_p, att_p, ssm_p, lw, gf)
    re_p, im_p = _lane_unorder(st_p.reshape(nb, -1), n_groups, n_state)

    xg_s = x_sample.reshape(nd, SUBLANES, d)
    q_s, k_s, v_s, u_s, lf_s = _token_stages_pre(xg_s, mods_s, lw, False)
    rows = dseq * n_heads
    heads_rows = lambda a: a.reshape(nd, dseq, n_heads, hd).reshape(nd, rows, hd)
    logf_pages = cache_logf[layer].reshape(n_pool, page * n_heads)
    dloc, tot = _page_sums(logf_pages, n_heads)
    c_new = _cumsum_short(lf_s.reshape(nd, dseq, LANES))[:, :, :n_heads].reshape(nd, rows)
    cq_col = jnp.broadcast_to(c_new[:, :, None], (nd, rows, LANES))
    cq_row = c_new.reshape(nd, 1, rows)
    att_s = _sample_attention(page_table, heads_rows(q_s), heads_rows(k_s), heads_rows(v_s),
                              cache_k[layer].reshape(n_pool, page * n_heads, hd),
                              cache_v[layer].reshape(n_pool, page * n_heads, hd),
                              dloc, tot, cq_col, cq_row, n_heads)
    att_s = att_s.reshape(nd, dseq, n_heads, hd).reshape(nd * dseq, fox_w)
    h0 = _lane_order(state_ssm_re[layer], state_ssm_im[layer])
    ssm_s, st_s = _s5_sample(u_s, h0, *s5w)
    y_s = _token_stages_post(xg_s, mods_s, att_s, ssm_s, lw, gf)
    re_s, im_s = _lane_unorder(st_s, n_groups, n_state)

    kv5 = lambda a, b_, l_: a.reshape(1, b_, l_, n_heads, hd)
    return (y_p.reshape(nb, seq, d), y_s.reshape(nd, dseq, d),
            kv5(k_p, nb, seq), kv5(v_p, nb, seq), lf_p[:, :n_heads].reshape(1, nb, seq, n_heads),
            re_p[None], im_p[None],
            kv5(k_s, nd, dseq), kv5(v_s, nd, dseq), lf_s[:, :n_heads].reshape(1, nd, dseq, n_heads),
            re_s[None], im_s[None])
```

```python
import functools
import math

import jax
import jax.numpy as jnp
from jax import lax
from jax.experimental import pallas as pl
from jax.experimental.pallas import tpu as pltpu

F32 = jnp.float32
BF16 = jnp.bfloat16
EPS = 1e-6
NEG = -1e30

SUBLANES = 8
LANES = 128
VMEM_LIMIT_BYTES = 56 * 1024 * 1024

N_MOD = 6
FOX_HEAD_DIM = 128
SSM_CH_PER_GROUP = 16
SSM_STATE = 64
SSM_CHUNK_GROUPS = 16
PEER_N_KEYS = 128
PEER_TOPK = 16
PEER_EXPERT_TILE = 1024
PEER_HEAD_GROUP = 2
PAIR_ROWS = 64
ROW_SENTINEL = 1024.0

TOK_BLOCK = 256
ATT_BLOCK = 512
ATT_HEAD_GROUP = 4
LOG2E = 1.4426950408889634
PEER_TOK_BLOCK = 512
MASK_TOK_BLOCK = 64
SCAN_LANES = 512


def _cparams(*sem):
    return pltpu.CompilerParams(dimension_semantics=sem, vmem_limit_bytes=VMEM_LIMIT_BYTES)


def _const_spec(shape):
    nd = len(shape)
    return pl.BlockSpec(shape, lambda *_: (0,) * nd)


def _split2(x):
    hi = x.astype(BF16)
    lo = (x - hi.astype(F32)).astype(BF16)
    return hi, lo


def _split3(x):
    x1 = x.astype(BF16)
    r1 = x - x1.astype(F32)
    x2 = r1.astype(BF16)
    x3 = (r1 - x2.astype(F32)).astype(BF16)
    return x1, x2, x3


def _dot(a, b):
    return jnp.dot(a, b, preferred_element_type=F32)


def _dot_nt(a, b):
    return lax.dot_general(a, b, (((1,), (1,)), ((), ())), preferred_element_type=F32)


def _rms_mod(xg, g, shift, scale):
    ms = jnp.mean(xg * xg, axis=-1, keepdims=True)
    y = xg * lax.rsqrt(ms + EPS) * g
    return y * (1.0 + scale) + shift


def _mod_spec(n_mod_rows, n_steps, tg, d, block_of=lambda i: i):
    if n_mod_rows == n_steps * tg:
        return pl.BlockSpec((tg, 1, d), lambda i, *_: (block_of(i), 0, 0))
    assert n_steps % n_mod_rows == 0
    per = n_steps // n_mod_rows
    return pl.BlockSpec((1, 1, d), lambda i, *_: (block_of(i) // per, 0, 0))


def _ada_kernel(c_ref, w_ref, b_ref, o_ref):
    a = jax.nn.silu(c_ref[...]).astype(BF16)
    o_ref[...] = _dot(a, w_ref[...].astype(BF16)) + b_ref[...]


def _ada_modulation(c_all, w_ada, b_ada):
    m, d = c_all.shape
    n = w_ada.shape[1]
    tn = 1024
    return pl.pallas_call(
        _ada_kernel,
        grid=(n // tn,),
        in_specs=[_const_spec((m, d)), pl.BlockSpec((d, tn), lambda j: (0, j)), pl.BlockSpec((1, tn), lambda j: (0, j))],
        out_specs=pl.BlockSpec((m, tn), lambda j: (0, j)),
        out_shape=jax.ShapeDtypeStruct((m, n), F32),
        compiler_params=_cparams("arbitrary"),
        name="ada_modulation",
    )(c_all, w_ada, b_ada.reshape(1, n))


def _inproj_kernel(x_ref, shift_ref, scale_ref, g_ref, wqkv_ref, wu_ref, wfg_ref, bf_ref,
                   q_ref, k_ref, v_ref, u_ref, lf_ref, *bf16_refs):
    tg, _, d = x_ref.shape
    h = _rms_mod(x_ref[...], g_ref[...], shift_ref[...], scale_ref[...]).reshape(tg * SUBLANES, d).astype(BF16)
    r = _dot(h, wqkv_ref[...])
    w = q_ref.shape[1]
    q_ref[...] = r[:, 0:w]
    k_ref[...] = r[:, w:2 * w]
    v_ref[...] = r[:, 2 * w:3 * w]
    u_ref[...] = _dot(h, wu_ref[...])
    lf_ref[...] = jax.nn.log_sigmoid(_dot(h, wfg_ref[...]) + bf_ref[...])
    for i, ref in enumerate(bf16_refs):
        ref[...] = r[:, i * w:(i + 1) * w].astype(BF16)


def _in_projection(xg, shift, scale, g, w_qkv, w_u, w_fg, bf_pad, emit_bf16):
    ng, _, d = xg.shape
    t = ng * SUBLANES
    tg = TOK_BLOCK // SUBLANES
    steps = ng // tg
    width = w_u.shape[1]
    mod_spec = _mod_spec(shift.shape[0], steps, tg, d)
    tok = lambda w: pl.BlockSpec((TOK_BLOCK, w), lambda i: (i, 0))
    n_bf16 = 3 if emit_bf16 else 0
    return pl.pallas_call(
        _inproj_kernel,
        grid=(steps,),
        in_specs=[pl.BlockSpec((tg, SUBLANES, d), lambda i: (i, 0, 0)), mod_spec, mod_spec,
                  _const_spec((1, d)), _const_spec(w_qkv.shape), _const_spec(w_u.shape), _const_spec(w_fg.shape),
                  _const_spec((1, LANES))],
        out_specs=[tok(width)] * 4 + [tok(LANES)] + [tok(width)] * n_bf16,
        out_shape=([jax.ShapeDtypeStruct((t, width), F32)] * 4 + [jax.ShapeDtypeStruct((t, LANES), F32)]
                   + [jax.ShapeDtypeStruct((t, width), BF16)] * n_bf16),
        compiler_params=_cparams("arbitrary"),
        name="in_projection",
    )(xg, shift, scale, g, w_qkv, w_u, w_fg, bf_pad)


def _cumsum_kernel(x_ref, tri_ref, o_ref, carry_ref):
    @pl.when(pl.program_id(1) == 0)
    def _():
        carry_ref[...] = jnp.zeros_like(carry_ref)

    x1, x2, x3 = _split3(x_ref[...])
    tri = tri_ref[...]
    loc = _dot(tri, x1) + _dot(tri, x2) + _dot(tri, x3) + carry_ref[...]
    o_ref[...] = loc
    carry_ref[...] = loc[-1:, :]


def _cumsum_time(x, n_batch):
    t, w = x.shape
    l = t // n_batch
    tb = 256
    tri = (lax.broadcasted_iota(jnp.int32, (tb, tb), 1) <= lax.broadcasted_iota(jnp.int32, (tb, tb), 0)).astype(BF16)
    per = l // tb
    return pl.pallas_call(
        _cumsum_kernel,
        grid=(n_batch, per),
        in_specs=[pl.BlockSpec((tb, w), lambda b, i: (b * per + i, 0)), _const_spec((tb, tb))],
        out_specs=pl.BlockSpec((tb, w), lambda b, i: (b * per + i, 0)),
        out_shape=jax.ShapeDtypeStruct((t, w), F32),
        scratch_shapes=[pltpu.VMEM((1, w), F32)],
        compiler_params=_cparams("arbitrary", "arbitrary"),
        name="logf_cumsum",
    )(x, tri)


def _cumsum_short_kernel(x_ref, o_ref):
    run = x_ref[:, 0:1, :]
    o_ref[:, 0:1, :] = run
    for r in range(1, x_ref.shape[1]):
        run = run + x_ref[:, r:r + 1, :]
        o_ref[:, r:r + 1, :] = run


def _cumsum_short(x):
    return pl.pallas_call(_cumsum_short_kernel, out_shape=jax.ShapeDtypeStruct(x.shape, F32), name="logf_cumsum_new")(x)


def _pattn_kernel(q_ref, k_ref, v_ref, fq_ref, fk_ref, o_ref, m_sc, l_sc, acc_sc, fq_sc, *, scale):
    qi = pl.program_id(2)
    tq = q_ref.shape[1]
    n_group, _, hd = acc_sc.shape
    m_sc[...] = jnp.full_like(m_sc, NEG)
    l_sc[...] = jnp.zeros_like(l_sc)
    acc_sc[...] = jnp.zeros_like(acc_sc)
    fq_sc[...] = fq_ref[0] * LOG2E
    c = scale * LOG2E

    def block(kj, diagonal):
        start = pl.multiple_of(kj * tq, tq)
        for g in range(n_group):
            hs = slice(g * hd, (g + 1) * hd)
            s = _dot_nt(q_ref[0, :, hs], k_ref[0, pl.ds(start, tq), hs])
            fk2 = fk_ref[0, g, pl.ds(kj, 1), :] * LOG2E
            tiles = []
            for t in range(tq // LANES):
                ls = slice(t * LANES, (t + 1) * LANES)
                st = s[:, ls] * c - fk2[:, ls]
                if diagonal:
                    row = lax.broadcasted_iota(jnp.int32, st.shape, 0)
                    col = lax.broadcasted_iota(jnp.int32, st.shape, 1) + t * LANES
                    st = jnp.where(col <= row, st, NEG)
                tiles.append(st)
            m_prev = m_sc[g]
            fq2 = fq_sc[g]
            m_new = jnp.maximum(m_prev, jnp.max(functools.reduce(jnp.maximum, tiles), axis=1, keepdims=True) + fq2)
            alpha = jnp.exp2(m_prev - m_new)
            shift = fq2 - m_new
            ps = [jnp.exp2(st + shift) for st in tiles]
            l_sc[g] = alpha * l_sc[g] + functools.reduce(jnp.add, ps)
            p = jnp.concatenate([x.astype(BF16) for x in ps], axis=1)
            acc_sc[g] = alpha * acc_sc[g] + _dot(p, v_ref[0, pl.ds(start, tq), hs])
            m_sc[g] = m_new

    def off_diagonal(kj, carry):
        block(kj, False)
        return carry

    lax.fori_loop(0, qi, off_diagonal, 0)
    block(qi, True)
    for g in range(n_group):
        o_ref[0, :, g * hd:(g + 1) * hd] = acc_sc[g] / jnp.sum(l_sc[g], axis=1, keepdims=True)


def _prompt_attention(q, k, v, fq, fk, n_heads):
    b, l, w = q.shape
    hd = w // n_heads
    tq = ATT_BLOCK
    ng = ATT_HEAD_GROUP
    assert fk.shape == (b, n_heads, l // tq, tq) and n_heads % ng == 0
    return pl.pallas_call(
        functools.partial(_pattn_kernel, scale=hd ** -0.5),
        grid=(b, n_heads // ng, l // tq),
        in_specs=[pl.BlockSpec((1, tq, ng * hd), lambda bi, h, i: (bi, i, h)),
                  pl.BlockSpec((1, l, ng * hd), lambda bi, h, i: (bi, 0, h)),
                  pl.BlockSpec((1, l, ng * hd), lambda bi, h, i: (bi, 0, h)),
                  pl.BlockSpec((1, ng, tq, LANES), lambda bi, h, i: (bi, h, i, 0)),
                  pl.BlockSpec((1, ng, l // tq, tq), lambda bi, h, i: (bi, h, 0, 0))],
        out_specs=pl.BlockSpec((1, tq, ng * hd), lambda bi, h, i: (bi, i, h)),
        out_shape=jax.ShapeDtypeStruct((b, l, w), F32),
        scratch_shapes=[pltpu.VMEM((ng, tq, LANES), F32), pltpu.VMEM((ng, tq, LANES), F32), pltpu.VMEM((ng, tq, hd), F32),
                        pltpu.VMEM((ng, tq, LANES), F32)],
        compiler_params=_cparams("arbitrary", "arbitrary", "arbitrary"),
        name="prompt_attention",
    )(q, k, v, fq, fk)


def _page_sums_kernel(x_ref, m_after_ref, m_all_ref, dloc_ref, tot_ref):
    x1, x2, x3 = _split3(x_ref[...])
    ma = m_after_ref[...]
    mt = m_all_ref[...]
    dloc_ref[...] = _dot(x1, ma) + _dot(x2, ma) + _dot(x3, ma)
    tot_ref[...] = _dot(x1, mt) + _dot(x2, mt) + _dot(x3, mt)


def _page_sums(logf_pages, n_heads):
    n_pool, w = logf_pages.shape
    r = lax.broadcasted_iota(jnp.int32, (w, w), 0)
    c = lax.broadcasted_iota(jnp.int32, (w, w), 1)
    same_head = (r % n_heads) == (c % n_heads)
    m_after = (same_head & (r // n_heads > c // n_heads)).astype(BF16)
    m_all = same_head.astype(BF16)
    tp = 256
    return pl.pallas_call(
        _page_sums_kernel,
        grid=(n_pool // tp,),
        in_specs=[pl.BlockSpec((tp, w), lambda i: (i, 0)), _const_spec((w, w)), _const_spec((w, w))],
        out_specs=[pl.BlockSpec((tp, w), lambda i: (i, 0))] * 2,
        out_shape=[jax.ShapeDtypeStruct((n_pool, w), F32)] * 2,
        compiler_params=_cparams("arbitrary"),
        name="page_logf_sums",
    )(logf_pages, m_after, m_all)


def _lane_fold(x, op):
    out = x[:, 0:LANES]
    for c in range(1, x.shape[1] // LANES):
        out = op(out, x[:, c * LANES:(c + 1) * LANES])
    return out


def _sattn_kernel(pt_ref, q_ref, kn_ref, vn_ref, cqc_ref, cqr_ref, mb_ref, mbn_ref, ck_hbm, cv_hbm, dloc_hbm, tot_hbm,
                  o_ref, kbuf, vbuf, dbuf, tbuf, sbuf, sems, *, scale):
    b = pl.program_id(0)
    n_pages = kbuf.shape[1]
    slot = lax.rem(b, 2)

    def page_copies(bb, sl, i):
        p = pt_ref[bb, i]
        return (pltpu.make_async_copy(ck_hbm.at[p], kbuf.at[sl, i], sems.at[sl, 0]),
                pltpu.make_async_copy(cv_hbm.at[p], vbuf.at[sl, i], sems.at[sl, 1]),
                pltpu.make_async_copy(dloc_hbm.at[pl.ds(p, 1)], dbuf.at[sl, i], sems.at[sl, 2]),
                pltpu.make_async_copy(tot_hbm.at[pl.ds(p, 1)], tbuf.at[sl, i], sems.at[sl, 3]))

    def fetch(bb, sl):
        for i in range(n_pages):
            for cp in page_copies(bb, sl, i):
                cp.start()

    @pl.when(b == 0)
    def _():
        fetch(0, 0)

    @pl.when(b + 1 < pl.num_programs(0))
    def _():
        fetch(b + 1, 1 - slot)

    for i in range(n_pages):
        for cp in page_copies(b, slot, i):
            cp.wait()

    q = q_ref[0].astype(BF16)
    cq = cqc_ref[0][:, 0:1]
    base = mb_ref[...] + cq

    def scores(k, carry):
        after, mrun = carry
        i = n_pages - 1 - k
        s = _dot_nt(q, kbuf[slot, i].astype(BF16)) * scale + base + (dbuf[slot, i] + after)
        sbuf[i] = s
        return after + tbuf[slot, i], jnp.maximum(mrun, _lane_fold(s, jnp.maximum))

    rows = q.shape[0]
    init = (jnp.zeros(dbuf.shape[2:], F32), jnp.full((rows, LANES), NEG, F32))
    _, mrun = lax.fori_loop(0, n_pages, scores, init, unroll=True)
    sn = _dot_nt(q, kn_ref[0].astype(BF16)) * scale + (cq - cqr_ref[0]) + mbn_ref[...]
    m = jnp.maximum(jnp.max(mrun, axis=1, keepdims=True), jnp.max(sn, axis=1, keepdims=True))

    def values(i, carry):
        lrun, acc = carry
        p = jnp.exp(sbuf[i] - m)
        return lrun + _lane_fold(p, jnp.add), acc + _dot(p.astype(BF16), vbuf[slot, i].astype(BF16))

    init = (jnp.zeros((rows, LANES), F32), jnp.zeros((rows, q.shape[1]), F32))
    lrun, acc = lax.fori_loop(0, n_pages, values, init, unroll=True)
    pn = jnp.exp(sn - m)
    l = jnp.sum(lrun, axis=1, keepdims=True) + jnp.sum(pn, axis=1, keepdims=True)
    acc = acc + _dot(pn.astype(BF16), vn_ref[0].astype(BF16))
    o_ref[0] = acc / l


def _sample_attention(page_table, q, kn, vn, cache_k, cache_v, dloc, tot, cq_col, cq_row, n_heads):
    b, rows, hd = q.shape
    n_pages = page_table.shape[1]
    cols = cache_k.shape[1]
    r = lax.broadcasted_iota(jnp.int32, (rows, cols), 0)
    c = lax.broadcasted_iota(jnp.int32, (rows, cols), 1)
    mask_past = jnp.where((r % n_heads) == (c % n_heads), 0.0, NEG).astype(F32)
    rn = lax.broadcasted_iota(jnp.int32, (rows, rows), 0)
    cn = lax.broadcasted_iota(jnp.int32, (rows, rows), 1)
    mask_new = jnp.where(((rn % n_heads) == (cn % n_heads)) & (cn // n_heads <= rn // n_heads), 0.0, NEG).astype(F32)
    per_b = lambda bi, pt: (bi, 0, 0)
    const2 = lambda bi, pt: (0, 0)
    hbm = pl.BlockSpec(memory_space=pltpu.HBM)
    grid_spec = pltpu.PrefetchScalarGridSpec(
        num_scalar_prefetch=1,
        grid=(b,),
        in_specs=[pl.BlockSpec((1, rows, hd), per_b), pl.BlockSpec((1, rows, hd), per_b), pl.BlockSpec((1, rows, hd), per_b),
                  pl.BlockSpec((1, rows, LANES), per_b), pl.BlockSpec((1, 1, rows), per_b),
                  pl.BlockSpec((rows, cols), const2), pl.BlockSpec((rows, rows), const2),
                  hbm, hbm, hbm, hbm],
        out_specs=pl.BlockSpec((1, rows, hd), per_b),
        scratch_shapes=[pltpu.VMEM((2, n_pages, cols, hd), F32), pltpu.VMEM((2, n_pages, cols, hd), F32),
                        pltpu.VMEM((2, n_pages, 1, cols), F32), pltpu.VMEM((2, n_pages, 1, cols), F32),
                        pltpu.VMEM((n_pages, rows, cols), F32), pltpu.SemaphoreType.DMA((2, 4))],
    )
    return pl.pallas_call(
        functools.partial(_sattn_kernel, scale=hd ** -0.5),
        grid_spec=grid_spec,
        out_shape=jax.ShapeDtypeStruct((b, rows, hd), F32),
        compiler_params=_cparams("arbitrary"),
        name="sample_attention",
    )(page_table, q, kn, vn, cq_col, cq_row, mask_past, mask_new, cache_k, cache_v, dloc, tot)


def _s5_tables_kernel(are_ref, aim_ref, ldt_ref, pre_ref, pim_ref, cre_ref, cim_ref):
    a_re = jnp.minimum(are_ref[...], -1e-4)
    a_im = aim_ref[...]
    dt = jnp.exp(ldt_ref[...])
    mag = jnp.exp(a_re * dt)
    abr = mag * jnp.cos(a_im * dt)
    abi = mag * jnp.sin(a_im * dt)
    den = a_re * a_re + a_im * a_im
    nr, ni = abr - 1.0, abi
    cre_ref[...] = (nr * a_re + ni * a_im) / den
    cim_ref[...] = (ni * a_re - nr * a_im) / den
    pr, pi = abr, abi
    for r in range(SUBLANES):
        pre_ref[r:r + 1, :] = pr
        pim_ref[r:r + 1, :] = pi
        pr, pi = pr * abr - pi * abi, pr * abi + pi * abr


def _s5_tables(a_re, a_im, log_dt):
    g, p = a_re.shape
    n = g * p
    flat = lambda x: x.reshape(1, n)
    ldt = jnp.broadcast_to(log_dt[:, None], (g, p))
    row = jax.ShapeDtypeStruct((1, n), F32)
    tab = jax.ShapeDtypeStruct((SUBLANES, n), F32)
    return pl.pallas_call(
        _s5_tables_kernel,
        out_shape=[tab, tab, row, row],
        name="s5_tables",
    )(flat(a_re), flat(a_im), flat(ldt))


def _s5_bbar_kernel(cre_ref, cim_ref, bre_ref, bim_ref, ore_ref, oim_ref):
    cr, ci = cre_ref[...], cim_ref[...]
    br, bi = bre_ref[...], bim_ref[...]
    ore_ref[...] = cr * br - ci * bi
    oim_ref[...] = cr * bi + ci * br


def _s5_bbar(coef_re, coef_im, b_re, b_im):
    shp = jax.ShapeDtypeStruct(b_re.shape, F32)
    return pl.pallas_call(_s5_bbar_kernel, out_shape=[shp, shp], name="s5_bbar")(
        coef_re[..., None], coef_im[..., None], b_re, b_im)


def _lane_order(re, im):
    lead = re.shape[:-2]
    g, p = re.shape[-2:]
    nch = g // SSM_CHUNK_GROUPS
    cw = SSM_CHUNK_GROUPS * p
    both = jnp.stack([re.reshape(lead + (nch, cw)), im.reshape(lead + (nch, cw))], axis=-2)
    return both.reshape(lead + (2 * g * p,))


def _lane_unorder(x, g, p):
    lead = x.shape[:-1]
    nch = g // SSM_CHUNK_GROUPS
    both = x.reshape(lead + (nch, 2, SSM_CHUNK_GROUPS, p))
    return both[..., 0, :, :].reshape(lead + (g, p)), both[..., 1, :, :].reshape(lead + (g, p))


def _block_diag_b(bbar_re, bbar_im):
    g, p, c = bbar_re.shape
    k = SSM_CHUNK_GROUPS
    nch = g // k
    cols = 2 * k * p
    src = jnp.stack([bbar_re, bbar_im]).reshape(2, nch, k, p, c).transpose(1, 4, 0, 2, 3).reshape(nch, 1, c, cols)
    col_group = (lax.broadcasted_iota(jnp.int32, (k, 1, cols), 2) % (k * p)) // p
    keep = col_group == lax.broadcasted_iota(jnp.int32, (k, 1, cols), 0)
    return jnp.where(keep[None], src, 0.0).reshape(nch, k * c, cols)


def _block_diag_c(c_re, c_im):
    g, c, p = c_re.shape
    k = SSM_CHUNK_GROUPS
    nch = g // k
    rows = 2 * k * p
    src = jnp.stack([c_re, -c_im]).reshape(2, nch, k, c, p).transpose(1, 0, 2, 4, 3).reshape(nch, rows, c)
    row_group = (lax.broadcasted_iota(jnp.int32, (rows, k * c), 0) % (k * p)) // p
    keep = row_group == lax.broadcasted_iota(jnp.int32, (rows, k * c), 1) // c
    return jnp.where(keep[None], jnp.tile(src, (1, 1, k)), 0.0)


def _s5_scan_chunk(hbuf, pw_ref, ch, carry_in, carry_ref, state_ref):
    cw2 = 2 * SSM_CHUNK_GROUPS * SSM_STATE
    cw = cw2 // 2
    rows = lax.broadcasted_iota(jnp.int32, (SUBLANES, SCAN_LANES), 0)
    for sub in range(cw // SCAN_LANES):
        re0 = ch * cw2 + sub * SCAN_LANES
        sl_re = pl.ds(re0, SCAN_LANES)
        sl_im = pl.ds(re0 + cw, SCAN_LANES)
        pr = pw_ref[:, sl_re]
        pi = pw_ref[:, sl_im]
        steps = []
        for sh in (1, 2, 4):
            ar = jnp.where(rows >= sh, pr[sh - 1:sh, :], 0.0)
            ai = jnp.where(rows >= sh, pi[sh - 1:sh, :], 0.0)
            steps.append((sh, ar, ai))
        if carry_in is None:
            hr, hi = carry_ref[:, sl_re], carry_ref[:, sl_im]
        for t in range(hbuf.shape[0] // SUBLANES):
            tile = pl.ds(t * SUBLANES, SUBLANES)
            xr = hbuf[tile, sl_re]
            xi = hbuf[tile, sl_im]
            for sh, ar, ai in steps:
                sr = pltpu.roll(xr, sh, 0)
                si = pltpu.roll(xi, sh, 0)
                xr, xi = xr + ar * sr - ai * si, xi + ar * si + ai * sr
            if carry_in is not None:
                hr, hi = carry_in(t, sl_re), carry_in(t, sl_im)
            xr, xi = xr + pr * hr - pi * hi, xi + pr * hi + pi * hr
            hbuf[tile, sl_re] = xr
            hbuf[tile, sl_im] = xi
            hr, hi = xr[SUBLANES - 1:, :], xi[SUBLANES - 1:, :]
            if state_ref is not None:
                state_ref[t:t + 1, sl_re] = hr
                state_ref[t:t + 1, sl_im] = hi
        if carry_in is None:
            carry_ref[:, sl_re] = hr
            carry_ref[:, sl_im] = hi


def _s5_block(u_ref, bhi_ref, blo_ref, pw_ref, cc_ref, d_ref, wg_ref, bg_ref, o_ref, hbuf, carry_in, carry_ref, state_ref):
    u = u_ref[...]
    u_hi, u_lo = _split2(u)
    n_ch, kc, nc = bhi_ref.shape
    ys = []
    for c in range(n_ch + 2):
        if c < n_ch:
            uh = u_hi[:, c * kc:(c + 1) * kc]
            ul = u_lo[:, c * kc:(c + 1) * kc]
            hbuf[:, c * nc:(c + 1) * nc] = _dot(uh, bhi_ref[c]) + _dot(uh, blo_ref[c]) + _dot(ul, bhi_ref[c])
        if 1 <= c <= n_ch:
            _s5_scan_chunk(hbuf, pw_ref, c - 1, carry_in, carry_ref, state_ref)
        if c >= 2:
            ys.append(_dot(hbuf[:, (c - 2) * nc:(c - 1) * nc].astype(BF16), cc_ref[c - 2]))
    y = jnp.concatenate(ys, axis=1) + d_ref[...] * u
    z = jax.nn.gelu(y)
    o_ref[...] = z * jax.nn.sigmoid(_dot(z.astype(BF16), wg_ref[...]) + bg_ref[...])


def _s5_prompt_kernel(u_ref, bhi_ref, blo_ref, pw_ref, cc_ref, d_ref, wg_ref, bg_ref, o_ref, st_ref, hbuf, carry_ref):
    ti = pl.program_id(1)

    @pl.when(ti == 0)
    def _():
        carry_ref[...] = jnp.zeros_like(carry_ref)

    _s5_block(u_ref, bhi_ref, blo_ref, pw_ref, cc_ref, d_ref, wg_ref, bg_ref, o_ref, hbuf, None, carry_ref, None)

    @pl.when(ti == pl.num_programs(1) - 1)
    def _():
        st_ref[0] = carry_ref[...]


def _s5_sample_kernel(u_ref, h0_ref, bhi_ref, blo_ref, pw_ref, cc_ref, d_ref, wg_ref, bg_ref, o_ref, st_ref, hbuf):
    _s5_block(u_ref, bhi_ref, blo_ref, pw_ref, cc_ref, d_ref, wg_ref, bg_ref, o_ref, hbuf,
              lambda t, lanes: h0_ref[t:t + 1, lanes], None, st_ref)


def _s5_weight_specs(bhi, pw, cc, width):
    return [_const_spec(bhi.shape), _const_spec(bhi.shape), _const_spec(pw.shape), _const_spec(cc.shape),
            _const_spec((1, width)), _const_spec((width, width)), _const_spec((1, width))]


def _s5_prompt(u, n_batch, bhi, blo, pw, cc, d_skip, w_glu, b_glu):
    t, width = u.shape
    per = t // n_batch // TOK_BLOCK
    two_n = pw.shape[1]
    tok = pl.BlockSpec((TOK_BLOCK, width), lambda b, i: (b * per + i, 0))
    return pl.pallas_call(
        _s5_prompt_kernel,
        grid=(n_batch, per),
        in_specs=[tok] + _s5_weight_specs(bhi, pw, cc, width),
        out_specs=[tok, pl.BlockSpec((1, 1, two_n), lambda b, i: (b, 0, 0))],
        out_shape=[jax.ShapeDtypeStruct((t, width), F32), jax.ShapeDtypeStruct((n_batch, 1, two_n), F32)],
        scratch_shapes=[pltpu.VMEM((TOK_BLOCK, two_n), F32), pltpu.VMEM((1, two_n), F32)],
        compiler_params=_cparams("arbitrary", "arbitrary"),
        name="s5_prompt",
    )(u, bhi, blo, pw, cc, d_skip, w_glu, b_glu)


def _s5_sample(u, h0, bhi, blo, pw, cc, d_skip, w_glu, b_glu):
    t, width = u.shape
    two_n = pw.shape[1]
    tiles = TOK_BLOCK // SUBLANES
    tok = pl.BlockSpec((TOK_BLOCK, width), lambda i: (i, 0))
    st = pl.BlockSpec((tiles, two_n), lambda i: (i, 0))
    return pl.pallas_call(
        _s5_sample_kernel,
        grid=(t // TOK_BLOCK,),
        in_specs=[tok, st] + _s5_weight_specs(bhi, pw, cc, width),
        out_specs=[tok, st],
        out_shape=[jax.ShapeDtypeStruct((t, width), F32), jax.ShapeDtypeStruct(h0.shape, F32)],
        scratch_shapes=[pltpu.VMEM((TOK_BLOCK, two_n), F32)],
        compiler_params=_cparams("arbitrary"),
        name="s5_sample",
    )(u, h0, bhi, blo, pw, cc, d_skip, w_glu, b_glu)


def _outproj_kernel(att_ref, ssm_ref, ga_ref, gs_ref, wa_ref, ws_ref, x_ref, gate_ref, o_ref):
    def norm(v, g):
        return (v * lax.rsqrt(jnp.mean(v * v, axis=-1, keepdims=True) + EPS) * g).astype(BF16)

    merged = _dot(norm(att_ref[...], ga_ref[...]), wa_ref[...]) + _dot(norm(ssm_ref[...], gs_ref[...]), ws_ref[...])
    tg, _, d = x_ref.shape
    o_ref[...] = x_ref[...] + gate_ref[...] * merged.reshape(tg, SUBLANES, d)


def _out_projection(att, ssm, g_att, g_ssm, w_att, w_ssm, xg, gate):
    ng, _, d = xg.shape
    tg = TOK_BLOCK // SUBLANES
    steps = ng // tg
    wa = att.shape[1]
    ws = ssm.shape[1]
    return pl.pallas_call(
        _outproj_kernel,
        grid=(steps,),
        in_specs=[pl.BlockSpec((TOK_BLOCK, wa), lambda i: (i, 0)), pl.BlockSpec((TOK_BLOCK, ws), lambda i: (i, 0)),
                  _const_spec((1, wa)), _const_spec((1, ws)), _const_spec((wa, d)), _const_spec((ws, d)),
                  pl.BlockSpec((tg, SUBLANES, d), lambda i: (i, 0, 0)), _mod_spec(gate.shape[0], steps, tg, d)],
        out_specs=pl.BlockSpec((tg, SUBLANES, d), lambda i: (i, 0, 0)),
        out_shape=jax.ShapeDtypeStruct(xg.shape, F32),
        compiler_params=_cparams("arbitrary"),
        name="out_projection",
    )(att, ssm, g_att, g_ssm, w_att, w_ssm, xg, gate)


def _topk_rows(problems, k, row=None):
    scores = [p[0] for p in problems]
    if row is None:
        row = lax.broadcasted_iota(jnp.int32, scores[0].shape, 0).astype(F32)
    for i in range(k):
        for z, (_, val_ref, idx_ref) in enumerate(problems):
            s = scores[z]
            m = jnp.max(s, axis=0, keepdims=True)
            j = jnp.min(jnp.where(s == m, row, ROW_SENTINEL), axis=0, keepdims=True)
            val_ref[i:i + 1, :] = m
            idx_ref[i:i + 1, :] = j
            scores[z] = jnp.where(row == j, -jnp.inf, s)


def _pair_candidates():
    k = PEER_TOPK
    runs = [(0, 0, 1, 0, k), (16, 1, 1, 0, 8), (24, 2, 1, 0, 8), (32, 3, 1, 0, 8),
            (40, 8, 8, 0, 1), (48, 4, 4, 0, 1), (52, 4, 4, 1, 1), (56, 4, 4, 2, 1)]
    covered = set()
    for _, a0, na, b0, nb in runs:
        for a in range(a0, a0 + na):
            for b in range(b0, b0 + nb):
                assert (a, b) not in covered
                covered.add((a, b))
    assert all((a, b) in covered for a in range(k) for b in range(k) if (a + 1) * (b + 1) <= k)
    return runs


def _pair_flat_index():
    k = PEER_TOPK
    r = lax.broadcasted_iota(jnp.int32, (PAIR_ROWS, LANES), 0)
    out = (ROW_SENTINEL / 2 + r).astype(F32)
    for row0, a0, na, b0, nb in _pair_candidates():
        i = r - row0
        flat = (a0 * k + b0 + i) if na == 1 else ((a0 + i) * k + b0)
        out = jnp.where((i >= 0) & (i < na * nb), flat.astype(F32), out)
    return out


def _take_rows(table_ref, idx, n):
    out = jnp.zeros_like(idx)
    for r in range(n):
        out = jnp.where(idx == float(r), table_ref[r:r + 1, :], out)
    return out


def _peer_select_kernel(x_ref, shift_ref, scale_ref, xn_ref, shiftn_ref, scalen_ref, g_ref, wqt_ref, sk_ref,
                        h_ref, e_ref, gw_ref, qt_sc, qn_sc, hc_sc, hn_sc, val_sc, idx_sc, cs_sc, top_sc, pick_sc):
    tg, _, d = x_ref.shape
    n_sides, n_keys, half = sk_ref.shape
    n_groups = n_sides // 2 // PEER_HEAD_GROUP
    q_rows = qt_sc.shape[0] // n_groups

    def normed(x, shift, scale):
        return _rms_mod(x[...], g_ref[...], shift[...], scale[...]).reshape(tg * SUBLANES, d).astype(BF16)

    @pl.when(pl.program_id(0) == 0)
    def _():
        h0 = normed(x_ref, shift_ref, scale_ref)
        hc_sc[...] = h0
        qt_sc[...] = _dot_nt(wqt_ref[...], h0).astype(BF16)

    h_ref[...] = hc_sc[...]
    hn_sc[...] = normed(xn_ref, shiftn_ref, scalen_ref)
    k = PEER_TOPK
    n_chunks = tg * SUBLANES // LANES
    flat = _pair_flat_index()
    runs = _pair_candidates()
    n_used = max(row0 + na * nb for row0, _, na, _, nb in runs)
    units = [(hg, c) for hg in range(PEER_HEAD_GROUP) for c in range(n_chunks)]
    for z in range(len(units)):
        cs_sc[z, n_used:, :] = jnp.full((PAIR_ROWS - n_used, LANES), -jnp.inf, F32)

    def group_body(group, carry):
        first = []
        for z, (hg, c) in enumerate(units):
            for side in range(2):
                hx = 2 * (group * PEER_HEAD_GROUP + hg) + side
                r0 = pl.multiple_of(hx * half, half)
                s = _dot(sk_ref[hx], qt_sc[pl.ds(r0, half), pl.ds(c * LANES, LANES)])
                first.append((s, val_sc.at[z, side], idx_sc.at[z, side]))
        _topk_rows(first, k)
        second = []
        for z in range(len(units)):
            for row0, a0, na, b0, nb in runs:
                cs_sc[z, row0:row0 + na * nb, :] = val_sc[z, 0, a0:a0 + na, :] + val_sc[z, 1, b0:b0 + nb, :]
            second.append((cs_sc[z], top_sc.at[z], pick_sc.at[z]))
        _topk_rows(second, k, flat)
        for z, (hg, c) in enumerate(units):
            r_out = pl.multiple_of((group * PEER_HEAD_GROUP + hg) * k, k)
            cols = pl.ds(c * LANES, LANES)
            ts = top_sc[z]
            p = jnp.exp(ts - ts[0:1, :])
            gw_ref[pl.ds(r_out, k), cols] = p / jnp.sum(p, axis=0, keepdims=True)
            pick = pick_sc[z]
            a = jnp.floor(pick * (1.0 / k))
            first_key = _take_rows(idx_sc.at[z, 0], a, k)
            second_key = _take_rows(idx_sc.at[z, 1], pick - a * float(k), k)
            e_ref[pl.ds(r_out, k), cols] = (first_key * float(n_keys) + second_key).astype(jnp.int32)
        q_slab = pl.ds(pl.multiple_of(group * q_rows, q_rows), q_rows)
        qn_sc[q_slab, :] = _dot_nt(wqt_ref[q_slab, :], hn_sc[...]).astype(BF16)
        return carry

    lax.fori_loop(0, n_groups, group_body, 0)
    qt_sc[...] = qn_sc[...]
    hc_sc[...] = hn_sc[...]


def _peer_select(xg, shift, scale, g, w_q_t, subkeys):
    ng, _, d = xg.shape
    t = ng * SUBLANES
    tg = TOK_BLOCK // SUBLANES
    steps = ng // tg
    n_sides = subkeys.shape[0]
    k = PEER_TOPK
    rows = (n_sides // 2) * k
    nch = PEER_HEAD_GROUP * TOK_BLOCK // LANES
    assert (n_sides // 2) % PEER_HEAD_GROUP == 0
    mod_spec = _mod_spec(shift.shape[0], steps, tg, d)
    following = lambda i: jnp.minimum(i + 1, steps - 1)
    mod_next = _mod_spec(shift.shape[0], steps, tg, d, following)
    x_spec = pl.BlockSpec((tg, SUBLANES, d), lambda i: (i, 0, 0))
    x_next = pl.BlockSpec((tg, SUBLANES, d), lambda i: (following(i), 0, 0))
    q_tile = pltpu.VMEM((w_q_t.shape[0], TOK_BLOCK), BF16)
    h_tile = pltpu.VMEM((TOK_BLOCK, d), BF16)
    return pl.pallas_call(
        _peer_select_kernel,
        grid=(steps,),
        in_specs=[x_spec, mod_spec, mod_spec, x_next, mod_next, mod_next, _const_spec((1, d)),
                  _const_spec(w_q_t.shape), _const_spec(subkeys.shape)],
        out_specs=[pl.BlockSpec((TOK_BLOCK, d), lambda i: (i, 0)), pl.BlockSpec((rows, TOK_BLOCK), lambda i: (0, i)),
                   pl.BlockSpec((rows, TOK_BLOCK), lambda i: (0, i))],
        out_shape=[jax.ShapeDtypeStruct((t, d), BF16), jax.ShapeDtypeStruct((rows, t), jnp.int32),
                   jax.ShapeDtypeStruct((rows, t), F32)],
        scratch_shapes=[q_tile, q_tile, h_tile, h_tile,
                        pltpu.VMEM((nch, 2, k, LANES), F32), pltpu.VMEM((nch, 2, k, LANES), F32),
                        pltpu.VMEM((nch, PAIR_ROWS, LANES), F32),
                        pltpu.VMEM((nch, k, LANES), F32), pltpu.VMEM((nch, k, LANES), F32)],
        compiler_params=_cparams("arbitrary"),
        name="peer_select",
    )(xg, shift, scale, xg, shift, scale, g, w_q_t, subkeys)


def _peer_mask_kernel(e_ref, g_ref, o_ref):
    n_tiles = o_ref.shape[0]
    n2 = o_ref.shape[2]
    n1 = n_tiles * SUBLANES
    shift = n2.bit_length() - 1
    as_bf16 = lambda i: i.astype(F32).astype(BF16)
    sub1 = as_bf16(lax.broadcasted_iota(jnp.int32, (n1, e_ref.shape[1]), 0))
    sub2 = as_bf16(lax.broadcasted_iota(jnp.int32, (n2, e_ref.shape[1]), 0))
    zero = jnp.zeros((), BF16)

    def body(t, _):
        e = e_ref[pl.ds(t, 1), :]
        g_hi, g_lo = _split2(g_ref[pl.ds(t, 1), :])
        hit1 = sub1 == as_bf16(jnp.right_shift(e, shift))
        hit2 = sub2 == as_bf16(jnp.bitwise_and(e, n2 - 1))
        r = jnp.where(hit2, jnp.ones((), BF16), zero)
        lhs = jnp.concatenate([jnp.where(hit1, g_hi, zero), jnp.where(hit1, g_lo, zero)], axis=1)
        m = _dot_nt(lhs, jnp.concatenate([r, r], axis=1))
        r0 = pl.multiple_of(t * SUBLANES, SUBLANES)
        for jb in range(n_tiles):
            o_ref[jb, pl.ds(r0, SUBLANES), :] = m[jb * SUBLANES:(jb + 1) * SUBLANES, :]
        return 0

    lax.fori_loop(0, e_ref.shape[0], body, 0, unroll=32)


def _peer_mask(e_tok, g_tok):
    t, picks = e_tok.shape
    n = PEER_N_KEYS
    assert n & (n - 1) == 0 and n <= 256
    tb = MASK_TOK_BLOCK
    return pl.pallas_call(
        _peer_mask_kernel,
        grid=(t // tb,),
        in_specs=[pl.BlockSpec((tb, picks), lambda i: (i, 0)), pl.BlockSpec((tb, picks), lambda i: (i, 0))],
        out_specs=pl.BlockSpec((n // SUBLANES, tb * SUBLANES, n), lambda i: (0, i, 0)),
        out_shape=jax.ShapeDtypeStruct((n // SUBLANES, t * SUBLANES, n), F32),
        compiler_params=_cparams("arbitrary"),
        name="peer_mask",
    )(e_tok, g_tok)


def _peer_ffn_kernel(h_ref, u_ref, v_ref, m_ref, x_ref, gate_ref, gf_ref, o_ref, acc_ref):
    j = pl.program_id(1)

    @pl.when(j == 0)
    def _():
        acc_ref[...] = jnp.zeros_like(acc_ref)

    tm = h_ref.shape[0]
    n2 = m_ref.shape[2]
    a = _dot_nt(h_ref[...], u_ref[...])
    m2 = m_ref.at[0]
    parts = []
    for i1 in range(SUBLANES):
        gate = m2[pl.ds(i1, tm, stride=SUBLANES), :]
        parts.append((gate * jax.nn.gelu(a[:, i1 * n2:(i1 + 1) * n2])).astype(BF16))
    acc_ref[...] += _dot(jnp.concatenate(parts, axis=1), v_ref[...])

    @pl.when(j == pl.num_programs(1) - 1)
    def _():
        tg, _, d = x_ref.shape
        x2 = x_ref[...] + gate_ref[...] * acc_ref[...].reshape(tg, SUBLANES, d)
        ms = jnp.mean(x2 * x2, axis=-1, keepdims=True)
        o_ref[...] = x2 * lax.rsqrt(ms + EPS) * gf_ref[...]


def _peer_ffn(h, exp_u, exp_v, mask, xg, gate, g_final):
    t, d = h.shape
    ne = exp_u.shape[0]
    n = PEER_N_KEYS
    tm = PEER_TOK_BLOCK
    te = PEER_EXPERT_TILE
    assert te == SUBLANES * n and mask.shape == (ne // te, t * SUBLANES, n)
    tg = tm // SUBLANES
    steps = t // tm
    return pl.pallas_call(
        _peer_ffn_kernel,
        grid=(steps, ne // te),
        in_specs=[pl.BlockSpec((tm, d), lambda i, j: (i, 0)),
                  pl.BlockSpec((te, d), lambda i, j: (j, 0)), pl.BlockSpec((te, d), lambda i, j: (j, 0)),
                  pl.BlockSpec((1, tm * SUBLANES, n), lambda i, j: (j, i, 0)),
                  pl.BlockSpec((tg, SUBLANES, d), lambda i, j: (i, 0, 0)),
                  _mod_spec(gate.shape[0], steps, tg, d), _const_spec((1, d))],
        out_specs=pl.BlockSpec((tg, SUBLANES, d), lambda i, j: (i, 0, 0)),
        out_shape=jax.ShapeDtypeStruct(xg.shape, F32),
        scratch_shapes=[pltpu.VMEM((tm, d), F32)],
        compiler_params=_cparams("arbitrary", "arbitrary"),
        name="peer_ffn",
    )(h, exp_u, exp_v, mask, xg, gate, g_final)


def _token_stages_pre(xg, mods, lw, emit_bf16):
    shift1, scale1 = mods[0], mods[1]
    return _in_projection(xg, shift1, scale1, lw["g_norm1"], lw["w_qkv"], lw["w_u"], lw["w_fg"], lw["bf_pad"], emit_bf16)


def _token_stages_post(xg, mods, att, ssm, lw, g_final):
    gate1, shift2, scale2, gate2 = mods[2], mods[3], mods[4], mods[5]
    x1 = _out_projection(att, ssm, lw["g_attn_out"], lw["g_ssm_out"], lw["w_out_att"], lw["w_out_ssm"], xg, gate1)
    h2, e_t, g_t = _peer_select(x1, shift2, scale2, lw["g_norm2"], lw["peer_w_q_t"], lw["peer_subkeys"])
    mask = _peer_mask(e_t.T, g_t.T)
    return _peer_ffn(h2, lw["peer_u"], lw["peer_v"], mask, x1, gate2, g_final)


def kernel(x_prompt, x_sample, cache_k, cache_v, cache_logf, state_ssm_re, state_ssm_im, page_table, c_prompt, c_sample, w_ada, b_ada, g_norm1, g_norm2, w_in, b_forget, ssm_a_re, ssm_a_im, ssm_log_dt, ssm_b_re, ssm_b_im, ssm_c_re, ssm_c_im, ssm_d, w_glu, b_glu, g_attn_out, g_ssm_out, w_out, peer_w_q, peer_subkeys, peer_u, peer_v, g_final):
    depth = w_ada.shape[0]
    assert depth == 1, "single trunk layer"
    nb, seq, d = x_prompt.shape
    nd, dseq, _ = x_sample.shape
    assert dseq == SUBLANES
    n_pool, page, n_heads, hd = cache_k.shape[1:]
    fox_w = n_heads * hd
    n_groups, n_state = ssm_a_re.shape[1:]
    ssm_w = n_groups * SSM_CH_PER_GROUP
    assert fox_w == ssm_w
    n_pages = page_table.shape[1]
    layer = 0

    w_in_l = w_in[layer]
    w_fg = jnp.pad(w_in_l[:, 3 * fox_w:3 * fox_w + n_heads], ((0, 0), (0, LANES - n_heads))).astype(BF16)
    bf_pad = jnp.concatenate([b_forget[layer], jnp.zeros((LANES - n_heads,), F32)]).reshape(1, LANES)
    w_out_l = w_out[layer].astype(BF16)
    lw = dict(
        w_qkv=w_in_l[:, :3 * fox_w].astype(BF16), w_u=w_in_l[:, 3 * fox_w + n_heads:].astype(BF16), w_fg=w_fg, bf_pad=bf_pad,
        g_norm1=g_norm1[layer].reshape(1, d), g_norm2=g_norm2[layer].reshape(1, d),
        g_attn_out=g_attn_out[layer].reshape(1, fox_w), g_ssm_out=g_ssm_out[layer].reshape(1, ssm_w),
        w_out_att=w_out_l[:fox_w], w_out_ssm=w_out_l[fox_w:],
        peer_w_q_t=peer_w_q[layer].T.astype(BF16),
        peer_subkeys=peer_subkeys[layer].reshape((-1,) + peer_subkeys.shape[-2:]).astype(BF16),
        peer_u=peer_u[layer].astype(BF16), peer_v=peer_v[layer].astype(BF16),
    )
    gf = g_final.reshape(1, d)

    n_c = nb + nd
    n_c_pad = -(-n_c // SUBLANES) * SUBLANES
    c_all = jnp.concatenate([c_prompt, c_sample, jnp.zeros((n_c_pad - n_c, d), F32)], axis=0)
    mod = _ada_modulation(c_all, w_ada[layer], b_ada[layer])
    mods_p = [mod[:nb, i * d:(i + 1) * d].reshape(nb, 1, d) for i in range(N_MOD)]
    mods_s = [mod[nb:n_c, i * d:(i + 1) * d].reshape(nd, 1, d) for i in range(N_MOD)]

    pw_re, pw_im, coef_re, coef_im = _s5_tables(ssm_a_re[layer], ssm_a_im[layer], ssm_log_dt[layer])
    bbar_re, bbar_im = _s5_bbar(coef_re.reshape(n_groups, n_state), coef_im.reshape(n_groups, n_state),
                                ssm_b_re[layer], ssm_b_im[layer])
    b_blk = _block_diag_b(bbar_re, bbar_im)
    b_hi = b_blk.astype(BF16)
    b_lo = (b_blk - b_hi.astype(F32)).astype(BF16)
    c_blk = _block_diag_c(ssm_c_re[layer], ssm_c_im[layer]).astype(BF16)
    pw = _lane_order(pw_re.reshape(SUBLANES, n_groups, n_state), pw_im.reshape(SUBLANES, n_groups, n_state))
    s5w = (b_hi, b_lo, pw, c_blk, ssm_d[layer].reshape(1, ssm_w), w_glu[layer].astype(BF16), b_glu[layer].reshape(1, ssm_w))

    xg_p = x_prompt.reshape(nb * seq // SUBLANES, SUBLANES, d)
    q_p, k_p, v_p, u_p, lf_p, qb_p, kb_p, vb_p = _token_stages_pre(xg_p, mods_p, lw, True)
    fcum = _cumsum_time(lf_p, nb).reshape(nb, seq, LANES)[:, :, :n_heads].transpose(0, 2, 1)
    fq = jnp.broadcast_to(fcum[..., None], (nb, n_heads, seq, LANES))
    fk = fcum.reshape(nb, n_heads, seq // ATT_BLOCK, ATT_BLOCK)
    as3 = lambda a: a.reshape(nb, seq, fox_w)
    att_p = _prompt_attention(as3(qb_p), as3(kb_p), as3(vb_p), fq, fk, n_heads).reshape(nb * seq, fox_w)
    ssm_p, st_p = _s5_prompt(u_p, nb, *s5w)
    y_p = _token_stages_post(xg_p, mods_p, att_p, ssm_p, lw, gf)
    re_p, im_p = _lane_unorder(st_p.reshape(nb, -1), n_groups, n_state)

    xg_s = x_sample.reshape(nd, SUBLANES, d)
    q_s, k_s, v_s, u_s, lf_s = _token_stages_pre(xg_s, mods_s, lw, False)
    rows = dseq * n_heads
    heads_rows = lambda a: a.reshape(nd, dseq, n_heads, hd).reshape(nd, rows, hd)
    logf_pages = cache_logf[layer].reshape(n_pool, page * n_heads)
    dloc, tot = _page_sums(logf_pages, n_heads)
    c_new = _cumsum_short(lf_s.reshape(nd, dseq, LANES))[:, :, :n_heads].reshape(nd, rows)
    cq_col = jnp.broadcast_to(c_new[:, :, None], (nd, rows, LANES))
    cq_row = c_new.reshape(nd, 1, rows)
    att_s = _sample_attention(page_table, heads_rows(q_s), heads_rows(k_s), heads_rows(v_s),
                              cache_k[layer].reshape(n_pool, page * n_heads, hd),
                              cache_v[layer].reshape(n_pool, page * n_heads, hd),
                              dloc, tot, cq_col, cq_row, n_heads)
    att_s = att_s.reshape(nd, dseq, n_heads, hd).reshape(nd * dseq, fox_w)
    h0 = _lane_order(state_ssm_re[layer], state_ssm_im[layer])
    ssm_s, st_s = _s5_sample(u_s, h0, *s5w)
    y_s = _token_stages_post(xg_s, mods_s, att_s, ssm_s, lw, gf)
    re_s, im_s = _lane_unorder(st_s, n_groups, n_state)

    kv5 = lambda a, b_, l_: a.reshape(1, b_, l_, n_heads, hd)
    return (y_p.reshape(nb, seq, d), y_s.reshape(nd, dseq, d),
            kv5(k_p, nb, seq), kv5(v_p, nb, seq), lf_p[:, :n_heads].reshape(1, nb, seq, n_heads),
            re_p[None], im_p[None],
            kv5(k_s, nd, dseq), kv5(v_s, nd, dseq), lf_s[:, :n_heads].reshape(1, nd, dseq, n_heads),
            re_s[None], im_s[None])
```

```python
import functools
import math

import jax
import jax.numpy as jnp
from jax import lax
from jax.experimental import pallas as pl
from jax.experimental.pallas import tpu as pltpu

F32 = jnp.float32
BF16 = jnp.bfloat16
EPS = 1e-6
NEG = -1e30

SUBLANES = 8
LANES = 128
VMEM_LIMIT_BYTES = 56 * 1024 * 1024

N_MOD = 6
FOX_HEAD_DIM = 128
SSM_CH_PER_GROUP = 16
SSM_STATE = 64
SSM_CHUNK_GROUPS = 16
PEER_N_KEYS = 128
PEER_TOPK = 16
PEER_EXPERT_TILE = 1024
PEER_HEAD_GROUP = 2
PAIR_ROWS = 64
ROW_SENTINEL = 1024.0

TOK_BLOCK = 256
ATT_BLOCK = 512
ATT_HEAD_GROUP = 4
LOG2E = 1.4426950408889634
PEER_TOK_BLOCK = 512
MASK_TOK_BLOCK = 128
SCAN_LANES = 512


def _cparams(*sem):
    return pltpu.CompilerParams(dimension_semantics=sem, vmem_limit_bytes=VMEM_LIMIT_BYTES)


def _const_spec(shape):
    nd = len(shape)
    return pl.BlockSpec(shape, lambda *_: (0,) * nd)


def _split2(x):
    hi = x.astype(BF16)
    lo = (x - hi.astype(F32)).astype(BF16)
    return hi, lo


def _split3(x):
    x1 = x.astype(BF16)
    r1 = x - x1.astype(F32)
    x2 = r1.astype(BF16)
    x3 = (r1 - x2.astype(F32)).astype(BF16)
    return x1, x2, x3


def _dot(a, b):
    return jnp.dot(a, b, preferred_element_type=F32)


def _dot_nt(a, b):
    return lax.dot_general(a, b, (((1,), (1,)), ((), ())), preferred_element_type=F32)


def _rms_mod(xg, g, shift, scale):
    ms = jnp.mean(xg * xg, axis=-1, keepdims=True)
    y = xg * lax.rsqrt(ms + EPS) * g
    return y * (1.0 + scale) + shift


def _mod_spec(n_mod_rows, n_steps, tg, d, block_of=lambda i: i):
    if n_mod_rows == n_steps * tg:
        return pl.BlockSpec((tg, 1, d), lambda i, *_: (block_of(i), 0, 0))
    assert n_steps % n_mod_rows == 0
    per = n_steps // n_mod_rows
    return pl.BlockSpec((1, 1, d), lambda i, *_: (block_of(i) // per, 0, 0))


def _ada_kernel(c_ref, w_ref, b_ref, o_ref):
    a = jax.nn.silu(c_ref[...]).astype(BF16)
    o_ref[...] = _dot(a, w_ref[...].astype(BF16)) + b_ref[...]


def _ada_modulation(c_all, w_ada, b_ada):
    m, d = c_all.shape
    n = w_ada.shape[1]
    tn = 1024
    return pl.pallas_call(
        _ada_kernel,
        grid=(n // tn,),
        in_specs=[_const_spec((m, d)), pl.BlockSpec((d, tn), lambda j: (0, j)), pl.BlockSpec((1, tn), lambda j: (0, j))],
        out_specs=pl.BlockSpec((m, tn), lambda j: (0, j)),
        out_shape=jax.ShapeDtypeStruct((m, n), F32),
        compiler_params=_cparams("arbitrary"),
        name="ada_modulation",
    )(c_all, w_ada, b_ada.reshape(1, n))


def _inproj_kernel(x_ref, shift_ref, scale_ref, g_ref, wqkv_ref, wu_ref, wfg_ref, bf_ref,
                   q_ref, k_ref, v_ref, u_ref, lf_ref, *bf16_refs):
    tg, _, d = x_ref.shape
    h = _rms_mod(x_ref[...], g_ref[...], shift_ref[...], scale_ref[...]).reshape(tg * SUBLANES, d).astype(BF16)
    r = _dot(h, wqkv_ref[...])
    w = q_ref.shape[1]
    q_ref[...] = r[:, 0:w]
    k_ref[...] = r[:, w:2 * w]
    v_ref[...] = r[:, 2 * w:3 * w]
    u_ref[...] = _dot(h, wu_ref[...])
    lf_ref[...] = jax.nn.log_sigmoid(_dot(h, wfg_ref[...]) + bf_ref[...])
    for i, ref in enumerate(bf16_refs):
        ref[...] = r[:, i * w:(i + 1) * w].astype(BF16)


def _in_projection(xg, shift, scale, g, w_all, w_u, w_fg, bf_pad, emit_bf16):
    ng, _, d = xg.shape
    t = ng * SUBLANES
    tg = TOK_BLOCK // SUBLANES
    steps = ng // tg
    width = w_u.shape[1]
    mod_spec = _mod_spec(shift.shape[0], steps, tg, d)
    tok = lambda w: pl.BlockSpec((TOK_BLOCK, w), lambda i: (i, 0))
    n_bf16 = 3 if emit_bf16 else 0
    return pl.pallas_call(
        _inproj_kernel,
        grid=(steps,),
        in_specs=[pl.BlockSpec((tg, SUBLANES, d), lambda i: (i, 0, 0)), mod_spec, mod_spec,
                  _const_spec((1, d)), _const_spec((d, 3 * width)), _const_spec(w_u.shape), _const_spec(w_fg.shape),
                  _const_spec((1, LANES))],
        out_specs=[tok(width)] * 4 + [tok(LANES)] + [tok(width)] * n_bf16,
        out_shape=([jax.ShapeDtypeStruct((t, width), F32)] * 4 + [jax.ShapeDtypeStruct((t, LANES), F32)]
                   + [jax.ShapeDtypeStruct((t, width), BF16)] * n_bf16),
        compiler_params=_cparams("arbitrary"),
        name="in_projection",
    )(xg, shift, scale, g, w_all, w_u, w_fg, bf_pad)


def _cumsum_kernel(x_ref, tri_ref, o_ref, carry_ref):
    @pl.when(pl.program_id(1) == 0)
    def _():
        carry_ref[...] = jnp.zeros_like(carry_ref)

    x1, x2, x3 = _split3(x_ref[...])
    tri = tri_ref[...]
    loc = _dot(tri, x1) + _dot(tri, x2) + _dot(tri, x3) + carry_ref[...]
    o_ref[...] = loc
    carry_ref[...] = loc[-1:, :]


def _cumsum_time(x, n_batch):
    t, w = x.shape
    l = t // n_batch
    tb = 256
    tri = (lax.broadcasted_iota(jnp.int32, (tb, tb), 1) <= lax.broadcasted_iota(jnp.int32, (tb, tb), 0)).astype(BF16)
    per = l // tb
    return pl.pallas_call(
        _cumsum_kernel,
        grid=(n_batch, per),
        in_specs=[pl.BlockSpec((tb, w), lambda b, i: (b * per + i, 0)), _const_spec((tb, tb))],
        out_specs=pl.BlockSpec((tb, w), lambda b, i: (b * per + i, 0)),
        out_shape=jax.ShapeDtypeStruct((t, w), F32),
        scratch_shapes=[pltpu.VMEM((1, w), F32)],
        compiler_params=_cparams("arbitrary", "arbitrary"),
        name="logf_cumsum",
    )(x, tri)


def _cumsum_short_kernel(x_ref, o_ref):
    run = x_ref[:, 0:1, :]
    o_ref[:, 0:1, :] = run
    for r in range(1, x_ref.shape[1]):
        run = run + x_ref[:, r:r + 1, :]
        o_ref[:, r:r + 1, :] = run


def _cumsum_short(x):
    return pl.pallas_call(_cumsum_short_kernel, out_shape=jax.ShapeDtypeStruct(x.shape, F32), name="logf_cumsum_new")(x)


def _pattn_kernel(q_ref, k_ref, v_ref, fq_ref, fk_ref, o_ref, m_sc, l_sc, acc_sc, fq_sc, *, scale):
    qi = pl.program_id(2)
    tq = q_ref.shape[1]
    n_group, _, hd = acc_sc.shape
    m_sc[...] = jnp.full_like(m_sc, NEG)
    l_sc[...] = jnp.zeros_like(l_sc)
    acc_sc[...] = jnp.zeros_like(acc_sc)
    fq_sc[...] = fq_ref[0] * LOG2E
    c = scale * LOG2E

    def block(kj, diagonal):
        start = pl.multiple_of(kj * tq, tq)
        for g in range(n_group):
            hs = slice(g * hd, (g + 1) * hd)
            s = _dot_nt(q_ref[0, :, hs], k_ref[0, pl.ds(start, tq), hs])
            fk2 = fk_ref[0, g, pl.ds(kj, 1), :] * LOG2E
            tiles = []
            for t in range(tq // LANES):
                ls = slice(t * LANES, (t + 1) * LANES)
                st = s[:, ls] * c - fk2[:, ls]
                if diagonal:
                    row = lax.broadcasted_iota(jnp.int32, st.shape, 0)
                    col = lax.broadcasted_iota(jnp.int32, st.shape, 1) + t * LANES
                    st = jnp.where(col <= row, st, NEG)
                tiles.append(st)
            m_prev = m_sc[g]
            fq2 = fq_sc[g]
            m_new = jnp.maximum(m_prev, jnp.max(functools.reduce(jnp.maximum, tiles), axis=1, keepdims=True) + fq2)
            alpha = jnp.exp2(m_prev - m_new)
            shift = fq2 - m_new
            ps = [jnp.exp2(st + shift) for st in tiles]
            l_sc[g] = alpha * l_sc[g] + functools.reduce(jnp.add, ps)
            p = jnp.concatenate([x.astype(BF16) for x in ps], axis=1)
            acc_sc[g] = alpha * acc_sc[g] + _dot(p, v_ref[0, pl.ds(start, tq), hs])
            m_sc[g] = m_new

    def off_diagonal(kj, carry):
        block(kj, False)
        return carry

    lax.fori_loop(0, qi, off_diagonal, 0)
    block(qi, True)
    for g in range(n_group):
        o_ref[0, :, g * hd:(g + 1) * hd] = acc_sc[g] / jnp.sum(l_sc[g], axis=1, keepdims=True)


def _prompt_attention(q, k, v, fq, fk, n_heads):
    b, l, w = q.shape
    hd = w // n_heads
    tq = ATT_BLOCK
    ng = ATT_HEAD_GROUP
    assert fk.shape == (b, n_heads, l // tq, tq) and n_heads % ng == 0
    return pl.pallas_call(
        functools.partial(_pattn_kernel, scale=hd ** -0.5),
        grid=(b, n_heads // ng, l // tq),
        in_specs=[pl.BlockSpec((1, tq, ng * hd), lambda bi, h, i: (bi, i, h)),
                  pl.BlockSpec((1, l, ng * hd), lambda bi, h, i: (bi, 0, h)),
                  pl.BlockSpec((1, l, ng * hd), lambda bi, h, i: (bi, 0, h)),
                  pl.BlockSpec((1, ng, tq, LANES), lambda bi, h, i: (bi, h, i, 0)),
                  pl.BlockSpec((1, ng, l // tq, tq), lambda bi, h, i: (bi, h, 0, 0))],
        out_specs=pl.BlockSpec((1, tq, ng * hd), lambda bi, h, i: (bi, i, h)),
        out_shape=jax.ShapeDtypeStruct((b, l, w), F32),
        scratch_shapes=[pltpu.VMEM((ng, tq, LANES), F32), pltpu.VMEM((ng, tq, LANES), F32), pltpu.VMEM((ng, tq, hd), F32),
                        pltpu.VMEM((ng, tq, LANES), F32)],
        compiler_params=_cparams("arbitrary", "arbitrary", "arbitrary"),
        name="prompt_attention",
    )(q, k, v, fq, fk)


def _page_sums_kernel(x_ref, m_after_ref, m_all_ref, dloc_ref, tot_ref):
    x1, x2, x3 = _split3(x_ref[...])
    ma = m_after_ref[...]
    mt = m_all_ref[...]
    dloc_ref[...] = _dot(x1, ma) + _dot(x2, ma) + _dot(x3, ma)
    tot_ref[...] = _dot(x1, mt) + _dot(x2, mt) + _dot(x3, mt)


def _page_sums(logf_pages, n_heads):
    n_pool, w = logf_pages.shape
    r = lax.broadcasted_iota(jnp.int32, (w, w), 0)
    c = lax.broadcasted_iota(jnp.int32, (w, w), 1)
    same_head = (r % n_heads) == (c % n_heads)
    m_after = (same_head & (r // n_heads > c // n_heads)).astype(BF16)
    m_all = same_head.astype(BF16)
    tp = 256
    return pl.pallas_call(
        _page_sums_kernel,
        grid=(n_pool // tp,),
        in_specs=[pl.BlockSpec((tp, w), lambda i: (i, 0)), _const_spec((w, w)), _const_spec((w, w))],
        out_specs=[pl.BlockSpec((tp, w), lambda i: (i, 0))] * 2,
        out_shape=[jax.ShapeDtypeStruct((n_pool, w), F32)] * 2,
        compiler_params=_cparams("arbitrary"),
        name="page_logf_sums",
    )(logf_pages, m_after, m_all)


def _lane_fold(x, op):
    out = x[:, 0:LANES]
    for c in range(1, x.shape[1] // LANES):
        out = op(out, x[:, c * LANES:(c + 1) * LANES])
    return out


def _sattn_kernel(pt_ref, q_ref, kn_ref, vn_ref, cqc_ref, cqr_ref, mb_ref, mbn_ref, ck_hbm, cv_hbm, dloc_hbm, tot_hbm,
                  o_ref, kbuf, vbuf, dbuf, tbuf, sbuf, sems, *, scale):
    b = pl.program_id(0)
    n_pages = kbuf.shape[1]
    slot = lax.rem(b, 2)

    def page_copies(bb, sl, i):
        p = pt_ref[bb, i]
        return (pltpu.make_async_copy(ck_hbm.at[p], kbuf.at[sl, i], sems.at[sl, 0]),
                pltpu.make_async_copy(cv_hbm.at[p], vbuf.at[sl, i], sems.at[sl, 1]),
                pltpu.make_async_copy(dloc_hbm.at[pl.ds(p, 1)], dbuf.at[sl, i], sems.at[sl, 2]),
                pltpu.make_async_copy(tot_hbm.at[pl.ds(p, 1)], tbuf.at[sl, i], sems.at[sl, 3]))

    def fetch(bb, sl):
        for i in range(n_pages):
            for cp in page_copies(bb, sl, i):
                cp.start()

    @pl.when(b == 0)
    def _():
        fetch(0, 0)

    @pl.when(b + 1 < pl.num_programs(0))
    def _():
        fetch(b + 1, 1 - slot)

    for i in range(n_pages):
        for cp in page_copies(b, slot, i):
            cp.wait()

    q = q_ref[0].astype(BF16)
    cq = cqc_ref[0][:, 0:1]
    base = mb_ref[...] + cq

    def scores(k, carry):
        after, mrun = carry
        i = n_pages - 1 - k
        s = _dot_nt(q, kbuf[slot, i].astype(BF16)) * scale + base + (dbuf[slot, i] + after)
        sbuf[i] = s
        return after + tbuf[slot, i], jnp.maximum(mrun, _lane_fold(s, jnp.maximum))

    rows = q.shape[0]
    init = (jnp.zeros(dbuf.shape[2:], F32), jnp.full((rows, LANES), NEG, F32))
    _, mrun = lax.fori_loop(0, n_pages, scores, init, unroll=True)
    sn = _dot_nt(q, kn_ref[0].astype(BF16)) * scale + (cq - cqr_ref[0]) + mbn_ref[...]
    m = jnp.maximum(jnp.max(mrun, axis=1, keepdims=True), jnp.max(sn, axis=1, keepdims=True))

    def values(i, carry):
        lrun, acc = carry
        p = jnp.exp(sbuf[i] - m)
        return lrun + _lane_fold(p, jnp.add), acc + _dot(p.astype(BF16), vbuf[slot, i].astype(BF16))

    init = (jnp.zeros((rows, LANES), F32), jnp.zeros((rows, q.shape[1]), F32))
    lrun, acc = lax.fori_loop(0, n_pages, values, init, unroll=True)
    pn = jnp.exp(sn - m)
    l = jnp.sum(lrun, axis=1, keepdims=True) + jnp.sum(pn, axis=1, keepdims=True)
    acc = acc + _dot(pn.astype(BF16), vn_ref[0].astype(BF16))
    o_ref[0] = acc / l


def _sample_attention(page_table, q, kn, vn, cache_k, cache_v, dloc, tot, cq_col, cq_row, n_heads):
    b, rows, hd = q.shape
    n_pages = page_table.shape[1]
    cols = cache_k.shape[1]
    r = lax.broadcasted_iota(jnp.int32, (rows, cols), 0)
    c = lax.broadcasted_iota(jnp.int32, (rows, cols), 1)
    mask_past = jnp.where((r % n_heads) == (c % n_heads), 0.0, NEG).astype(F32)
    rn = lax.broadcasted_iota(jnp.int32, (rows, rows), 0)
    cn = lax.broadcasted_iota(jnp.int32, (rows, rows), 1)
    mask_new = jnp.where(((rn % n_heads) == (cn % n_heads)) & (cn // n_heads <= rn // n_heads), 0.0, NEG).astype(F32)
    per_b = lambda bi, pt: (bi, 0, 0)
    const2 = lambda bi, pt: (0, 0)
    hbm = pl.BlockSpec(memory_space=pltpu.HBM)
    grid_spec = pltpu.PrefetchScalarGridSpec(
        num_scalar_prefetch=1,
        grid=(b,),
        in_specs=[pl.BlockSpec((1, rows, hd), per_b), pl.BlockSpec((1, rows, hd), per_b), pl.BlockSpec((1, rows, hd), per_b),
                  pl.BlockSpec((1, rows, LANES), per_b), pl.BlockSpec((1, 1, rows), per_b),
                  pl.BlockSpec((rows, cols), const2), pl.BlockSpec((rows, rows), const2),
                  hbm, hbm, hbm, hbm],
        out_specs=pl.BlockSpec((1, rows, hd), per_b),
        scratch_shapes=[pltpu.VMEM((2, n_pages, cols, hd), F32), pltpu.VMEM((2, n_pages, cols, hd), F32),
                        pltpu.VMEM((2, n_pages, 1, cols), F32), pltpu.VMEM((2, n_pages, 1, cols), F32),
                        pltpu.VMEM((n_pages, rows, cols), F32), pltpu.SemaphoreType.DMA((2, 4))],
    )
    return pl.pallas_call(
        functools.partial(_sattn_kernel, scale=hd ** -0.5),
        grid_spec=grid_spec,
        out_shape=jax.ShapeDtypeStruct((b, rows, hd), F32),
        compiler_params=_cparams("arbitrary"),
        name="sample_attention",
    )(page_table, q, kn, vn, cq_col, cq_row, mask_past, mask_new, cache_k, cache_v, dloc, tot)


def _s5_tables_kernel(are_ref, aim_ref, ldt_ref, pre_ref, pim_ref, cre_ref, cim_ref):
    a_re = jnp.minimum(are_ref[...], -1e-4)
    a_im = aim_ref[...]
    dt = jnp.exp(ldt_ref[...])
    mag = jnp.exp(a_re * dt)
    abr = mag * jnp.cos(a_im * dt)
    abi = mag * jnp.sin(a_im * dt)
    den = a_re * a_re + a_im * a_im
    nr, ni = abr - 1.0, abi
    cre_ref[...] = (nr * a_re + ni * a_im) / den
    cim_ref[...] = (ni * a_re - nr * a_im) / den
    pr, pi = abr, abi
    for r in range(SUBLANES):
        pre_ref[r:r + 1, :] = pr
        pim_ref[r:r + 1, :] = pi
        pr, pi = pr * abr - pi * abi, pr * abi + pi * abr


def _s5_tables(a_re, a_im, log_dt):
    g, p = a_re.shape
    n = g * p
    flat = lambda x: x.reshape(1, n)
    ldt = jnp.broadcast_to(log_dt[:, None], (g, p))
    row = jax.ShapeDtypeStruct((1, n), F32)
    tab = jax.ShapeDtypeStruct((SUBLANES, n), F32)
    return pl.pallas_call(
        _s5_tables_kernel,
        out_shape=[tab, tab, row, row],
        name="s5_tables",
    )(flat(a_re), flat(a_im), flat(ldt))


def _s5_bbar_kernel(cre_ref, cim_ref, bre_ref, bim_ref, ore_ref, oim_ref):
    cr, ci = cre_ref[...], cim_ref[...]
    br, bi = bre_ref[...], bim_ref[...]
    ore_ref[...] = cr * br - ci * bi
    oim_ref[...] = cr * bi + ci * br


def _s5_bbar(coef_re, coef_im, b_re, b_im):
    shp = jax.ShapeDtypeStruct(b_re.shape, F32)
    return pl.pallas_call(_s5_bbar_kernel, out_shape=[shp, shp], name="s5_bbar")(
        coef_re[..., None], coef_im[..., None], b_re, b_im)


def _lane_order(re, im):
    lead = re.shape[:-2]
    n = re.shape[-2] * re.shape[-1]
    return jnp.concatenate([re.reshape(lead + (n,)), im.reshape(lead + (n,))], axis=-1)


def _lane_unorder(x, g, p):
    lead = x.shape[:-1]
    n = g * p
    return x[..., :n].reshape(lead + (g, p)), x[..., n:].reshape(lead + (g, p))


def _block_diag_b(bbar_re, bbar_im):
    g, p, c = bbar_re.shape
    k = SSM_CHUNK_GROUPS
    nch = g // k
    cols = 2 * k * p
    src = jnp.stack([bbar_re, bbar_im]).reshape(2, nch, k, p, c).transpose(1, 4, 0, 2, 3).reshape(nch, 1, c, cols)
    col_group = (lax.broadcasted_iota(jnp.int32, (k, 1, cols), 2) % (k * p)) // p
    keep = col_group == lax.broadcasted_iota(jnp.int32, (k, 1, cols), 0)
    return jnp.where(keep[None], src, 0.0).reshape(nch, k * c, cols)


def _block_diag_c(c_re, c_im):
    g, c, p = c_re.shape
    k = SSM_CHUNK_GROUPS
    nch = g // k
    rows = 2 * k * p
    src = jnp.stack([c_re, -c_im]).reshape(2, nch, k, c, p).transpose(1, 0, 2, 4, 3).reshape(nch, rows, c)
    row_group = (lax.broadcasted_iota(jnp.int32, (rows, k * c), 0) % (k * p)) // p
    keep = row_group == lax.broadcasted_iota(jnp.int32, (rows, k * c), 1) // c
    return jnp.where(keep[None], jnp.tile(src, (1, 1, k)), 0.0)


def _s5_scan_chunk(hbuf, pw_ref, ch, carry_in, carry_ref, state_ref):
    cw = SSM_CHUNK_GROUPS * SSM_STATE
    n = hbuf.shape[1] // 2
    rows = lax.broadcasted_iota(jnp.int32, (SUBLANES, SCAN_LANES), 0)
    for sub in range(cw // SCAN_LANES):
        re0 = ch * cw + sub * SCAN_LANES
        sl_re = pl.ds(re0, SCAN_LANES)
        sl_im = pl.ds(n + re0, SCAN_LANES)
        pr = pw_ref[:, sl_re]
        pi = pw_ref[:, sl_im]
        steps = []
        for sh in (1, 2, 4):
            ar = jnp.where(rows >= sh, pr[sh - 1:sh, :], 0.0)
            ai = jnp.where(rows >= sh, pi[sh - 1:sh, :], 0.0)
            steps.append((sh, ar, ai))
        if carry_in is None:
            hr, hi = carry_ref[:, sl_re], carry_ref[:, sl_im]
        for t in range(hbuf.shape[0] // SUBLANES):
            tile = pl.ds(t * SUBLANES, SUBLANES)
            xr = hbuf[tile, sl_re]
            xi = hbuf[tile, sl_im]
            for sh, ar, ai in steps:
                sr = pltpu.roll(xr, sh, 0)
                si = pltpu.roll(xi, sh, 0)
                xr, xi = xr + ar * sr - ai * si, xi + ar * si + ai * sr
            if carry_in is not None:
                hr, hi = carry_in(t, sl_re), carry_in(t, sl_im)
            xr, xi = xr + pr * hr - pi * hi, xi + pr * hi + pi * hr
            hbuf[tile, sl_re] = xr
            hbuf[tile, sl_im] = xi
            hr, hi = xr[SUBLANES - 1:, :], xi[SUBLANES - 1:, :]
            if state_ref is not None:
                state_ref[t:t + 1, sl_re] = hr
                state_ref[t:t + 1, sl_im] = hi
        if carry_in is None:
            carry_ref[:, sl_re] = hr
            carry_ref[:, sl_im] = hi


def _s5_block(u_ref, bhi_ref, blo_ref, pw_ref, cc_ref, d_ref, wg_ref, bg_ref, o_ref, hbuf, carry_in, carry_ref, state_ref):
    u = u_ref[...]
    u_hi, u_lo = _split2(u)
    n_ch, kc, nc = bhi_ref.shape
    cw = nc // 2
    n = hbuf.shape[1] // 2
    parts = lambda c: (slice(c * cw, (c + 1) * cw), slice(n + c * cw, n + (c + 1) * cw))
    ys = []
    for c in range(n_ch + 2):
        if c < n_ch:
            uh = u_hi[:, c * kc:(c + 1) * kc]
            ul = u_lo[:, c * kc:(c + 1) * kc]
            bu = _dot(uh, bhi_ref[c]) + _dot(uh, blo_ref[c]) + _dot(ul, bhi_ref[c])
            re, im = parts(c)
            hbuf[:, re] = bu[:, :cw]
            hbuf[:, im] = bu[:, cw:]
        if 1 <= c <= n_ch:
            _s5_scan_chunk(hbuf, pw_ref, c - 1, carry_in, carry_ref, state_ref)
        if c >= 2:
            re, im = parts(c - 2)
            h = jnp.concatenate([hbuf[:, re], hbuf[:, im]], axis=1).astype(BF16)
            ys.append(_dot(h, cc_ref[c - 2]))
    y = jnp.concatenate(ys, axis=1) + d_ref[...] * u
    z = jax.nn.gelu(y)
    o_ref[...] = z * jax.nn.sigmoid(_dot(z.astype(BF16), wg_ref[...]) + bg_ref[...])


def _s5_prompt_kernel(u_ref, bhi_ref, blo_ref, pw_ref, cc_ref, d_ref, wg_ref, bg_ref, o_ref, st_ref, hbuf, carry_ref):
    ti = pl.program_id(1)

    @pl.when(ti == 0)
    def _():
        carry_ref[...] = jnp.zeros_like(carry_ref)

    _s5_block(u_ref, bhi_ref, blo_ref, pw_ref, cc_ref, d_ref, wg_ref, bg_ref, o_ref, hbuf, None, carry_ref, None)

    @pl.when(ti == pl.num_programs(1) - 1)
    def _():
        st_ref[0] = carry_ref[...]


def _s5_sample_kernel(u_ref, h0_ref, bhi_ref, blo_ref, pw_ref, cc_ref, d_ref, wg_ref, bg_ref, o_ref, st_ref, hbuf):
    _s5_block(u_ref, bhi_ref, blo_ref, pw_ref, cc_ref, d_ref, wg_ref, bg_ref, o_ref, hbuf,
              lambda t, lanes: h0_ref[t:t + 1, lanes], None, st_ref)


def _s5_weight_specs(bhi, pw, cc, width):
    return [_const_spec(bhi.shape), _const_spec(bhi.shape), _const_spec(pw.shape), _const_spec(cc.shape),
            _const_spec((1, width)), _const_spec((width, width)), _const_spec((1, width))]


def _s5_prompt(u, n_batch, bhi, blo, pw, cc, d_skip, w_glu, b_glu):
    t, width = u.shape
    per = t // n_batch // TOK_BLOCK
    two_n = pw.shape[1]
    tok = pl.BlockSpec((TOK_BLOCK, width), lambda b, i: (b * per + i, 0))
    return pl.pallas_call(
        _s5_prompt_kernel,
        grid=(n_batch, per),
        in_specs=[tok] + _s5_weight_specs(bhi, pw, cc, width),
        out_specs=[tok, pl.BlockSpec((1, 1, two_n), lambda b, i: (b, 0, 0))],
        out_shape=[jax.ShapeDtypeStruct((t, width), F32), jax.ShapeDtypeStruct((n_batch, 1, two_n), F32)],
        scratch_shapes=[pltpu.VMEM((TOK_BLOCK, two_n), F32), pltpu.VMEM((1, two_n), F32)],
        compiler_params=_cparams("arbitrary", "arbitrary"),
        name="s5_prompt",
    )(u, bhi, blo, pw, cc, d_skip, w_glu, b_glu)


def _s5_sample(u, h0, bhi, blo, pw, cc, d_skip, w_glu, b_glu):
    t, width = u.shape
    two_n = pw.shape[1]
    tiles = TOK_BLOCK // SUBLANES
    tok = pl.BlockSpec((TOK_BLOCK, width), lambda i: (i, 0))
    st = pl.BlockSpec((tiles, two_n), lambda i: (i, 0))
    return pl.pallas_call(
        _s5_sample_kernel,
        grid=(t // TOK_BLOCK,),
        in_specs=[tok, st] + _s5_weight_specs(bhi, pw, cc, width),
        out_specs=[tok, st],
        out_shape=[jax.ShapeDtypeStruct((t, width), F32), jax.ShapeDtypeStruct(h0.shape, F32)],
        scratch_shapes=[pltpu.VMEM((TOK_BLOCK, two_n), F32)],
        compiler_params=_cparams("arbitrary"),
        name="s5_sample",
    )(u, h0, bhi, blo, pw, cc, d_skip, w_glu, b_glu)


def _outproj_kernel(att_ref, ssm_ref, ga_ref, gs_ref, wa_ref, ws_ref, x_ref, gate_ref, o_ref):
    def norm(v, g):
        return (v * lax.rsqrt(jnp.mean(v * v, axis=-1, keepdims=True) + EPS) * g).astype(BF16)

    merged = _dot(norm(att_ref[...], ga_ref[...]), wa_ref[...]) + _dot(norm(ssm_ref[...], gs_ref[...]), ws_ref[...])
    tg, _, d = x_ref.shape
    o_ref[...] = x_ref[...] + gate_ref[...] * merged.reshape(tg, SUBLANES, d)


def _out_projection(att, ssm, g_att, g_ssm, w_att, w_ssm, xg, gate):
    ng, _, d = xg.shape
    tg = TOK_BLOCK // SUBLANES
    steps = ng // tg
    wa = att.shape[1]
    ws = ssm.shape[1]
    return pl.pallas_call(
        _outproj_kernel,
        grid=(steps,),
        in_specs=[pl.BlockSpec((TOK_BLOCK, wa), lambda i: (i, 0)), pl.BlockSpec((TOK_BLOCK, ws), lambda i: (i, 0)),
                  _const_spec((1, wa)), _const_spec((1, ws)), _const_spec((wa, d)), _const_spec((ws, d)),
                  pl.BlockSpec((tg, SUBLANES, d), lambda i: (i, 0, 0)), _mod_spec(gate.shape[0], steps, tg, d)],
        out_specs=pl.BlockSpec((tg, SUBLANES, d), lambda i: (i, 0, 0)),
        out_shape=jax.ShapeDtypeStruct(xg.shape, F32),
        compiler_params=_cparams("arbitrary"),
        name="out_projection",
    )(att, ssm, g_att, g_ssm, w_att, w_ssm, xg, gate)


def _topk_rows(problems, k, row=None):
    scores = [p[0] for p in problems]
    if row is None:
        row = lax.broadcasted_iota(jnp.int32, scores[0].shape, 0).astype(F32)
    for i in range(k):
        for z, (_, val_ref, idx_ref) in enumerate(problems):
            s = scores[z]
            m = jnp.max(s, axis=0, keepdims=True)
            j = jnp.min(jnp.where(s == m, row, ROW_SENTINEL), axis=0, keepdims=True)
            val_ref[i:i + 1, :] = m
            idx_ref[i:i + 1, :] = j
            scores[z] = jnp.where(row == j, -jnp.inf, s)


def _pair_candidates():
    k = PEER_TOPK
    runs = [(0, 0, 1, 0, k), (16, 1, 1, 0, 8), (24, 2, 1, 0, 8), (32, 3, 1, 0, 8),
            (40, 8, 8, 0, 1), (48, 4, 4, 0, 1), (52, 4, 4, 1, 1), (56, 4, 4, 2, 1)]
    covered = set()
    for _, a0, na, b0, nb in runs:
        for a in range(a0, a0 + na):
            for b in range(b0, b0 + nb):
                assert (a, b) not in covered
                covered.add((a, b))
    assert all((a, b) in covered for a in range(k) for b in range(k) if (a + 1) * (b + 1) <= k)
    return runs


def _pair_flat_index():
    k = PEER_TOPK
    r = lax.broadcasted_iota(jnp.int32, (PAIR_ROWS, LANES), 0)
    out = (ROW_SENTINEL / 2 + r).astype(F32)
    for row0, a0, na, b0, nb in _pair_candidates():
        i = r - row0
        flat = (a0 * k + b0 + i) if na == 1 else ((a0 + i) * k + b0)
        out = jnp.where((i >= 0) & (i < na * nb), flat.astype(F32), out)
    return out


def _take_rows(table_ref, idx, n):
    out = jnp.zeros_like(idx)
    for r in range(n):
        out = jnp.where(idx == float(r), table_ref[r:r + 1, :], out)
    return out


def _peer_select_kernel(x_ref, shift_ref, scale_ref, xn_ref, shiftn_ref, scalen_ref, g_ref, wqt_ref, sk_ref,
                        h_ref, e_ref, gw_ref, qt_sc, qn_sc, hc_sc, hn_sc, val_sc, idx_sc, cs_sc, top_sc, pick_sc):
    tg, _, d = x_ref.shape
    n_sides, n_keys, half = sk_ref.shape
    n_groups = n_sides // 2 // PEER_HEAD_GROUP
    q_rows = qt_sc.shape[0] // n_groups

    def normed(x, shift, scale):
        return _rms_mod(x[...], g_ref[...], shift[...], scale[...]).reshape(tg * SUBLANES, d).astype(BF16)

    @pl.when(pl.program_id(0) == 0)
    def _():
        h0 = normed(x_ref, shift_ref, scale_ref)
        hc_sc[...] = h0
        qt_sc[...] = _dot_nt(wqt_ref[...], h0).astype(BF16)

    h_ref[...] = hc_sc[...]
    hn_sc[...] = normed(xn_ref, shiftn_ref, scalen_ref)
    k = PEER_TOPK
    n_chunks = tg * SUBLANES // LANES
    flat = _pair_flat_index()
    runs = _pair_candidates()
    n_used = max(row0 + na * nb for row0, _, na, _, nb in runs)
    units = [(hg, c) for hg in range(PEER_HEAD_GROUP) for c in range(n_chunks)]
    for z in range(len(units)):
        cs_sc[z, n_used:, :] = jnp.full((PAIR_ROWS - n_used, LANES), -jnp.inf, F32)

    def group_body(group, carry):
        first = []
        for z, (hg, c) in enumerate(units):
            for side in range(2):
                hx = 2 * (group * PEER_HEAD_GROUP + hg) + side
                r0 = pl.multiple_of(hx * half, half)
                s = _dot(sk_ref[hx], qt_sc[pl.ds(r0, half), pl.ds(c * LANES, LANES)])
                first.append((s, val_sc.at[z, side], idx_sc.at[z, side]))
        _topk_rows(first, k)
        second = []
        for z in range(len(units)):
            for row0, a0, na, b0, nb in runs:
                cs_sc[z, row0:row0 + na * nb, :] = val_sc[z, 0, a0:a0 + na, :] + val_sc[z, 1, b0:b0 + nb, :]
            second.append((cs_sc[z], top_sc.at[z], pick_sc.at[z]))
        _topk_rows(second, k, flat)
        for z, (hg, c) in enumerate(units):
            r_out = pl.multiple_of((group * PEER_HEAD_GROUP + hg) * k, k)
            cols = pl.ds(c * LANES, LANES)
            ts = top_sc[z]
            p = jnp.exp(ts - ts[0:1, :])
            gw_ref[pl.ds(r_out, k), cols] = p / jnp.sum(p, axis=0, keepdims=True)
            pick = pick_sc[z]
            a = jnp.floor(pick * (1.0 / k))
            first_key = _take_rows(idx_sc.at[z, 0], a, k)
            second_key = _take_rows(idx_sc.at[z, 1], pick - a * float(k), k)
            e_ref[pl.ds(r_out, k), cols] = (first_key * float(n_keys) + second_key).astype(jnp.int32)
        q_slab = pl.ds(pl.multiple_of(group * q_rows, q_rows), q_rows)
        qn_sc[q_slab, :] = _dot_nt(wqt_ref[q_slab, :], hn_sc[...]).astype(BF16)
        return carry

    lax.fori_loop(0, n_groups, group_body, 0)
    qt_sc[...] = qn_sc[...]
    hc_sc[...] = hn_sc[...]


def _peer_select(xg, shift, scale, g, w_q_t, subkeys):
    ng, _, d = xg.shape
    t = ng * SUBLANES
    tg = TOK_BLOCK // SUBLANES
    steps = ng // tg
    n_sides = subkeys.shape[0]
    k = PEER_TOPK
    rows = (n_sides // 2) * k
    nch = PEER_HEAD_GROUP * TOK_BLOCK // LANES
    assert (n_sides // 2) % PEER_HEAD_GROUP == 0
    mod_spec = _mod_spec(shift.shape[0], steps, tg, d)
    following = lambda i: jnp.minimum(i + 1, steps - 1)
    mod_next = _mod_spec(shift.shape[0], steps, tg, d, following)
    x_spec = pl.BlockSpec((tg, SUBLANES, d), lambda i: (i, 0, 0))
    x_next = pl.BlockSpec((tg, SUBLANES, d), lambda i: (following(i), 0, 0))
    q_tile = pltpu.VMEM((w_q_t.shape[0], TOK_BLOCK), BF16)
    h_tile = pltpu.VMEM((TOK_BLOCK, d), BF16)
    return pl.pallas_call(
        _peer_select_kernel,
        grid=(steps,),
        in_specs=[x_spec, mod_spec, mod_spec, x_next, mod_next, mod_next, _const_spec((1, d)),
                  _const_spec(w_q_t.shape), _const_spec(subkeys.shape)],
        out_specs=[pl.BlockSpec((TOK_BLOCK, d), lambda i: (i, 0)), pl.BlockSpec((rows, TOK_BLOCK), lambda i: (0, i)),
                   pl.BlockSpec((rows, TOK_BLOCK), lambda i: (0, i))],
        out_shape=[jax.ShapeDtypeStruct((t, d), BF16), jax.ShapeDtypeStruct((rows, t), jnp.int32),
                   jax.ShapeDtypeStruct((rows, t), F32)],
        scratch_shapes=[q_tile, q_tile, h_tile, h_tile,
                        pltpu.VMEM((nch, 2, k, LANES), F32), pltpu.VMEM((nch, 2, k, LANES), F32),
                        pltpu.VMEM((nch, PAIR_ROWS, LANES), F32),
                        pltpu.VMEM((nch, k, LANES), F32), pltpu.VMEM((nch, k, LANES), F32)],
        compiler_params=_cparams("arbitrary"),
        name="peer_select",
    )(xg, shift, scale, xg, shift, scale, g, w_q_t, subkeys)


def _peer_mask_kernel(e_ref, g_ref, o_ref, et_sc, gt_sc):
    n_tiles = o_ref.shape[0]
    n2 = o_ref.shape[2]
    n1 = n_tiles * SUBLANES
    picks = e_ref.shape[0]
    shift = n2.bit_length() - 1
    et_sc[...] = e_ref[...].T
    gt_sc[...] = g_ref[...].T
    as_bf16 = lambda i: i.astype(F32).astype(BF16)
    sub1 = as_bf16(lax.broadcasted_iota(jnp.int32, (n1, picks), 0))
    sub2 = as_bf16(lax.broadcasted_iota(jnp.int32, (n2, picks), 0))
    zero = jnp.zeros((), BF16)

    def body(t, _):
        e = et_sc[pl.ds(t, 1), :]
        g_hi, g_lo = _split2(gt_sc[pl.ds(t, 1), :])
        hit1 = sub1 == as_bf16(jnp.right_shift(e, shift))
        hit2 = sub2 == as_bf16(jnp.bitwise_and(e, n2 - 1))
        r = jnp.where(hit2, jnp.ones((), BF16), zero)
        lhs = jnp.concatenate([jnp.where(hit1, g_hi, zero), jnp.where(hit1, g_lo, zero)], axis=1)
        m = _dot_nt(lhs, jnp.concatenate([r, r], axis=1))
        r0 = pl.multiple_of(t * SUBLANES, SUBLANES)
        for jb in range(n_tiles):
            o_ref[jb, pl.ds(r0, SUBLANES), :] = m[jb * SUBLANES:(jb + 1) * SUBLANES, :]
        return 0

    lax.fori_loop(0, et_sc.shape[0], body, 0, unroll=32)


def _peer_mask(e_pick, g_pick):
    picks, t = e_pick.shape
    n = PEER_N_KEYS
    assert n & (n - 1) == 0 and n <= 256
    tb = MASK_TOK_BLOCK
    return pl.pallas_call(
        _peer_mask_kernel,
        grid=(t // tb,),
        in_specs=[pl.BlockSpec((picks, tb), lambda i: (0, i)), pl.BlockSpec((picks, tb), lambda i: (0, i))],
        out_specs=pl.BlockSpec((n // SUBLANES, tb * SUBLANES, n), lambda i: (0, i, 0)),
        out_shape=jax.ShapeDtypeStruct((n // SUBLANES, t * SUBLANES, n), F32),
        scratch_shapes=[pltpu.VMEM((tb, picks), jnp.int32), pltpu.VMEM((tb, picks), F32)],
        compiler_params=_cparams("arbitrary"),
        name="peer_mask",
    )(e_pick, g_pick)


def _peer_ffn_kernel(h_ref, u_ref, v_ref, m_ref, x_ref, gate_ref, gf_ref, o_ref, acc_ref):
    j = pl.program_id(1)

    @pl.when(j == 0)
    def _():
        acc_ref[...] = jnp.zeros_like(acc_ref)

    tm = h_ref.shape[0]
    n2 = m_ref.shape[2]
    a = _dot_nt(h_ref[...], u_ref[...])
    m2 = m_ref.at[0]
    parts = []
    for i1 in range(SUBLANES):
        gate = m2[pl.ds(i1, tm, stride=SUBLANES), :]
        parts.append((gate * jax.nn.gelu(a[:, i1 * n2:(i1 + 1) * n2])).astype(BF16))
    acc_ref[...] += _dot(jnp.concatenate(parts, axis=1), v_ref[...])

    @pl.when(j == pl.num_programs(1) - 1)
    def _():
        tg, _, d = x_ref.shape
        x2 = x_ref[...] + gate_ref[...] * acc_ref[...].reshape(tg, SUBLANES, d)
        ms = jnp.mean(x2 * x2, axis=-1, keepdims=True)
        o_ref[...] = x2 * lax.rsqrt(ms + EPS) * gf_ref[...]


def _peer_ffn(h, exp_u, exp_v, mask, xg, gate, g_final):
    t, d = h.shape
    ne = exp_u.shape[0]
    n = PEER_N_KEYS
    tm = PEER_TOK_BLOCK
    te = PEER_EXPERT_TILE
    assert te == SUBLANES * n and mask.shape == (ne // te, t * SUBLANES, n)
    tg = tm // SUBLANES
    steps = t // tm
    return pl.pallas_call(
        _peer_ffn_kernel,
        grid=(steps, ne // te),
        in_specs=[pl.BlockSpec((tm, d), lambda i, j: (i, 0)),
                  pl.BlockSpec((te, d), lambda i, j: (j, 0)), pl.BlockSpec((te, d), lambda i, j: (j, 0)),
                  pl.BlockSpec((1, tm * SUBLANES, n), lambda i, j: (j, i, 0)),
                  pl.BlockSpec((tg, SUBLANES, d), lambda i, j: (i, 0, 0)),
                  _mod_spec(gate.shape[0], steps, tg, d), _const_spec((1, d))],
        out_specs=pl.BlockSpec((tg, SUBLANES, d), lambda i, j: (i, 0, 0)),
        out_shape=jax.ShapeDtypeStruct(xg.shape, F32),
        scratch_shapes=[pltpu.VMEM((tm, d), F32)],
        compiler_params=_cparams("arbitrary", "arbitrary"),
        name="peer_ffn",
    )(h, exp_u, exp_v, mask, xg, gate, g_final)


def _token_stages_pre(xg, mods, lw, emit_bf16):
    shift1, scale1 = mods[0], mods[1]
    return _in_projection(xg, shift1, scale1, lw["g_norm1"], lw["w_in"], lw["w_u"], lw["w_fg"], lw["bf_pad"], emit_bf16)


def _token_stages_post(xg, mods, att, ssm, lw, g_final):
    gate1, shift2, scale2, gate2 = mods[2], mods[3], mods[4], mods[5]
    x1 = _out_projection(att, ssm, lw["g_attn_out"], lw["g_ssm_out"], lw["w_out_att"], lw["w_out_ssm"], xg, gate1)
    h2, e_t, g_t = _peer_select(x1, shift2, scale2, lw["g_norm2"], lw["peer_w_q_t"], lw["peer_subkeys"])
    mask = _peer_mask(e_t, g_t)
    return _peer_ffn(h2, lw["peer_u"], lw["peer_v"], mask, x1, gate2, g_final)


def kernel(x_prompt, x_sample, cache_k, cache_v, cache_logf, state_ssm_re, state_ssm_im, page_table, c_prompt, c_sample, w_ada, b_ada, g_norm1, g_norm2, w_in, b_forget, ssm_a_re, ssm_a_im, ssm_log_dt, ssm_b_re, ssm_b_im, ssm_c_re, ssm_c_im, ssm_d, w_glu, b_glu, g_attn_out, g_ssm_out, w_out, peer_w_q, peer_subkeys, peer_u, peer_v, g_final):
    depth = w_ada.shape[0]
    assert depth == 1, "single trunk layer"
    nb, seq, d = x_prompt.shape
    nd, dseq, _ = x_sample.shape
    assert dseq == SUBLANES
    n_pool, page, n_heads, hd = cache_k.shape[1:]
    fox_w = n_heads * hd
    n_groups, n_state = ssm_a_re.shape[1:]
    ssm_w = n_groups * SSM_CH_PER_GROUP
    assert fox_w == ssm_w
    n_pages = page_table.shape[1]
    layer = 0

    w_in_l = w_in[layer]
    w_fg = jnp.pad(w_in_l[:, 3 * fox_w:3 * fox_w + n_heads], ((0, 0), (0, LANES - n_heads))).astype(BF16)
    bf_pad = jnp.concatenate([b_forget[layer], jnp.zeros((LANES - n_heads,), F32)]).reshape(1, LANES)
    w_out_l = w_out[layer].astype(BF16)
    lw = dict(
        w_in=w_in_l.astype(BF16), w_u=w_in_l[:, 3 * fox_w + n_heads:].astype(BF16), w_fg=w_fg, bf_pad=bf_pad,
        g_norm1=g_norm1[layer].reshape(1, d), g_norm2=g_norm2[layer].reshape(1, d),
        g_attn_out=g_attn_out[layer].reshape(1, fox_w), g_ssm_out=g_ssm_out[layer].reshape(1, ssm_w),
        w_out_att=w_out_l[:fox_w], w_out_ssm=w_out_l[fox_w:],
        peer_w_q_t=peer_w_q[layer].T.astype(BF16),
        peer_subkeys=peer_subkeys[layer].reshape((-1,) + peer_subkeys.shape[-2:]).astype(BF16),
        peer_u=peer_u[layer].astype(BF16), peer_v=peer_v[layer].astype(BF16),
    )
    gf = g_final.reshape(1, d)

    n_c = nb + nd
    n_c_pad = -(-n_c // SUBLANES) * SUBLANES
    c_all = jnp.concatenate([c_prompt, c_sample, jnp.zeros((n_c_pad - n_c, d), F32)], axis=0)
    mod = _ada_modulation(c_all, w_ada[layer], b_ada[layer])
    mods_p = [mod[:nb, i * d:(i + 1) * d].reshape(nb, 1, d) for i in range(N_MOD)]
    mods_s = [mod[nb:n_c, i * d:(i + 1) * d].reshape(nd, 1, d) for i in range(N_MOD)]

    pw_re, pw_im, coef_re, coef_im = _s5_tables(ssm_a_re[layer], ssm_a_im[layer], ssm_log_dt[layer])
    bbar_re, bbar_im = _s5_bbar(coef_re.reshape(n_groups, n_state), coef_im.reshape(n_groups, n_state),
                                ssm_b_re[layer], ssm_b_im[layer])
    b_blk = _block_diag_b(bbar_re, bbar_im)
    b_hi = b_blk.astype(BF16)
    b_lo = (b_blk - b_hi.astype(F32)).astype(BF16)
    c_blk = _block_diag_c(ssm_c_re[layer], ssm_c_im[layer]).astype(BF16)
    pw = _lane_order(pw_re.reshape(SUBLANES, n_groups, n_state), pw_im.reshape(SUBLANES, n_groups, n_state))
    s5w = (b_hi, b_lo, pw, c_blk, ssm_d[layer].reshape(1, ssm_w), w_glu[layer].astype(BF16), b_glu[layer].reshape(1, ssm_w))

    xg_p = x_prompt.reshape(nb * seq // SUBLANES, SUBLANES, d)
    q_p, k_p, v_p, u_p, lf_p, qb_p, kb_p, vb_p = _token_stages_pre(xg_p, mods_p, lw, True)
    fcum = _cumsum_time(lf_p, nb).reshape(nb, seq, LANES)[:, :, :n_heads].transpose(0, 2, 1)
    fq = jnp.broadcast_to(fcum[..., None], (nb, n_heads, seq, LANES))
    fk = fcum.reshape(nb, n_heads, seq // ATT_BLOCK, ATT_BLOCK)
    as3 = lambda a: a.reshape(nb, seq, fox_w)
    att_p = _prompt_attention(as3(qb_p), as3(kb_p), as3(vb_p), fq, fk, n_heads).reshape(nb * seq, fox_w)
    ssm_p, st_p = _s5_prompt(u_p, nb, *s5w)
    y_p = _token_stages_post(xg_p, mods_p, att_p, ssm_p, lw, gf)
    re_p, im_p = _lane_unorder(st_p.reshape(nb, -1), n_groups, n_state)

    xg_s = x_sample.reshape(nd, SUBLANES, d)
    q_s, k_s, v_s, u_s, lf_s = _token_stages_pre(xg_s, mods_s, lw, False)
    rows = dseq * n_heads
    heads_rows = lambda a: a.reshape(nd, dseq, n_heads, hd).reshape(nd, rows, hd)
    logf_pages = cache_logf[layer].reshape(n_pool, page * n_heads)
    dloc, tot = _page_sums(logf_pages, n_heads)
    c_new = _cumsum_short(lf_s.reshape(nd, dseq, LANES))[:, :, :n_heads].reshape(nd, rows)
    cq_col = jnp.broadcast_to(c_new[:, :, None], (nd, rows, LANES))
    cq_row = c_new.reshape(nd, 1, rows)
    att_s = _sample_attention(page_table, heads_rows(q_s), heads_rows(k_s), heads_rows(v_s),
                              cache_k[layer].reshape(n_pool, page * n_heads, hd),
                              cache_v[layer].reshape(n_pool, page * n_heads, hd),
                              dloc, tot, cq_col, cq_row, n_heads)
    att_s = att_s.reshape(nd, dseq, n_heads, hd).reshape(nd * dseq, fox_w)
    h0 = _lane_order(state_ssm_re[layer], state_ssm_im[layer])
    ssm_s, st_s = _s5_sample(u_s, h0, *s5w)
    y_s = _token_stages_post(xg_s, mods_s, att_s, ssm_s, lw, gf)
    re_s, im_s = _lane_unorder(st_s, n_groups, n_state)

    kv5 = lambda a, b_, l_: a.reshape(1, b_, l_, n_heads, hd)
    return (y_p.reshape(nb, seq, d), y_s.reshape(nd, dseq, d),
            kv5(k_p, nb, seq), kv5(v_p, nb, seq), lf_p[:, :n_heads].reshape(1, nb, seq, n_heads),
            re_p[None], im_p[None],
            kv5(k_s, nd, dseq), kv5(v_s, nd, dseq), lf_s[:, :n_heads].reshape(1, nd, dseq, n_heads),
            re_s[None], im_s[None])
```

```python
import functools
import math

import jax
import jax.numpy as jnp
from jax import lax
from jax.experimental import pallas as pl
from jax.experimental.pallas import tpu as pltpu

F32 = jnp.float32
BF16 = jnp.bfloat16
EPS = 1e-6
NEG = -1e30

SUBLANES = 8
LANES = 128
VMEM_LIMIT_BYTES = 56 * 1024 * 1024

N_MOD = 6
FOX_HEAD_DIM = 128
SSM_CH_PER_GROUP = 16
SSM_STATE = 64
SSM_CHUNK_GROUPS = 16
PEER_N_KEYS = 128
PEER_TOPK = 16
PEER_EXPERT_TILE = 1024
PEER_HEAD_GROUP = 2
PAIR_ROWS = 64
ROW_SENTINEL = 1024.0

TOK_BLOCK = 256
ATT_BLOCK = 512
ATT_HEAD_GROUP = 4
LOG2E = 1.4426950408889634
PEER_TOK_BLOCK = 512
MASK_TOK_BLOCK = 128
SCAN_LANES = 512


def _cparams(*sem):
    return pltpu.CompilerParams(dimension_semantics=sem, vmem_limit_bytes=VMEM_LIMIT_BYTES)


def _const_spec(shape):
    nd = len(shape)
    return pl.BlockSpec(shape, lambda *_: (0,) * nd)


def _split2(x):
    hi = x.astype(BF16)
    lo = (x - hi.astype(F32)).astype(BF16)
    return hi, lo


def _split3(x):
    x1 = x.astype(BF16)
    r1 = x - x1.astype(F32)
    x2 = r1.astype(BF16)
    x3 = (r1 - x2.astype(F32)).astype(BF16)
    return x1, x2, x3


def _dot(a, b):
    return jnp.dot(a, b, preferred_element_type=F32)


def _dot_nt(a, b):
    return lax.dot_general(a, b, (((1,), (1,)), ((), ())), preferred_element_type=F32)


def _rms_mod(xg, g, shift, scale):
    ms = jnp.mean(xg * xg, axis=-1, keepdims=True)
    y = xg * lax.rsqrt(ms + EPS) * g
    return y * (1.0 + scale) + shift


def _mod_spec(n_mod_rows, n_steps, tg, d, block_of=lambda i: i):
    if n_mod_rows == n_steps * tg:
        return pl.BlockSpec((tg, 1, d), lambda i, *_: (block_of(i), 0, 0))
    assert n_steps % n_mod_rows == 0
    per = n_steps // n_mod_rows
    return pl.BlockSpec((1, 1, d), lambda i, *_: (block_of(i) // per, 0, 0))


def _ada_kernel(c_ref, w_ref, b_ref, o_ref):
    a = jax.nn.silu(c_ref[...]).astype(BF16)
    o_ref[...] = _dot(a, w_ref[...].astype(BF16)) + b_ref[...]


def _ada_modulation(c_all, w_ada, b_ada):
    m, d = c_all.shape
    n = w_ada.shape[1]
    tn = 1024
    return pl.pallas_call(
        _ada_kernel,
        grid=(n // tn,),
        in_specs=[_const_spec((m, d)), pl.BlockSpec((d, tn), lambda j: (0, j)), pl.BlockSpec((1, tn), lambda j: (0, j))],
        out_specs=pl.BlockSpec((m, tn), lambda j: (0, j)),
        out_shape=jax.ShapeDtypeStruct((m, n), F32),
        compiler_params=_cparams("arbitrary"),
        name="ada_modulation",
    )(c_all, w_ada, b_ada.reshape(1, n))


def _inproj_kernel(x_ref, shift_ref, scale_ref, g_ref, wqkv_ref, wu_ref, wfg_ref, bf_ref,
                   q_ref, k_ref, v_ref, u_ref, lf_ref, *bf16_refs):
    tg, _, d = x_ref.shape
    h = _rms_mod(x_ref[...], g_ref[...], shift_ref[...], scale_ref[...]).reshape(tg * SUBLANES, d).astype(BF16)
    r = _dot(h, wqkv_ref[...])
    w = q_ref.shape[1]
    q_ref[...] = r[:, 0:w]
    k_ref[...] = r[:, w:2 * w]
    v_ref[...] = r[:, 2 * w:3 * w]
    u_ref[...] = _dot(h, wu_ref[...])
    lf_ref[...] = jax.nn.log_sigmoid(_dot(h, wfg_ref[...]) + bf_ref[...])
    for i, ref in enumerate(bf16_refs):
        ref[...] = r[:, i * w:(i + 1) * w].astype(BF16)


def _in_projection(xg, shift, scale, g, w_all, w_u, w_fg, bf_pad, emit_bf16):
    ng, _, d = xg.shape
    t = ng * SUBLANES
    tg = TOK_BLOCK // SUBLANES
    steps = ng // tg
    width = w_u.shape[1]
    mod_spec = _mod_spec(shift.shape[0], steps, tg, d)
    tok = lambda w: pl.BlockSpec((TOK_BLOCK, w), lambda i: (i, 0))
    n_bf16 = 3 if emit_bf16 else 0
    return pl.pallas_call(
        _inproj_kernel,
        grid=(steps,),
        in_specs=[pl.BlockSpec((tg, SUBLANES, d), lambda i: (i, 0, 0)), mod_spec, mod_spec,
                  _const_spec((1, d)), _const_spec((d, 3 * width)), _const_spec(w_u.shape), _const_spec(w_fg.shape),
                  _const_spec((1, LANES))],
        out_specs=[tok(width)] * 4 + [tok(LANES)] + [tok(width)] * n_bf16,
        out_shape=([jax.ShapeDtypeStruct((t, width), F32)] * 4 + [jax.ShapeDtypeStruct((t, LANES), F32)]
                   + [jax.ShapeDtypeStruct((t, width), BF16)] * n_bf16),
        compiler_params=_cparams("arbitrary"),
        name="in_projection",
    )(xg, shift, scale, g, w_all, w_u, w_fg, bf_pad)


def _cumsum_kernel(x_ref, tri_ref, o_ref, carry_ref):
    @pl.when(pl.program_id(1) == 0)
    def _():
        carry_ref[...] = jnp.zeros_like(carry_ref)

    x1, x2, x3 = _split3(x_ref[...])
    tri = tri_ref[...]
    loc = _dot(tri, x1) + _dot(tri, x2) + _dot(tri, x3) + carry_ref[...]
    o_ref[...] = loc
    carry_ref[...] = loc[-1:, :]


def _cumsum_time(x, n_batch):
    t, w = x.shape
    l = t // n_batch
    tb = 256
    tri = (lax.broadcasted_iota(jnp.int32, (tb, tb), 1) <= lax.broadcasted_iota(jnp.int32, (tb, tb), 0)).astype(BF16)
    per = l // tb
    return pl.pallas_call(
        _cumsum_kernel,
        grid=(n_batch, per),
        in_specs=[pl.BlockSpec((tb, w), lambda b, i: (b * per + i, 0)), _const_spec((tb, tb))],
        out_specs=pl.BlockSpec((tb, w), lambda b, i: (b * per + i, 0)),
        out_shape=jax.ShapeDtypeStruct((t, w), F32),
        scratch_shapes=[pltpu.VMEM((1, w), F32)],
        compiler_params=_cparams("arbitrary", "arbitrary"),
        name="logf_cumsum",
    )(x, tri)


def _cumsum_short_kernel(x_ref, o_ref):
    run = x_ref[:, 0:1, :]
    o_ref[:, 0:1, :] = run
    for r in range(1, x_ref.shape[1]):
        run = run + x_ref[:, r:r + 1, :]
        o_ref[:, r:r + 1, :] = run


def _cumsum_short(x):
    return pl.pallas_call(_cumsum_short_kernel, out_shape=jax.ShapeDtypeStruct(x.shape, F32), name="logf_cumsum_new")(x)


def _pattn_kernel(*refs, scale, n_cast):
    q_ref, k_ref, v_ref, fq_ref, fk_ref = refs[:5]
    cast_in = refs[5:5 + n_cast]
    o_ref = refs[5 + n_cast]
    cast_out = refs[6 + n_cast:6 + 2 * n_cast]
    m_sc, l_sc, acc_sc, fq_sc = refs[6 + 2 * n_cast:]
    for src, dst in zip(cast_in, cast_out):
        dst[...] = src[...].astype(BF16)
    qi = pl.program_id(2)
    tq = q_ref.shape[1]
    n_group, _, hd = acc_sc.shape
    m_sc[...] = jnp.full_like(m_sc, NEG)
    l_sc[...] = jnp.zeros_like(l_sc)
    acc_sc[...] = jnp.zeros_like(acc_sc)
    fq_sc[...] = fq_ref[0] * LOG2E
    c = scale * LOG2E

    def block(kj, diagonal):
        start = pl.multiple_of(kj * tq, tq)
        for g in range(n_group):
            hs = slice(g * hd, (g + 1) * hd)
            s = _dot_nt(q_ref[0, :, hs], k_ref[0, pl.ds(start, tq), hs])
            fk2 = fk_ref[0, g, pl.ds(kj, 1), :] * LOG2E
            tiles = []
            for t in range(tq // LANES):
                ls = slice(t * LANES, (t + 1) * LANES)
                st = s[:, ls] * c - fk2[:, ls]
                if diagonal:
                    row = lax.broadcasted_iota(jnp.int32, st.shape, 0)
                    col = lax.broadcasted_iota(jnp.int32, st.shape, 1) + t * LANES
                    st = jnp.where(col <= row, st, NEG)
                tiles.append(st)
            m_prev = m_sc[g]
            fq2 = fq_sc[g]
            m_new = jnp.maximum(m_prev, jnp.max(functools.reduce(jnp.maximum, tiles), axis=1, keepdims=True) + fq2)
            alpha = jnp.exp2(m_prev - m_new)
            shift = fq2 - m_new
            ps = [jnp.exp2(st + shift) for st in tiles]
            l_sc[g] = alpha * l_sc[g] + functools.reduce(jnp.add, ps)
            p = jnp.concatenate([x.astype(BF16) for x in ps], axis=1)
            acc_sc[g] = alpha * acc_sc[g] + _dot(p, v_ref[0, pl.ds(start, tq), hs])
            m_sc[g] = m_new

    def off_diagonal(kj, carry):
        block(kj, False)
        return carry

    lax.fori_loop(0, qi, off_diagonal, 0)
    block(qi, True)
    for g in range(n_group):
        o_ref[0, :, g * hd:(g + 1) * hd] = acc_sc[g] / jnp.sum(l_sc[g], axis=1, keepdims=True)


def _prompt_attention(q, k, v, fq, fk, n_heads, tables):
    b, l, w = q.shape
    hd = w // n_heads
    tq = ATT_BLOCK
    ng = ATT_HEAD_GROUP
    assert fk.shape == (b, n_heads, l // tq, tq) and n_heads % ng == 0
    n_hg, n_q = n_heads // ng, l // tq
    steps = b * n_hg * n_q
    slab = lambda bi, h, i: ((bi * n_hg + h) * n_q + i, 0)
    for tab in tables:
        assert tab.shape[0] % (steps * 2 * SUBLANES) == 0
    table_specs = [pl.BlockSpec((tab.shape[0] // steps, tab.shape[1]), slab) for tab in tables]
    return pl.pallas_call(
        functools.partial(_pattn_kernel, scale=hd ** -0.5, n_cast=len(tables)),
        grid=(b, n_hg, n_q),
        in_specs=[pl.BlockSpec((1, tq, ng * hd), lambda bi, h, i: (bi, i, h)),
                  pl.BlockSpec((1, l, ng * hd), lambda bi, h, i: (bi, 0, h)),
                  pl.BlockSpec((1, l, ng * hd), lambda bi, h, i: (bi, 0, h)),
                  pl.BlockSpec((1, ng, tq, LANES), lambda bi, h, i: (bi, h, i, 0)),
                  pl.BlockSpec((1, ng, l // tq, tq), lambda bi, h, i: (bi, h, 0, 0))] + table_specs,
        out_specs=[pl.BlockSpec((1, tq, ng * hd), lambda bi, h, i: (bi, i, h))] + table_specs,
        out_shape=[jax.ShapeDtypeStruct((b, l, w), F32)] + [jax.ShapeDtypeStruct(tab.shape, BF16) for tab in tables],
        scratch_shapes=[pltpu.VMEM((ng, tq, LANES), F32), pltpu.VMEM((ng, tq, LANES), F32), pltpu.VMEM((ng, tq, hd), F32),
                        pltpu.VMEM((ng, tq, LANES), F32)],
        compiler_params=_cparams("arbitrary", "arbitrary", "arbitrary"),
        name="prompt_attention",
    )(q, k, v, fq, fk, *tables)


def _page_sums_kernel(x_ref, m_after_ref, m_all_ref, dloc_ref, tot_ref):
    x1, x2, x3 = _split3(x_ref[...])
    ma = m_after_ref[...]
    mt = m_all_ref[...]
    dloc_ref[...] = _dot(x1, ma) + _dot(x2, ma) + _dot(x3, ma)
    tot_ref[...] = _dot(x1, mt) + _dot(x2, mt) + _dot(x3, mt)


def _page_sums(logf_pages, n_heads):
    n_pool, w = logf_pages.shape
    r = lax.broadcasted_iota(jnp.int32, (w, w), 0)
    c = lax.broadcasted_iota(jnp.int32, (w, w), 1)
    same_head = (r % n_heads) == (c % n_heads)
    m_after = (same_head & (r // n_heads > c // n_heads)).astype(BF16)
    m_all = same_head.astype(BF16)
    tp = 256
    return pl.pallas_call(
        _page_sums_kernel,
        grid=(n_pool // tp,),
        in_specs=[pl.BlockSpec((tp, w), lambda i: (i, 0)), _const_spec((w, w)), _const_spec((w, w))],
        out_specs=[pl.BlockSpec((tp, w), lambda i: (i, 0))] * 2,
        out_shape=[jax.ShapeDtypeStruct((n_pool, w), F32)] * 2,
        compiler_params=_cparams("arbitrary"),
        name="page_logf_sums",
    )(logf_pages, m_after, m_all)


def _lane_fold(x, op):
    out = x[:, 0:LANES]
    for c in range(1, x.shape[1] // LANES):
        out = op(out, x[:, c * LANES:(c + 1) * LANES])
    return out


def _sattn_kernel(pt_ref, q_ref, kn_ref, vn_ref, cqc_ref, cqr_ref, mb_ref, mbn_ref, ck_hbm, cv_hbm, dloc_hbm, tot_hbm,
                  o_ref, kbuf, vbuf, dbuf, tbuf, sbuf, sems, *, scale):
    b = pl.program_id(0)
    n_pages = kbuf.shape[1]
    slot = lax.rem(b, 2)

    def page_copies(bb, sl, i):
        p = pt_ref[bb, i]
        return (pltpu.make_async_copy(ck_hbm.at[p], kbuf.at[sl, i], sems.at[sl, 0]),
                pltpu.make_async_copy(cv_hbm.at[p], vbuf.at[sl, i], sems.at[sl, 1]),
                pltpu.make_async_copy(dloc_hbm.at[pl.ds(p, 1)], dbuf.at[sl, i], sems.at[sl, 2]),
                pltpu.make_async_copy(tot_hbm.at[pl.ds(p, 1)], tbuf.at[sl, i], sems.at[sl, 3]))

    def fetch(bb, sl):
        for i in range(n_pages):
            for cp in page_copies(bb, sl, i):
                cp.start()

    @pl.when(b == 0)
    def _():
        fetch(0, 0)

    @pl.when(b + 1 < pl.num_programs(0))
    def _():
        fetch(b + 1, 1 - slot)

    for i in range(n_pages):
        for cp in page_copies(b, slot, i):
            cp.wait()

    q = q_ref[0].astype(BF16)
    cq = cqc_ref[0][:, 0:1]
    base = mb_ref[...] + cq

    def scores(k, carry):
        after, mrun = carry
        i = n_pages - 1 - k
        s = _dot_nt(q, kbuf[slot, i].astype(BF16)) * scale + base + (dbuf[slot, i] + after)
        sbuf[i] = s
        return after + tbuf[slot, i], jnp.maximum(mrun, _lane_fold(s, jnp.maximum))

    rows = q.shape[0]
    init = (jnp.zeros(dbuf.shape[2:], F32), jnp.full((rows, LANES), NEG, F32))
    _, mrun = lax.fori_loop(0, n_pages, scores, init, unroll=True)
    sn = _dot_nt(q, kn_ref[0].astype(BF16)) * scale + (cq - cqr_ref[0]) + mbn_ref[...]
    m = jnp.maximum(jnp.max(mrun, axis=1, keepdims=True), jnp.max(sn, axis=1, keepdims=True))

    def values(i, carry):
        lrun, acc = carry
        p = jnp.exp(sbuf[i] - m)
        return lrun + _lane_fold(p, jnp.add), acc + _dot(p.astype(BF16), vbuf[slot, i].astype(BF16))

    init = (jnp.zeros((rows, LANES), F32), jnp.zeros((rows, q.shape[1]), F32))
    lrun, acc = lax.fori_loop(0, n_pages, values, init, unroll=True)
    pn = jnp.exp(sn - m)
    l = jnp.sum(lrun, axis=1, keepdims=True) + jnp.sum(pn, axis=1, keepdims=True)
    acc = acc + _dot(pn.astype(BF16), vn_ref[0].astype(BF16))
    o_ref[0] = acc / l


def _sample_attention(page_table, q, kn, vn, cache_k, cache_v, dloc, tot, cq_col, cq_row, n_heads):
    b, rows, hd = q.shape
    n_pages = page_table.shape[1]
    cols = cache_k.shape[1]
    r = lax.broadcasted_iota(jnp.int32, (rows, cols), 0)
    c = lax.broadcasted_iota(jnp.int32, (rows, cols), 1)
    mask_past = jnp.where((r % n_heads) == (c % n_heads), 0.0, NEG).astype(F32)
    rn = lax.broadcasted_iota(jnp.int32, (rows, rows), 0)
    cn = lax.broadcasted_iota(jnp.int32, (rows, rows), 1)
    mask_new = jnp.where(((rn % n_heads) == (cn % n_heads)) & (cn // n_heads <= rn // n_heads), 0.0, NEG).astype(F32)
    per_b = lambda bi, pt: (bi, 0, 0)
    const2 = lambda bi, pt: (0, 0)
    hbm = pl.BlockSpec(memory_space=pltpu.HBM)
    grid_spec = pltpu.PrefetchScalarGridSpec(
        num_scalar_prefetch=1,
        grid=(b,),
        in_specs=[pl.BlockSpec((1, rows, hd), per_b), pl.BlockSpec((1, rows, hd), per_b), pl.BlockSpec((1, rows, hd), per_b),
                  pl.BlockSpec((1, rows, LANES), per_b), pl.BlockSpec((1, 1, rows), per_b),
                  pl.BlockSpec((rows, cols), const2), pl.BlockSpec((rows, rows), const2),
                  hbm, hbm, hbm, hbm],
        out_specs=pl.BlockSpec((1, rows, hd), per_b),
        scratch_shapes=[pltpu.VMEM((2, n_pages, cols, hd), F32), pltpu.VMEM((2, n_pages, cols, hd), F32),
                        pltpu.VMEM((2, n_pages, 1, cols), F32), pltpu.VMEM((2, n_pages, 1, cols), F32),
                        pltpu.VMEM((n_pages, rows, cols), F32), pltpu.SemaphoreType.DMA((2, 4))],
    )
    return pl.pallas_call(
        functools.partial(_sattn_kernel, scale=hd ** -0.5),
        grid_spec=grid_spec,
        out_shape=jax.ShapeDtypeStruct((b, rows, hd), F32),
        compiler_params=_cparams("arbitrary"),
        name="sample_attention",
    )(page_table, q, kn, vn, cq_col, cq_row, mask_past, mask_new, cache_k, cache_v, dloc, tot)


def _s5_tables_kernel(are_ref, aim_ref, ldt_ref, pre_ref, pim_ref, cre_ref, cim_ref):
    a_re = jnp.minimum(are_ref[...], -1e-4)
    a_im = aim_ref[...]
    dt = jnp.exp(ldt_ref[...])
    mag = jnp.exp(a_re * dt)
    abr = mag * jnp.cos(a_im * dt)
    abi = mag * jnp.sin(a_im * dt)
    den = a_re * a_re + a_im * a_im
    nr, ni = abr - 1.0, abi
    cre_ref[...] = (nr * a_re + ni * a_im) / den
    cim_ref[...] = (ni * a_re - nr * a_im) / den
    pr, pi = abr, abi
    for r in range(SUBLANES):
        pre_ref[r:r + 1, :] = pr
        pim_ref[r:r + 1, :] = pi
        pr, pi = pr * abr - pi * abi, pr * abi + pi * abr


def _s5_tables(a_re, a_im, log_dt):
    g, p = a_re.shape
    n = g * p
    flat = lambda x: x.reshape(1, n)
    ldt = jnp.broadcast_to(log_dt[:, None], (g, p))
    row = jax.ShapeDtypeStruct((1, n), F32)
    tab = jax.ShapeDtypeStruct((SUBLANES, n), F32)
    return pl.pallas_call(
        _s5_tables_kernel,
        out_shape=[tab, tab, row, row],
        name="s5_tables",
    )(flat(a_re), flat(a_im), flat(ldt))


def _s5_bbar_kernel(cre_ref, cim_ref, bre_ref, bim_ref, ore_ref, oim_ref):
    cr, ci = cre_ref[...], cim_ref[...]
    br, bi = bre_ref[...], bim_ref[...]
    ore_ref[...] = cr * br - ci * bi
    oim_ref[...] = cr * bi + ci * br


def _s5_bbar(coef_re, coef_im, b_re, b_im):
    shp = jax.ShapeDtypeStruct(b_re.shape, F32)
    return pl.pallas_call(_s5_bbar_kernel, out_shape=[shp, shp], name="s5_bbar")(
        coef_re[..., None], coef_im[..., None], b_re, b_im)


def _lane_order(re, im):
    lead = re.shape[:-2]
    n = re.shape[-2] * re.shape[-1]
    return jnp.concatenate([re.reshape(lead + (n,)), im.reshape(lead + (n,))], axis=-1)


def _lane_unorder(x, g, p):
    lead = x.shape[:-1]
    n = g * p
    return x[..., :n].reshape(lead + (g, p)), x[..., n:].reshape(lead + (g, p))


def _block_diag_b(bbar_re, bbar_im):
    g, p, c = bbar_re.shape
    k = SSM_CHUNK_GROUPS
    nch = g // k
    cols = 2 * k * p
    src = jnp.stack([bbar_re, bbar_im]).reshape(2, nch, k, p, c).transpose(1, 4, 0, 2, 3).reshape(nch, 1, c, cols)
    col_group = (lax.broadcasted_iota(jnp.int32, (k, 1, cols), 2) % (k * p)) // p
    keep = col_group == lax.broadcasted_iota(jnp.int32, (k, 1, cols), 0)
    return jnp.where(keep[None], src, 0.0).reshape(nch, k * c, cols)


def _block_diag_c(c_re, c_im):
    g, c, p = c_re.shape
    k = SSM_CHUNK_GROUPS
    nch = g // k
    rows = 2 * k * p
    src = jnp.stack([c_re, -c_im]).reshape(2, nch, k, c, p).transpose(1, 0, 2, 4, 3).reshape(nch, rows, c)
    row_group = (lax.broadcasted_iota(jnp.int32, (rows, k * c), 0) % (k * p)) // p
    keep = row_group == lax.broadcasted_iota(jnp.int32, (rows, k * c), 1) // c
    return jnp.where(keep[None], jnp.tile(src, (1, 1, k)), 0.0)


def _s5_scan_chunk(hbuf, pw_ref, ch, carry_in, carry_ref, state_ref):
    cw = SSM_CHUNK_GROUPS * SSM_STATE
    n = hbuf.shape[1] // 2
    rows = lax.broadcasted_iota(jnp.int32, (SUBLANES, SCAN_LANES), 0)
    for sub in range(cw // SCAN_LANES):
        re0 = ch * cw + sub * SCAN_LANES
        sl_re = pl.ds(re0, SCAN_LANES)
        sl_im = pl.ds(n + re0, SCAN_LANES)
        pr = pw_ref[:, sl_re]
        pi = pw_ref[:, sl_im]
        steps = []
        for sh in (1, 2, 4):
            ar = jnp.where(rows >= sh, pr[sh - 1:sh, :], 0.0)
            ai = jnp.where(rows >= sh, pi[sh - 1:sh, :], 0.0)
            steps.append((sh, ar, ai))
        if carry_in is None:
            hr, hi = carry_ref[:, sl_re], carry_ref[:, sl_im]
        for t in range(hbuf.shape[0] // SUBLANES):
            tile = pl.ds(t * SUBLANES, SUBLANES)
            xr = hbuf[tile, sl_re]
            xi = hbuf[tile, sl_im]
            for sh, ar, ai in steps:
                sr = pltpu.roll(xr, sh, 0)
                si = pltpu.roll(xi, sh, 0)
                xr, xi = xr + ar * sr - ai * si, xi + ar * si + ai * sr
            if carry_in is not None:
                hr, hi = carry_in(t, sl_re), carry_in(t, sl_im)
            xr, xi = xr + pr * hr - pi * hi, xi + pr * hi + pi * hr
            hbuf[tile, sl_re] = xr
            hbuf[tile, sl_im] = xi
            hr, hi = xr[SUBLANES - 1:, :], xi[SUBLANES - 1:, :]
            if state_ref is not None:
                state_ref[t:t + 1, sl_re] = hr
                state_ref[t:t + 1, sl_im] = hi
        if carry_in is None:
            carry_ref[:, sl_re] = hr
            carry_ref[:, sl_im] = hi


def _s5_block(u_ref, bhi_ref, blo_ref, pw_ref, cc_ref, d_ref, wg_ref, bg_ref, o_ref, hbuf, carry_in, carry_ref, state_ref):
    u = u_ref[...]
    u_hi, u_lo = _split2(u)
    n_ch, kc, nc = bhi_ref.shape
    cw = nc // 2
    n = hbuf.shape[1] // 2
    parts = lambda c: (slice(c * cw, (c + 1) * cw), slice(n + c * cw, n + (c + 1) * cw))
    ys = []
    for c in range(n_ch + 2):
        if c < n_ch:
            uh = u_hi[:, c * kc:(c + 1) * kc]
            ul = u_lo[:, c * kc:(c + 1) * kc]
            bu = _dot(uh, bhi_ref[c]) + _dot(uh, blo_ref[c]) + _dot(ul, bhi_ref[c])
            re, im = parts(c)
            hbuf[:, re] = bu[:, :cw]
            hbuf[:, im] = bu[:, cw:]
        if 1 <= c <= n_ch:
            _s5_scan_chunk(hbuf, pw_ref, c - 1, carry_in, carry_ref, state_ref)
        if c >= 2:
            re, im = parts(c - 2)
            h = jnp.concatenate([hbuf[:, re], hbuf[:, im]], axis=1).astype(BF16)
            ys.append(_dot(h, cc_ref[c - 2]))
    y = jnp.concatenate(ys, axis=1) + d_ref[...] * u
    z = jax.nn.gelu(y)
    o_ref[...] = z * jax.nn.sigmoid(_dot(z.astype(BF16), wg_ref[...]) + bg_ref[...])


def _s5_prompt_kernel(u_ref, bhi_ref, blo_ref, pw_ref, cc_ref, d_ref, wg_ref, bg_ref, o_ref, st_ref, hbuf, carry_ref):
    ti = pl.program_id(1)

    @pl.when(ti == 0)
    def _():
        carry_ref[...] = jnp.zeros_like(carry_ref)

    _s5_block(u_ref, bhi_ref, blo_ref, pw_ref, cc_ref, d_ref, wg_ref, bg_ref, o_ref, hbuf, None, carry_ref, None)

    @pl.when(ti == pl.num_programs(1) - 1)
    def _():
        st_ref[0] = carry_ref[...]


def _s5_sample_kernel(u_ref, h0_ref, bhi_ref, blo_ref, pw_ref, cc_ref, d_ref, wg_ref, bg_ref, o_ref, st_ref, hbuf):
    _s5_block(u_ref, bhi_ref, blo_ref, pw_ref, cc_ref, d_ref, wg_ref, bg_ref, o_ref, hbuf,
              lambda t, lanes: h0_ref[t:t + 1, lanes], None, st_ref)


def _s5_weight_specs(bhi, pw, cc, width):
    return [_const_spec(bhi.shape), _const_spec(bhi.shape), _const_spec(pw.shape), _const_spec(cc.shape),
            _const_spec((1, width)), _const_spec((width, width)), _const_spec((1, width))]


def _s5_prompt(u, n_batch, bhi, blo, pw, cc, d_skip, w_glu, b_glu):
    t, width = u.shape
    per = t // n_batch // TOK_BLOCK
    two_n = pw.shape[1]
    tok = pl.BlockSpec((TOK_BLOCK, width), lambda b, i: (b * per + i, 0))
    return pl.pallas_call(
        _s5_prompt_kernel,
        grid=(n_batch, per),
        in_specs=[tok] + _s5_weight_specs(bhi, pw, cc, width),
        out_specs=[tok, pl.BlockSpec((1, 1, two_n), lambda b, i: (b, 0, 0))],
        out_shape=[jax.ShapeDtypeStruct((t, width), F32), jax.ShapeDtypeStruct((n_batch, 1, two_n), F32)],
        scratch_shapes=[pltpu.VMEM((TOK_BLOCK, two_n), F32), pltpu.VMEM((1, two_n), F32)],
        compiler_params=_cparams("arbitrary", "arbitrary"),
        name="s5_prompt",
    )(u, bhi, blo, pw, cc, d_skip, w_glu, b_glu)


def _s5_sample(u, h0, bhi, blo, pw, cc, d_skip, w_glu, b_glu):
    t, width = u.shape
    two_n = pw.shape[1]
    tiles = TOK_BLOCK // SUBLANES
    tok = pl.BlockSpec((TOK_BLOCK, width), lambda i: (i, 0))
    st = pl.BlockSpec((tiles, two_n), lambda i: (i, 0))
    return pl.pallas_call(
        _s5_sample_kernel,
        grid=(t // TOK_BLOCK,),
        in_specs=[tok, st] + _s5_weight_specs(bhi, pw, cc, width),
        out_specs=[tok, st],
        out_shape=[jax.ShapeDtypeStruct((t, width), F32), jax.ShapeDtypeStruct(h0.shape, F32)],
        scratch_shapes=[pltpu.VMEM((TOK_BLOCK, two_n), F32)],
        compiler_params=_cparams("arbitrary"),
        name="s5_sample",
    )(u, h0, bhi, blo, pw, cc, d_skip, w_glu, b_glu)


def _outproj_kernel(att_ref, ssm_ref, ga_ref, gs_ref, wa_ref, ws_ref, x_ref, gate_ref, o_ref):
    def norm(v, g):
        return (v * lax.rsqrt(jnp.mean(v * v, axis=-1, keepdims=True) + EPS) * g).astype(BF16)

    merged = _dot(norm(att_ref[...], ga_ref[...]), wa_ref[...]) + _dot(norm(ssm_ref[...], gs_ref[...]), ws_ref[...])
    tg, _, d = x_ref.shape
    o_ref[...] = x_ref[...] + gate_ref[...] * merged.reshape(tg, SUBLANES, d)


def _out_projection(att, ssm, g_att, g_ssm, w_att, w_ssm, xg, gate):
    ng, _, d = xg.shape
    tg = TOK_BLOCK // SUBLANES
    steps = ng // tg
    wa = att.shape[1]
    ws = ssm.shape[1]
    return pl.pallas_call(
        _outproj_kernel,
        grid=(steps,),
        in_specs=[pl.BlockSpec((TOK_BLOCK, wa), lambda i: (i, 0)), pl.BlockSpec((TOK_BLOCK, ws), lambda i: (i, 0)),
                  _const_spec((1, wa)), _const_spec((1, ws)), _const_spec((wa, d)), _const_spec((ws, d)),
                  pl.BlockSpec((tg, SUBLANES, d), lambda i: (i, 0, 0)), _mod_spec(gate.shape[0], steps, tg, d)],
        out_specs=pl.BlockSpec((tg, SUBLANES, d), lambda i: (i, 0, 0)),
        out_shape=jax.ShapeDtypeStruct(xg.shape, F32),
        compiler_params=_cparams("arbitrary"),
        name="out_projection",
    )(att, ssm, g_att, g_ssm, w_att, w_ssm, xg, gate)


def _topk_rows(problems, k, row=None):
    scores = [p[0] for p in problems]
    if row is None:
        row = lax.broadcasted_iota(jnp.int32, scores[0].shape, 0).astype(F32)
    for i in range(k):
        for z, (_, val_ref, idx_ref) in enumerate(problems):
            s = scores[z]
            m = jnp.max(s, axis=0, keepdims=True)
            j = jnp.min(jnp.where(s == m, row, ROW_SENTINEL), axis=0, keepdims=True)
            val_ref[i:i + 1, :] = m
            idx_ref[i:i + 1, :] = j
            scores[z] = jnp.where(row == j, -jnp.inf, s)


def _pair_candidates():
    k = PEER_TOPK
    runs = [(0, 0, 1, 0, k), (16, 1, 1, 0, 8), (24, 2, 1, 0, 8), (32, 3, 1, 0, 8),
            (40, 8, 8, 0, 1), (48, 4, 4, 0, 1), (52, 4, 4, 1, 1), (56, 4, 4, 2, 1)]
    covered = set()
    for _, a0, na, b0, nb in runs:
        for a in range(a0, a0 + na):
            for b in range(b0, b0 + nb):
                assert (a, b) not in covered
                covered.add((a, b))
    assert all((a, b) in covered for a in range(k) for b in range(k) if (a + 1) * (b + 1) <= k)
    return runs


def _pair_flat_index():
    k = PEER_TOPK
    r = lax.broadcasted_iota(jnp.int32, (PAIR_ROWS, LANES), 0)
    out = (ROW_SENTINEL / 2 + r).astype(F32)
    for row0, a0, na, b0, nb in _pair_candidates():
        i = r - row0
        flat = (a0 * k + b0 + i) if na == 1 else ((a0 + i) * k + b0)
        out = jnp.where((i >= 0) & (i < na * nb), flat.astype(F32), out)
    return out


def _take_rows(table_ref, idx, n):
    out = jnp.zeros_like(idx)
    for r in range(n):
        out = jnp.where(idx == float(r), table_ref[r:r + 1, :], out)
    return out


def _peer_select_kernel(x_ref, shift_ref, scale_ref, xn_ref, shiftn_ref, scalen_ref, g_ref, wqt_ref, sk_ref,
                        h_ref, e_ref, gw_ref, qt_sc, qn_sc, hc_sc, hn_sc, val_sc, idx_sc, cs_sc, top_sc, pick_sc):
    tg, _, d = x_ref.shape
    n_sides, n_keys, half = sk_ref.shape
    n_groups = n_sides // 2 // PEER_HEAD_GROUP
    q_rows = qt_sc.shape[0] // n_groups

    def normed(x, shift, scale):
        return _rms_mod(x[...], g_ref[...], shift[...], scale[...]).reshape(tg * SUBLANES, d).astype(BF16)

    @pl.when(pl.program_id(0) == 0)
    def _():
        h0 = normed(x_ref, shift_ref, scale_ref)
        hc_sc[...] = h0
        qt_sc[...] = _dot_nt(wqt_ref[...], h0).astype(BF16)

    h_ref[...] = hc_sc[...]
    hn_sc[...] = normed(xn_ref, shiftn_ref, scalen_ref)
    k = PEER_TOPK
    n_chunks = tg * SUBLANES // LANES
    flat = _pair_flat_index()
    runs = _pair_candidates()
    n_used = max(row0 + na * nb for row0, _, na, _, nb in runs)
    units = [(hg, c) for hg in range(PEER_HEAD_GROUP) for c in range(n_chunks)]
    for z in range(len(units)):
        cs_sc[z, n_used:, :] = jnp.full((PAIR_ROWS - n_used, LANES), -jnp.inf, F32)

    def group_body(group, carry):
        first = []
        for z, (hg, c) in enumerate(units):
            for side in range(2):
                hx = 2 * (group * PEER_HEAD_GROUP + hg) + side
                r0 = pl.multiple_of(hx * half, half)
                s = _dot(sk_ref[hx], qt_sc[pl.ds(r0, half), pl.ds(c * LANES, LANES)])
                first.append((s, val_sc.at[z, side], idx_sc.at[z, side]))
        _topk_rows(first, k)
        second = []
        for z in range(len(units)):
            for row0, a0, na, b0, nb in runs:
                cs_sc[z, row0:row0 + na * nb, :] = val_sc[z, 0, a0:a0 + na, :] + val_sc[z, 1, b0:b0 + nb, :]
            second.append((cs_sc[z], top_sc.at[z], pick_sc.at[z]))
        _topk_rows(second, k, flat)
        for z, (hg, c) in enumerate(units):
            r_out = pl.multiple_of((group * PEER_HEAD_GROUP + hg) * k, k)
            cols = pl.ds(c * LANES, LANES)
            ts = top_sc[z]
            p = jnp.exp(ts - ts[0:1, :])
            gw_ref[pl.ds(r_out, k), cols] = p / jnp.sum(p, axis=0, keepdims=True)
            pick = pick_sc[z]
            a = jnp.floor(pick * (1.0 / k))
            first_key = _take_rows(idx_sc.at[z, 0], a, k)
            second_key = _take_rows(idx_sc.at[z, 1], pick - a * float(k), k)
            e_ref[pl.ds(r_out, k), cols] = (first_key * float(n_keys) + second_key).astype(jnp.int32)
        q_slab = pl.ds(pl.multiple_of(group * q_rows, q_rows), q_rows)
        qn_sc[q_slab, :] = _dot_nt(wqt_ref[q_slab, :], hn_sc[...]).astype(BF16)
        return carry

    lax.fori_loop(0, n_groups, group_body, 0)
    qt_sc[...] = qn_sc[...]
    hc_sc[...] = hn_sc[...]


def _peer_select(xg, shift, scale, g, w_q_t, subkeys):
    ng, _, d = xg.shape
    t = ng * SUBLANES
    tg = TOK_BLOCK // SUBLANES
    steps = ng // tg
    n_sides = subkeys.shape[0]
    k = PEER_TOPK
    rows = (n_sides // 2) * k
    nch = PEER_HEAD_GROUP * TOK_BLOCK // LANES
    assert (n_sides // 2) % PEER_HEAD_GROUP == 0
    mod_spec = _mod_spec(shift.shape[0], steps, tg, d)
    following = lambda i: jnp.minimum(i + 1, steps - 1)
    mod_next = _mod_spec(shift.shape[0], steps, tg, d, following)
    x_spec = pl.BlockSpec((tg, SUBLANES, d), lambda i: (i, 0, 0))
    x_next = pl.BlockSpec((tg, SUBLANES, d), lambda i: (following(i), 0, 0))
    q_tile = pltpu.VMEM((w_q_t.shape[0], TOK_BLOCK), BF16)
    h_tile = pltpu.VMEM((TOK_BLOCK, d), BF16)
    return pl.pallas_call(
        _peer_select_kernel,
        grid=(steps,),
        in_specs=[x_spec, mod_spec, mod_spec, x_next, mod_next, mod_next, _const_spec((1, d)),
                  _const_spec(w_q_t.shape), _const_spec(subkeys.shape)],
        out_specs=[pl.BlockSpec((TOK_BLOCK, d), lambda i: (i, 0)), pl.BlockSpec((rows, TOK_BLOCK), lambda i: (0, i)),
                   pl.BlockSpec((rows, TOK_BLOCK), lambda i: (0, i))],
        out_shape=[jax.ShapeDtypeStruct((t, d), BF16), jax.ShapeDtypeStruct((rows, t), jnp.int32),
                   jax.ShapeDtypeStruct((rows, t), F32)],
        scratch_shapes=[q_tile, q_tile, h_tile, h_tile,
                        pltpu.VMEM((nch, 2, k, LANES), F32), pltpu.VMEM((nch, 2, k, LANES), F32),
                        pltpu.VMEM((nch, PAIR_ROWS, LANES), F32),
                        pltpu.VMEM((nch, k, LANES), F32), pltpu.VMEM((nch, k, LANES), F32)],
        compiler_params=_cparams("arbitrary"),
        name="peer_select",
    )(xg, shift, scale, xg, shift, scale, g, w_q_t, subkeys)


def _peer_mask_kernel(e_ref, g_ref, o_ref, et_sc, gt_sc):
    n_tiles = o_ref.shape[0]
    n2 = o_ref.shape[2]
    n1 = n_tiles * SUBLANES
    picks = e_ref.shape[0]
    shift = n2.bit_length() - 1
    et_sc[...] = e_ref[...].T
    gt_sc[...] = g_ref[...].T
    as_bf16 = lambda i: i.astype(F32).astype(BF16)
    sub1 = as_bf16(lax.broadcasted_iota(jnp.int32, (n1, picks), 0))
    sub2 = as_bf16(lax.broadcasted_iota(jnp.int32, (n2, picks), 0))
    zero = jnp.zeros((), BF16)

    def body(t, _):
        e = et_sc[pl.ds(t, 1), :]
        g_hi, g_lo = _split2(gt_sc[pl.ds(t, 1), :])
        hit1 = sub1 == as_bf16(jnp.right_shift(e, shift))
        hit2 = sub2 == as_bf16(jnp.bitwise_and(e, n2 - 1))
        r = jnp.where(hit2, jnp.ones((), BF16), zero)
        lhs = jnp.concatenate([jnp.where(hit1, g_hi, zero), jnp.where(hit1, g_lo, zero)], axis=1)
        m = _dot_nt(lhs, jnp.concatenate([r, r], axis=1))
        r0 = pl.multiple_of(t * SUBLANES, SUBLANES)
        for jb in range(n_tiles):
            o_ref[jb, pl.ds(r0, SUBLANES), :] = m[jb * SUBLANES:(jb + 1) * SUBLANES, :]
        return 0

    lax.fori_loop(0, et_sc.shape[0], body, 0, unroll=32)


def _peer_mask(e_pick, g_pick):
    picks, t = e_pick.shape
    n = PEER_N_KEYS
    assert n & (n - 1) == 0 and n <= 256
    tb = MASK_TOK_BLOCK
    return pl.pallas_call(
        _peer_mask_kernel,
        grid=(t // tb,),
        in_specs=[pl.BlockSpec((picks, tb), lambda i: (0, i)), pl.BlockSpec((picks, tb), lambda i: (0, i))],
        out_specs=pl.BlockSpec((n // SUBLANES, tb * SUBLANES, n), lambda i: (0, i, 0)),
        out_shape=jax.ShapeDtypeStruct((n // SUBLANES, t * SUBLANES, n), F32),
        scratch_shapes=[pltpu.VMEM((tb, picks), jnp.int32), pltpu.VMEM((tb, picks), F32)],
        compiler_params=_cparams("arbitrary"),
        name="peer_mask",
    )(e_pick, g_pick)


def _peer_ffn_kernel(h_ref, u_ref, v_ref, m_ref, x_ref, gate_ref, gf_ref, o_ref, acc_ref):
    j = pl.program_id(1)

    @pl.when(j == 0)
    def _():
        acc_ref[...] = jnp.zeros_like(acc_ref)

    tm = h_ref.shape[0]
    n2 = m_ref.shape[2]
    a = _dot_nt(h_ref[...], u_ref[...])
    m2 = m_ref.at[0]
    parts = []
    for i1 in range(SUBLANES):
        gate = m2[pl.ds(i1, tm, stride=SUBLANES), :]
        parts.append((gate * jax.nn.gelu(a[:, i1 * n2:(i1 + 1) * n2])).astype(BF16))
    acc_ref[...] += _dot(jnp.concatenate(parts, axis=1), v_ref[...])

    @pl.when(j == pl.num_programs(1) - 1)
    def _():
        tg, _, d = x_ref.shape
        x2 = x_ref[...] + gate_ref[...] * acc_ref[...].reshape(tg, SUBLANES, d)
        ms = jnp.mean(x2 * x2, axis=-1, keepdims=True)
        o_ref[...] = x2 * lax.rsqrt(ms + EPS) * gf_ref[...]


def _peer_ffn(h, exp_u, exp_v, mask, xg, gate, g_final):
    t, d = h.shape
    ne = exp_u.shape[0]
    n = PEER_N_KEYS
    tm = PEER_TOK_BLOCK
    te = PEER_EXPERT_TILE
    assert te == SUBLANES * n and mask.shape == (ne // te, t * SUBLANES, n)
    tg = tm // SUBLANES
    steps = t // tm
    return pl.pallas_call(
        _peer_ffn_kernel,
        grid=(steps, ne // te),
        in_specs=[pl.BlockSpec((tm, d), lambda i, j: (i, 0)),
                  pl.BlockSpec((te, d), lambda i, j: (j, 0)), pl.BlockSpec((te, d), lambda i, j: (j, 0)),
                  pl.BlockSpec((1, tm * SUBLANES, n), lambda i, j: (j, i, 0)),
                  pl.BlockSpec((tg, SUBLANES, d), lambda i, j: (i, 0, 0)),
                  _mod_spec(gate.shape[0], steps, tg, d), _const_spec((1, d))],
        out_specs=pl.BlockSpec((tg, SUBLANES, d), lambda i, j: (i, 0, 0)),
        out_shape=jax.ShapeDtypeStruct(xg.shape, F32),
        scratch_shapes=[pltpu.VMEM((tm, d), F32)],
        compiler_params=_cparams("arbitrary", "arbitrary"),
        name="peer_ffn",
    )(h, exp_u, exp_v, mask, xg, gate, g_final)


def _token_stages_pre(xg, mods, lw, emit_bf16):
    shift1, scale1 = mods[0], mods[1]
    return _in_projection(xg, shift1, scale1, lw["g_norm1"], lw["w_in"], lw["w_u"], lw["w_fg"], lw["bf_pad"], emit_bf16)


def _token_stages_post(xg, mods, att, ssm, lw, g_final):
    gate1, shift2, scale2, gate2 = mods[2], mods[3], mods[4], mods[5]
    x1 = _out_projection(att, ssm, lw["g_attn_out"], lw["g_ssm_out"], lw["w_out_att"], lw["w_out_ssm"], xg, gate1)
    h2, e_t, g_t = _peer_select(x1, shift2, scale2, lw["g_norm2"], lw["peer_w_q_t"], lw["peer_subkeys"])
    mask = _peer_mask(e_t, g_t)
    return _peer_ffn(h2, lw["peer_u"], lw["peer_v"], mask, x1, gate2, g_final)


def kernel(x_prompt, x_sample, cache_k, cache_v, cache_logf, state_ssm_re, state_ssm_im, page_table, c_prompt, c_sample, w_ada, b_ada, g_norm1, g_norm2, w_in, b_forget, ssm_a_re, ssm_a_im, ssm_log_dt, ssm_b_re, ssm_b_im, ssm_c_re, ssm_c_im, ssm_d, w_glu, b_glu, g_attn_out, g_ssm_out, w_out, peer_w_q, peer_subkeys, peer_u, peer_v, g_final):
    depth = w_ada.shape[0]
    assert depth == 1, "single trunk layer"
    nb, seq, d = x_prompt.shape
    nd, dseq, _ = x_sample.shape
    assert dseq == SUBLANES
    n_pool, page, n_heads, hd = cache_k.shape[1:]
    fox_w = n_heads * hd
    n_groups, n_state = ssm_a_re.shape[1:]
    ssm_w = n_groups * SSM_CH_PER_GROUP
    assert fox_w == ssm_w
    n_pages = page_table.shape[1]
    layer = 0

    w_in_l = w_in[layer]
    w_fg = jnp.pad(w_in_l[:, 3 * fox_w:3 * fox_w + n_heads], ((0, 0), (0, LANES - n_heads))).astype(BF16)
    bf_pad = jnp.concatenate([b_forget[layer], jnp.zeros((LANES - n_heads,), F32)]).reshape(1, LANES)
    w_out_l = w_out[layer].astype(BF16)
    lw = dict(
        w_in=w_in_l.astype(BF16), w_u=w_in_l[:, 3 * fox_w + n_heads:].astype(BF16), w_fg=w_fg, bf_pad=bf_pad,
        g_norm1=g_norm1[layer].reshape(1, d), g_norm2=g_norm2[layer].reshape(1, d),
        g_attn_out=g_attn_out[layer].reshape(1, fox_w), g_ssm_out=g_ssm_out[layer].reshape(1, ssm_w),
        w_out_att=w_out_l[:fox_w], w_out_ssm=w_out_l[fox_w:],
        peer_w_q_t=peer_w_q[layer].T.astype(BF16),
        peer_subkeys=peer_subkeys[layer].reshape((-1,) + peer_subkeys.shape[-2:]).astype(BF16),
    )
    gf = g_final.reshape(1, d)

    n_c = nb + nd
    n_c_pad = -(-n_c // SUBLANES) * SUBLANES
    c_all = jnp.concatenate([c_prompt, c_sample, jnp.zeros((n_c_pad - n_c, d), F32)], axis=0)
    mod = _ada_modulation(c_all, w_ada[layer], b_ada[layer])
    mods_p = [mod[:nb, i * d:(i + 1) * d].reshape(nb, 1, d) for i in range(N_MOD)]
    mods_s = [mod[nb:n_c, i * d:(i + 1) * d].reshape(nd, 1, d) for i in range(N_MOD)]

    pw_re, pw_im, coef_re, coef_im = _s5_tables(ssm_a_re[layer], ssm_a_im[layer], ssm_log_dt[layer])
    bbar_re, bbar_im = _s5_bbar(coef_re.reshape(n_groups, n_state), coef_im.reshape(n_groups, n_state),
                                ssm_b_re[layer], ssm_b_im[layer])
    b_blk = _block_diag_b(bbar_re, bbar_im)
    b_hi = b_blk.astype(BF16)
    b_lo = (b_blk - b_hi.astype(F32)).astype(BF16)
    c_blk = _block_diag_c(ssm_c_re[layer], ssm_c_im[layer]).astype(BF16)
    pw = _lane_order(pw_re.reshape(SUBLANES, n_groups, n_state), pw_im.reshape(SUBLANES, n_groups, n_state))
    s5w = (b_hi, b_lo, pw, c_blk, ssm_d[layer].reshape(1, ssm_w), w_glu[layer].astype(BF16), b_glu[layer].reshape(1, ssm_w))

    xg_p = x_prompt.reshape(nb * seq // SUBLANES, SUBLANES, d)
    q_p, k_p, v_p, u_p, lf_p, qb_p, kb_p, vb_p = _token_stages_pre(xg_p, mods_p, lw, True)
    fcum = _cumsum_time(lf_p, nb).reshape(nb, seq, LANES)[:, :, :n_heads].transpose(0, 2, 1)
    fq = jnp.broadcast_to(fcum[..., None], (nb, n_heads, seq, LANES))
    fk = fcum.reshape(nb, n_heads, seq // ATT_BLOCK, ATT_BLOCK)
    as3 = lambda a: a.reshape(nb, seq, fox_w)
    att_p, lw["peer_u"], lw["peer_v"] = _prompt_attention(as3(qb_p), as3(kb_p), as3(vb_p), fq, fk, n_heads,
                                                          (peer_u[layer], peer_v[layer]))
    att_p = att_p.reshape(nb * seq, fox_w)
    ssm_p, st_p = _s5_prompt(u_p, nb, *s5w)
    y_p = _token_stages_post(xg_p, mods_p, att_p, ssm_p, lw, gf)
    re_p, im_p = _lane_unorder(st_p.reshape(nb, -1), n_groups, n_state)

    xg_s = x_sample.reshape(nd, SUBLANES, d)
    q_s, k_s, v_s, u_s, lf_s = _token_stages_pre(xg_s, mods_s, lw, False)
    rows = dseq * n_heads
    heads_rows = lambda a: a.reshape(nd, dseq, n_heads, hd).reshape(nd, rows, hd)
    logf_pages = cache_logf[layer].reshape(n_pool, page * n_heads)
    dloc, tot = _page_sums(logf_pages, n_heads)
    c_new = _cumsum_short(lf_s.reshape(nd, dseq, LANES))[:, :, :n_heads].reshape(nd, rows)
    cq_col = jnp.broadcast_to(c_new[:, :, None], (nd, rows, LANES))
    cq_row = c_new.reshape(nd, 1, rows)
    att_s = _sample_attention(page_table, heads_rows(q_s), heads_rows(k_s), heads_rows(v_s),
                              cache_k[layer].reshape(n_pool, page * n_heads, hd),
                              cache_v[layer].reshape(n_pool, page * n_heads, hd),
                              dloc, tot, cq_col, cq_row, n_heads)
    att_s = att_s.reshape(nd, dseq, n_heads, hd).reshape(nd * dseq, fox_w)
    h0 = _lane_order(state_ssm_re[layer], state_ssm_im[layer])
    ssm_s, st_s = _s5_sample(u_s, h0, *s5w)
    y_s = _token_stages_post(xg_s, mods_s, att_s, ssm_s, lw, gf)
    re_s, im_s = _lane_unorder(st_s, n_groups, n_state)

    kv5 = lambda a, b_, l_: a.reshape(1, b_, l_, n_heads, hd)
    return (y_p.reshape(nb, seq, d), y_s.reshape(nd, dseq, d),
            kv5(k_p, nb, seq), kv5(v_p, nb, seq), lf_p[:, :n_heads].reshape(1, nb, seq, n_heads),
            re_p[None], im_p[None],
            kv5(k_s, nd, dseq), kv5(v_s, nd, dseq), lf_s[:, :n_heads].reshape(1, nd, dseq, n_heads),
            re_s[None], im_s[None])
```

```python
import functools
import math

import jax
import jax.numpy as jnp
from jax import lax
from jax.experimental import pallas as pl
from jax.experimental.pallas import tpu as pltpu

F32 = jnp.float32
BF16 = jnp.bfloat16
EPS = 1e-6
NEG = -1e30

SUBLANES = 8
LANES = 128
VMEM_LIMIT_BYTES = 56 * 1024 * 1024

N_MOD = 6
FOX_HEAD_DIM = 128
SSM_CH_PER_GROUP = 16
SSM_STATE = 64
SSM_CHUNK_GROUPS = 16
PEER_N_KEYS = 128
PEER_TOPK = 16
PEER_EXPERT_TILE = 1024
PEER_HEAD_GROUP = 2
PAIR_ROWS = 64
ROW_SENTINEL = 1024.0

TOK_BLOCK = 256
ATT_BLOCK = 512
ATT_HEAD_GROUP = 4
LOG2E = 1.4426950408889634
PEER_TOK_BLOCK = 1024
MASK_TOK_BLOCK = 128
SCAN_LANES = 512


def _cparams(*sem):
    return pltpu.CompilerParams(dimension_semantics=sem, vmem_limit_bytes=VMEM_LIMIT_BYTES)


def _const_spec(shape):
    nd = len(shape)
    return pl.BlockSpec(shape, lambda *_: (0,) * nd)


def _split2(x):
    hi = x.astype(BF16)
    lo = (x - hi.astype(F32)).astype(BF16)
    return hi, lo


def _split3(x):
    x1 = x.astype(BF16)
    r1 = x - x1.astype(F32)
    x2 = r1.astype(BF16)
    x3 = (r1 - x2.astype(F32)).astype(BF16)
    return x1, x2, x3


def _dot(a, b):
    return jnp.dot(a, b, preferred_element_type=F32)


def _dot_nt(a, b):
    return lax.dot_general(a, b, (((1,), (1,)), ((), ())), preferred_element_type=F32)


def _rms_mod(xg, g, shift, scale):
    ms = jnp.mean(xg * xg, axis=-1, keepdims=True)
    y = xg * lax.rsqrt(ms + EPS) * g
    return y * (1.0 + scale) + shift


def _mod_spec(n_mod_rows, n_steps, tg, d, block_of=lambda i: i):
    if n_mod_rows == n_steps * tg:
        return pl.BlockSpec((tg, 1, d), lambda i, *_: (block_of(i), 0, 0))
    assert n_steps % n_mod_rows == 0
    per = n_steps // n_mod_rows
    return pl.BlockSpec((1, 1, d), lambda i, *_: (block_of(i) // per, 0, 0))


def _ada_kernel(c_ref, w_ref, b_ref, o_ref):
    a = jax.nn.silu(c_ref[...]).astype(BF16)
    o_ref[...] = _dot(a, w_ref[...].astype(BF16)) + b_ref[...]


def _ada_modulation(c_all, w_ada, b_ada):
    m, d = c_all.shape
    n = w_ada.shape[1]
    tn = 1024
    return pl.pallas_call(
        _ada_kernel,
        grid=(n // tn,),
        in_specs=[_const_spec((m, d)), pl.BlockSpec((d, tn), lambda j: (0, j)), pl.BlockSpec((1, tn), lambda j: (0, j))],
        out_specs=pl.BlockSpec((m, tn), lambda j: (0, j)),
        out_shape=jax.ShapeDtypeStruct((m, n), F32),
        compiler_params=_cparams("arbitrary"),
        name="ada_modulation",
    )(c_all, w_ada, b_ada.reshape(1, n))


def _inproj_kernel(x_ref, shift_ref, scale_ref, g_ref, wqkv_ref, wu_ref, wfg_ref, bf_ref,
                   q_ref, k_ref, v_ref, u_ref, lf_ref, *bf16_refs):
    tg, _, d = x_ref.shape
    h = _rms_mod(x_ref[...], g_ref[...], shift_ref[...], scale_ref[...]).reshape(tg * SUBLANES, d).astype(BF16)
    r = _dot(h, wqkv_ref[...])
    w = q_ref.shape[1]
    q_ref[...] = r[:, 0:w]
    k_ref[...] = r[:, w:2 * w]
    v_ref[...] = r[:, 2 * w:3 * w]
    u_ref[...] = _dot(h, wu_ref[...])
    lf_ref[...] = jax.nn.log_sigmoid(_dot(h, wfg_ref[...]) + bf_ref[...])
    for i, ref in enumerate(bf16_refs):
        ref[...] = r[:, i * w:(i + 1) * w].astype(BF16)


def _in_projection(xg, shift, scale, g, w_all, w_u, w_fg, bf_pad, emit_bf16):
    ng, _, d = xg.shape
    t = ng * SUBLANES
    tg = TOK_BLOCK // SUBLANES
    steps = ng // tg
    width = w_u.shape[1]
    mod_spec = _mod_spec(shift.shape[0], steps, tg, d)
    tok = lambda w: pl.BlockSpec((TOK_BLOCK, w), lambda i: (i, 0))
    n_bf16 = 3 if emit_bf16 else 0
    return pl.pallas_call(
        _inproj_kernel,
        grid=(steps,),
        in_specs=[pl.BlockSpec((tg, SUBLANES, d), lambda i: (i, 0, 0)), mod_spec, mod_spec,
                  _const_spec((1, d)), _const_spec((d, 3 * width)), _const_spec(w_u.shape), _const_spec(w_fg.shape),
                  _const_spec((1, LANES))],
        out_specs=[tok(width)] * 4 + [tok(LANES)] + [tok(width)] * n_bf16,
        out_shape=([jax.ShapeDtypeStruct((t, width), F32)] * 4 + [jax.ShapeDtypeStruct((t, LANES), F32)]
                   + [jax.ShapeDtypeStruct((t, width), BF16)] * n_bf16),
        compiler_params=_cparams("arbitrary"),
        name="in_projection",
    )(xg, shift, scale, g, w_all, w_u, w_fg, bf_pad)


def _cumsum_kernel(x_ref, tri_ref, o_ref, carry_ref):
    @pl.when(pl.program_id(1) == 0)
    def _():
        carry_ref[...] = jnp.zeros_like(carry_ref)

    x1, x2, x3 = _split3(x_ref[...])
    tri = tri_ref[...]
    loc = _dot(tri, x1) + _dot(tri, x2) + _dot(tri, x3) + carry_ref[...]
    o_ref[...] = loc
    carry_ref[...] = loc[-1:, :]


def _cumsum_time(x, n_batch):
    t, w = x.shape
    l = t // n_batch
    tb = 256
    tri = (lax.broadcasted_iota(jnp.int32, (tb, tb), 1) <= lax.broadcasted_iota(jnp.int32, (tb, tb), 0)).astype(BF16)
    per = l // tb
    return pl.pallas_call(
        _cumsum_kernel,
        grid=(n_batch, per),
        in_specs=[pl.BlockSpec((tb, w), lambda b, i: (b * per + i, 0)), _const_spec((tb, tb))],
        out_specs=pl.BlockSpec((tb, w), lambda b, i: (b * per + i, 0)),
        out_shape=jax.ShapeDtypeStruct((t, w), F32),
        scratch_shapes=[pltpu.VMEM((1, w), F32)],
        compiler_params=_cparams("arbitrary", "arbitrary"),
        name="logf_cumsum",
    )(x, tri)


def _cumsum_short_kernel(x_ref, o_ref):
    run = x_ref[:, 0:1, :]
    o_ref[:, 0:1, :] = run
    for r in range(1, x_ref.shape[1]):
        run = run + x_ref[:, r:r + 1, :]
        o_ref[:, r:r + 1, :] = run


def _cumsum_short(x):
    return pl.pallas_call(_cumsum_short_kernel, out_shape=jax.ShapeDtypeStruct(x.shape, F32), name="logf_cumsum_new")(x)


def _pattn_kernel(*refs, scale, n_cast):
    q_ref, k_ref, v_ref, fq_ref, fk_ref = refs[:5]
    cast_in = refs[5:5 + n_cast]
    o_ref = refs[5 + n_cast]
    cast_out = refs[6 + n_cast:6 + 2 * n_cast]
    m_sc, l_sc, acc_sc, fq_sc = refs[6 + 2 * n_cast:]
    for src, dst in zip(cast_in, cast_out):
        dst[...] = src[...].astype(BF16)
    qi = pl.program_id(2)
    tq = q_ref.shape[1]
    n_group, _, hd = acc_sc.shape
    m_sc[...] = jnp.full_like(m_sc, NEG)
    l_sc[...] = jnp.zeros_like(l_sc)
    acc_sc[...] = jnp.zeros_like(acc_sc)
    fq_sc[...] = fq_ref[0] * LOG2E
    c = scale * LOG2E

    def block(kj, diagonal):
        start = pl.multiple_of(kj * tq, tq)
        for g in range(n_group):
            hs = slice(g * hd, (g + 1) * hd)
            s = _dot_nt(q_ref[0, :, hs], k_ref[0, pl.ds(start, tq), hs])
            fk2 = fk_ref[0, g, pl.ds(kj, 1), :] * LOG2E
            tiles = []
            for t in range(tq // LANES):
                ls = slice(t * LANES, (t + 1) * LANES)
                st = s[:, ls] * c - fk2[:, ls]
                if diagonal:
                    row = lax.broadcasted_iota(jnp.int32, st.shape, 0)
                    col = lax.broadcasted_iota(jnp.int32, st.shape, 1) + t * LANES
                    st = jnp.where(col <= row, st, NEG)
                tiles.append(st)
            m_prev = m_sc[g]
            fq2 = fq_sc[g]
            m_new = jnp.maximum(m_prev, jnp.max(functools.reduce(jnp.maximum, tiles), axis=1, keepdims=True) + fq2)
            alpha = jnp.exp2(m_prev - m_new)
            shift = fq2 - m_new
            ps = [jnp.exp2(st + shift) for st in tiles]
            l_sc[g] = alpha * l_sc[g] + functools.reduce(jnp.add, ps)
            p = jnp.concatenate([x.astype(BF16) for x in ps], axis=1)
            acc_sc[g] = alpha * acc_sc[g] + _dot(p, v_ref[0, pl.ds(start, tq), hs])
            m_sc[g] = m_new

    def off_diagonal(kj, carry):
        block(kj, False)
        return carry

    lax.fori_loop(0, qi, off_diagonal, 0)
    block(qi, True)
    for g in range(n_group):
        o_ref[0, :, g * hd:(g + 1) * hd] = acc_sc[g] / jnp.sum(l_sc[g], axis=1, keepdims=True)


def _prompt_attention(q, k, v, fq, fk, n_heads, tables):
    b, l, w = q.shape
    hd = w // n_heads
    tq = ATT_BLOCK
    ng = ATT_HEAD_GROUP
    assert fk.shape == (b, n_heads, l // tq, tq) and n_heads % ng == 0
    n_hg, n_q = n_heads // ng, l // tq
    steps = b * n_hg * n_q
    slab = lambda bi, h, i: ((bi * n_hg + h) * n_q + i, 0)
    for tab in tables:
        assert tab.shape[0] % (steps * 2 * SUBLANES) == 0
    table_specs = [pl.BlockSpec((tab.shape[0] // steps, tab.shape[1]), slab) for tab in tables]
    return pl.pallas_call(
        functools.partial(_pattn_kernel, scale=hd ** -0.5, n_cast=len(tables)),
        grid=(b, n_hg, n_q),
        in_specs=[pl.BlockSpec((1, tq, ng * hd), lambda bi, h, i: (bi, i, h)),
                  pl.BlockSpec((1, l, ng * hd), lambda bi, h, i: (bi, 0, h)),
                  pl.BlockSpec((1, l, ng * hd), lambda bi, h, i: (bi, 0, h)),
                  pl.BlockSpec((1, ng, tq, LANES), lambda bi, h, i: (bi, h, i, 0)),
                  pl.BlockSpec((1, ng, l // tq, tq), lambda bi, h, i: (bi, h, 0, 0))] + table_specs,
        out_specs=[pl.BlockSpec((1, tq, ng * hd), lambda bi, h, i: (bi, i, h))] + table_specs,
        out_shape=[jax.ShapeDtypeStruct((b, l, w), F32)] + [jax.ShapeDtypeStruct(tab.shape, BF16) for tab in tables],
        scratch_shapes=[pltpu.VMEM((ng, tq, LANES), F32), pltpu.VMEM((ng, tq, LANES), F32), pltpu.VMEM((ng, tq, hd), F32),
                        pltpu.VMEM((ng, tq, LANES), F32)],
        compiler_params=_cparams("arbitrary", "arbitrary", "arbitrary"),
        name="prompt_attention",
    )(q, k, v, fq, fk, *tables)


def _page_sums_kernel(x_ref, m_after_ref, m_all_ref, dloc_ref, tot_ref):
    x1, x2, x3 = _split3(x_ref[...])
    ma = m_after_ref[...]
    mt = m_all_ref[...]
    dloc_ref[...] = _dot(x1, ma) + _dot(x2, ma) + _dot(x3, ma)
    tot_ref[...] = _dot(x1, mt) + _dot(x2, mt) + _dot(x3, mt)


def _page_sums(logf_pages, n_heads):
    n_pool, w = logf_pages.shape
    r = lax.broadcasted_iota(jnp.int32, (w, w), 0)
    c = lax.broadcasted_iota(jnp.int32, (w, w), 1)
    same_head = (r % n_heads) == (c % n_heads)
    m_after = (same_head & (r // n_heads > c // n_heads)).astype(BF16)
    m_all = same_head.astype(BF16)
    tp = 256
    return pl.pallas_call(
        _page_sums_kernel,
        grid=(n_pool // tp,),
        in_specs=[pl.BlockSpec((tp, w), lambda i: (i, 0)), _const_spec((w, w)), _const_spec((w, w))],
        out_specs=[pl.BlockSpec((tp, w), lambda i: (i, 0))] * 2,
        out_shape=[jax.ShapeDtypeStruct((n_pool, w), F32)] * 2,
        compiler_params=_cparams("arbitrary"),
        name="page_logf_sums",
    )(logf_pages, m_after, m_all)


def _lane_fold(x, op):
    out = x[:, 0:LANES]
    for c in range(1, x.shape[1] // LANES):
        out = op(out, x[:, c * LANES:(c + 1) * LANES])
    return out


def _sattn_kernel(pt_ref, q_ref, kn_ref, vn_ref, cqc_ref, cqr_ref, mb_ref, mbn_ref, ck_hbm, cv_hbm, dloc_hbm, tot_hbm,
                  o_ref, kbuf, vbuf, dbuf, tbuf, sbuf, sems, *, scale):
    b = pl.program_id(0)
    n_pages = kbuf.shape[1]
    slot = lax.rem(b, 2)

    def page_copies(bb, sl, i):
        p = pt_ref[bb, i]
        return (pltpu.make_async_copy(ck_hbm.at[p], kbuf.at[sl, i], sems.at[sl, 0]),
                pltpu.make_async_copy(cv_hbm.at[p], vbuf.at[sl, i], sems.at[sl, 1]),
                pltpu.make_async_copy(dloc_hbm.at[pl.ds(p, 1)], dbuf.at[sl, i], sems.at[sl, 2]),
                pltpu.make_async_copy(tot_hbm.at[pl.ds(p, 1)], tbuf.at[sl, i], sems.at[sl, 3]))

    def fetch(bb, sl):
        for i in range(n_pages):
            for cp in page_copies(bb, sl, i):
                cp.start()

    @pl.when(b == 0)
    def _():
        fetch(0, 0)

    @pl.when(b + 1 < pl.num_programs(0))
    def _():
        fetch(b + 1, 1 - slot)

    for i in range(n_pages):
        for cp in page_copies(b, slot, i):
            cp.wait()

    q = q_ref[0].astype(BF16)
    cq = cqc_ref[0][:, 0:1]
    base = mb_ref[...] + cq

    def scores(k, carry):
        after, mrun = carry
        i = n_pages - 1 - k
        s = _dot_nt(q, kbuf[slot, i].astype(BF16)) * scale + base + (dbuf[slot, i] + after)
        sbuf[i] = s
        return after + tbuf[slot, i], jnp.maximum(mrun, _lane_fold(s, jnp.maximum))

    rows = q.shape[0]
    init = (jnp.zeros(dbuf.shape[2:], F32), jnp.full((rows, LANES), NEG, F32))
    _, mrun = lax.fori_loop(0, n_pages, scores, init, unroll=True)
    sn = _dot_nt(q, kn_ref[0].astype(BF16)) * scale + (cq - cqr_ref[0]) + mbn_ref[...]
    m = jnp.maximum(jnp.max(mrun, axis=1, keepdims=True), jnp.max(sn, axis=1, keepdims=True))

    def values(i, carry):
        lrun, acc = carry
        p = jnp.exp(sbuf[i] - m)
        return lrun + _lane_fold(p, jnp.add), acc + _dot(p.astype(BF16), vbuf[slot, i].astype(BF16))

    init = (jnp.zeros((rows, LANES), F32), jnp.zeros((rows, q.shape[1]), F32))
    lrun, acc = lax.fori_loop(0, n_pages, values, init, unroll=True)
    pn = jnp.exp(sn - m)
    l = jnp.sum(lrun, axis=1, keepdims=True) + jnp.sum(pn, axis=1, keepdims=True)
    acc = acc + _dot(pn.astype(BF16), vn_ref[0].astype(BF16))
    o_ref[0] = acc / l


def _sample_attention(page_table, q, kn, vn, cache_k, cache_v, dloc, tot, cq_col, cq_row, n_heads):
    b, rows, hd = q.shape
    n_pages = page_table.shape[1]
    cols = cache_k.shape[1]
    r = lax.broadcasted_iota(jnp.int32, (rows, cols), 0)
    c = lax.broadcasted_iota(jnp.int32, (rows, cols), 1)
    mask_past = jnp.where((r % n_heads) == (c % n_heads), 0.0, NEG).astype(F32)
    rn = lax.broadcasted_iota(jnp.int32, (rows, rows), 0)
    cn = lax.broadcasted_iota(jnp.int32, (rows, rows), 1)
    mask_new = jnp.where(((rn % n_heads) == (cn % n_heads)) & (cn // n_heads <= rn // n_heads), 0.0, NEG).astype(F32)
    per_b = lambda bi, pt: (bi, 0, 0)
    const2 = lambda bi, pt: (0, 0)
    hbm = pl.BlockSpec(memory_space=pltpu.HBM)
    grid_spec = pltpu.PrefetchScalarGridSpec(
        num_scalar_prefetch=1,
        grid=(b,),
        in_specs=[pl.BlockSpec((1, rows, hd), per_b), pl.BlockSpec((1, rows, hd), per_b), pl.BlockSpec((1, rows, hd), per_b),
                  pl.BlockSpec((1, rows, LANES), per_b), pl.BlockSpec((1, 1, rows), per_b),
                  pl.BlockSpec((rows, cols), const2), pl.BlockSpec((rows, rows), const2),
                  hbm, hbm, hbm, hbm],
        out_specs=pl.BlockSpec((1, rows, hd), per_b),
        scratch_shapes=[pltpu.VMEM((2, n_pages, cols, hd), F32), pltpu.VMEM((2, n_pages, cols, hd), F32),
                        pltpu.VMEM((2, n_pages, 1, cols), F32), pltpu.VMEM((2, n_pages, 1, cols), F32),
                        pltpu.VMEM((n_pages, rows, cols), F32), pltpu.SemaphoreType.DMA((2, 4))],
    )
    return pl.pallas_call(
        functools.partial(_sattn_kernel, scale=hd ** -0.5),
        grid_spec=grid_spec,
        out_shape=jax.ShapeDtypeStruct((b, rows, hd), F32),
        compiler_params=_cparams("arbitrary"),
        name="sample_attention",
    )(page_table, q, kn, vn, cq_col, cq_row, mask_past, mask_new, cache_k, cache_v, dloc, tot)


def _s5_tables_kernel(are_ref, aim_ref, ldt_ref, pre_ref, pim_ref, cre_ref, cim_ref):
    a_re = jnp.minimum(are_ref[...], -1e-4)
    a_im = aim_ref[...]
    dt = jnp.exp(ldt_ref[...])
    mag = jnp.exp(a_re * dt)
    abr = mag * jnp.cos(a_im * dt)
    abi = mag * jnp.sin(a_im * dt)
    den = a_re * a_re + a_im * a_im
    nr, ni = abr - 1.0, abi
    cre_ref[...] = (nr * a_re + ni * a_im) / den
    cim_ref[...] = (ni * a_re - nr * a_im) / den
    pr, pi = abr, abi
    for r in range(SUBLANES):
        pre_ref[r:r + 1, :] = pr
        pim_ref[r:r + 1, :] = pi
        pr, pi = pr * abr - pi * abi, pr * abi + pi * abr


def _s5_tables(a_re, a_im, log_dt):
    g, p = a_re.shape
    n = g * p
    flat = lambda x: x.reshape(1, n)
    ldt = jnp.broadcast_to(log_dt[:, None], (g, p))
    row = jax.ShapeDtypeStruct((1, n), F32)
    tab = jax.ShapeDtypeStruct((SUBLANES, n), F32)
    return pl.pallas_call(
        _s5_tables_kernel,
        out_shape=[tab, tab, row, row],
        name="s5_tables",
    )(flat(a_re), flat(a_im), flat(ldt))


def _s5_bbar_kernel(cre_ref, cim_ref, bre_ref, bim_ref, ore_ref, oim_ref):
    cr, ci = cre_ref[...], cim_ref[...]
    br, bi = bre_ref[...], bim_ref[...]
    ore_ref[...] = cr * br - ci * bi
    oim_ref[...] = cr * bi + ci * br


def _s5_bbar(coef_re, coef_im, b_re, b_im):
    shp = jax.ShapeDtypeStruct(b_re.shape, F32)
    return pl.pallas_call(_s5_bbar_kernel, out_shape=[shp, shp], name="s5_bbar")(
        coef_re[..., None], coef_im[..., None], b_re, b_im)


def _lane_order(re, im):
    lead = re.shape[:-2]
    n = re.shape[-2] * re.shape[-1]
    return jnp.concatenate([re.reshape(lead + (n,)), im.reshape(lead + (n,))], axis=-1)


def _lane_unorder(x, g, p):
    lead = x.shape[:-1]
    n = g * p
    return x[..., :n].reshape(lead + (g, p)), x[..., n:].reshape(lead + (g, p))


def _block_diag_b(bbar_re, bbar_im):
    g, p, c = bbar_re.shape
    k = SSM_CHUNK_GROUPS
    nch = g // k
    cols = 2 * k * p
    src = jnp.stack([bbar_re, bbar_im]).reshape(2, nch, k, p, c).transpose(1, 4, 0, 2, 3).reshape(nch, 1, c, cols)
    col_group = (lax.broadcasted_iota(jnp.int32, (k, 1, cols), 2) % (k * p)) // p
    keep = col_group == lax.broadcasted_iota(jnp.int32, (k, 1, cols), 0)
    return jnp.where(keep[None], src, 0.0).reshape(nch, k * c, cols)


def _block_diag_c(c_re, c_im):
    g, c, p = c_re.shape
    k = SSM_CHUNK_GROUPS
    nch = g // k
    rows = 2 * k * p
    src = jnp.stack([c_re, -c_im]).reshape(2, nch, k, c, p).transpose(1, 0, 2, 4, 3).reshape(nch, rows, c)
    row_group = (lax.broadcasted_iota(jnp.int32, (rows, k * c), 0) % (k * p)) // p
    keep = row_group == lax.broadcasted_iota(jnp.int32, (rows, k * c), 1) // c
    return jnp.where(keep[None], jnp.tile(src, (1, 1, k)), 0.0)


def _s5_scan_chunk(hbuf, pw_ref, ch, carry_in, carry_ref, state_ref):
    cw = SSM_CHUNK_GROUPS * SSM_STATE
    n = hbuf.shape[1] // 2
    rows = lax.broadcasted_iota(jnp.int32, (SUBLANES, SCAN_LANES), 0)
    for sub in range(cw // SCAN_LANES):
        re0 = ch * cw + sub * SCAN_LANES
        sl_re = pl.ds(re0, SCAN_LANES)
        sl_im = pl.ds(n + re0, SCAN_LANES)
        pr = pw_ref[:, sl_re]
        pi = pw_ref[:, sl_im]
        steps = []
        for sh in (1, 2, 4):
            ar = jnp.where(rows >= sh, pr[sh - 1:sh, :], 0.0)
            ai = jnp.where(rows >= sh, pi[sh - 1:sh, :], 0.0)
            steps.append((sh, ar, ai))
        if carry_in is None:
            hr, hi = carry_ref[:, sl_re], carry_ref[:, sl_im]
        for t in range(hbuf.shape[0] // SUBLANES):
            tile = pl.ds(t * SUBLANES, SUBLANES)
            xr = hbuf[tile, sl_re]
            xi = hbuf[tile, sl_im]
            for sh, ar, ai in steps:
                sr = pltpu.roll(xr, sh, 0)
                si = pltpu.roll(xi, sh, 0)
                xr, xi = xr + ar * sr - ai * si, xi + ar * si + ai * sr
            if carry_in is not None:
                hr, hi = carry_in(t, sl_re), carry_in(t, sl_im)
            xr, xi = xr + pr * hr - pi * hi, xi + pr * hi + pi * hr
            hbuf[tile, sl_re] = xr
            hbuf[tile, sl_im] = xi
            hr, hi = xr[SUBLANES - 1:, :], xi[SUBLANES - 1:, :]
            if state_ref is not None:
                state_ref[t:t + 1, sl_re] = hr
                state_ref[t:t + 1, sl_im] = hi
        if carry_in is None:
            carry_ref[:, sl_re] = hr
            carry_ref[:, sl_im] = hi


def _s5_block(u_ref, bhi_ref, blo_ref, pw_ref, cc_ref, d_ref, wg_ref, bg_ref, o_ref, hbuf, carry_in, carry_ref, state_ref):
    u = u_ref[...]
    u_hi, u_lo = _split2(u)
    n_ch, kc, nc = bhi_ref.shape
    cw = nc // 2
    n = hbuf.shape[1] // 2
    parts = lambda c: (slice(c * cw, (c + 1) * cw), slice(n + c * cw, n + (c + 1) * cw))
    ys = []
    for c in range(n_ch + 2):
        if c < n_ch:
            uh = u_hi[:, c * kc:(c + 1) * kc]
            ul = u_lo[:, c * kc:(c + 1) * kc]
            bu = _dot(uh, bhi_ref[c]) + _dot(uh, blo_ref[c]) + _dot(ul, bhi_ref[c])
            re, im = parts(c)
            hbuf[:, re] = bu[:, :cw]
            hbuf[:, im] = bu[:, cw:]
        if 1 <= c <= n_ch:
            _s5_scan_chunk(hbuf, pw_ref, c - 1, carry_in, carry_ref, state_ref)
        if c >= 2:
            re, im = parts(c - 2)
            h = jnp.concatenate([hbuf[:, re], hbuf[:, im]], axis=1).astype(BF16)
            ys.append(_dot(h, cc_ref[c - 2]))
    y = jnp.concatenate(ys, axis=1) + d_ref[...] * u
    z = jax.nn.gelu(y)
    o_ref[...] = z * jax.nn.sigmoid(_dot(z.astype(BF16), wg_ref[...]) + bg_ref[...])


def _s5_prompt_kernel(u_ref, bhi_ref, blo_ref, pw_ref, cc_ref, d_ref, wg_ref, bg_ref, o_ref, st_ref, hbuf, carry_ref):
    ti = pl.program_id(1)

    @pl.when(ti == 0)
    def _():
        carry_ref[...] = jnp.zeros_like(carry_ref)

    _s5_block(u_ref, bhi_ref, blo_ref, pw_ref, cc_ref, d_ref, wg_ref, bg_ref, o_ref, hbuf, None, carry_ref, None)

    @pl.when(ti == pl.num_programs(1) - 1)
    def _():
        st_ref[0] = carry_ref[...]


def _s5_sample_kernel(u_ref, h0_ref, bhi_ref, blo_ref, pw_ref, cc_ref, d_ref, wg_ref, bg_ref, o_ref, st_ref, hbuf):
    _s5_block(u_ref, bhi_ref, blo_ref, pw_ref, cc_ref, d_ref, wg_ref, bg_ref, o_ref, hbuf,
              lambda t, lanes: h0_ref[t:t + 1, lanes], None, st_ref)


def _s5_weight_specs(bhi, pw, cc, width):
    return [_const_spec(bhi.shape), _const_spec(bhi.shape), _const_spec(pw.shape), _const_spec(cc.shape),
            _const_spec((1, width)), _const_spec((width, width)), _const_spec((1, width))]


def _s5_prompt(u, n_batch, bhi, blo, pw, cc, d_skip, w_glu, b_glu):
    t, width = u.shape
    per = t // n_batch // TOK_BLOCK
    two_n = pw.shape[1]
    tok = pl.BlockSpec((TOK_BLOCK, width), lambda b, i: (b * per + i, 0))
    return pl.pallas_call(
        _s5_prompt_kernel,
        grid=(n_batch, per),
        in_specs=[tok] + _s5_weight_specs(bhi, pw, cc, width),
        out_specs=[tok, pl.BlockSpec((1, 1, two_n), lambda b, i: (b, 0, 0))],
        out_shape=[jax.ShapeDtypeStruct((t, width), F32), jax.ShapeDtypeStruct((n_batch, 1, two_n), F32)],
        scratch_shapes=[pltpu.VMEM((TOK_BLOCK, two_n), F32), pltpu.VMEM((1, two_n), F32)],
        compiler_params=_cparams("arbitrary", "arbitrary"),
        name="s5_prompt",
    )(u, bhi, blo, pw, cc, d_skip, w_glu, b_glu)


def _s5_sample(u, h0, bhi, blo, pw, cc, d_skip, w_glu, b_glu):
    t, width = u.shape
    two_n = pw.shape[1]
    tiles = TOK_BLOCK // SUBLANES
    tok = pl.BlockSpec((TOK_BLOCK, width), lambda i: (i, 0))
    st = pl.BlockSpec((tiles, two_n), lambda i: (i, 0))
    return pl.pallas_call(
        _s5_sample_kernel,
        grid=(t // TOK_BLOCK,),
        in_specs=[tok, st] + _s5_weight_specs(bhi, pw, cc, width),
        out_specs=[tok, st],
        out_shape=[jax.ShapeDtypeStruct((t, width), F32), jax.ShapeDtypeStruct(h0.shape, F32)],
        scratch_shapes=[pltpu.VMEM((TOK_BLOCK, two_n), F32)],
        compiler_params=_cparams("arbitrary"),
        name="s5_sample",
    )(u, h0, bhi, blo, pw, cc, d_skip, w_glu, b_glu)


def _outproj_kernel(att_ref, ssm_ref, ga_ref, gs_ref, wa_ref, ws_ref, x_ref, gate_ref, o_ref):
    def norm(v, g):
        return (v * lax.rsqrt(jnp.mean(v * v, axis=-1, keepdims=True) + EPS) * g).astype(BF16)

    merged = _dot(norm(att_ref[...], ga_ref[...]), wa_ref[...]) + _dot(norm(ssm_ref[...], gs_ref[...]), ws_ref[...])
    tg, _, d = x_ref.shape
    o_ref[...] = x_ref[...] + gate_ref[...] * merged.reshape(tg, SUBLANES, d)


def _out_projection(att, ssm, g_att, g_ssm, w_att, w_ssm, xg, gate):
    ng, _, d = xg.shape
    tg = TOK_BLOCK // SUBLANES
    steps = ng // tg
    wa = att.shape[1]
    ws = ssm.shape[1]
    return pl.pallas_call(
        _outproj_kernel,
        grid=(steps,),
        in_specs=[pl.BlockSpec((TOK_BLOCK, wa), lambda i: (i, 0)), pl.BlockSpec((TOK_BLOCK, ws), lambda i: (i, 0)),
                  _const_spec((1, wa)), _const_spec((1, ws)), _const_spec((wa, d)), _const_spec((ws, d)),
                  pl.BlockSpec((tg, SUBLANES, d), lambda i: (i, 0, 0)), _mod_spec(gate.shape[0], steps, tg, d)],
        out_specs=pl.BlockSpec((tg, SUBLANES, d), lambda i: (i, 0, 0)),
        out_shape=jax.ShapeDtypeStruct(xg.shape, F32),
        compiler_params=_cparams("arbitrary"),
        name="out_projection",
    )(att, ssm, g_att, g_ssm, w_att, w_ssm, xg, gate)


def _topk_rows(problems, k, row=None):
    scores = [p[0] for p in problems]
    if row is None:
        row = lax.broadcasted_iota(jnp.int32, scores[0].shape, 0).astype(F32)
    for i in range(k):
        for z, (_, val_ref, idx_ref) in enumerate(problems):
            s = scores[z]
            m = jnp.max(s, axis=0, keepdims=True)
            j = jnp.min(jnp.where(s == m, row, ROW_SENTINEL), axis=0, keepdims=True)
            val_ref[i:i + 1, :] = m
            idx_ref[i:i + 1, :] = j
            scores[z] = jnp.where(row == j, -jnp.inf, s)


def _pair_candidates():
    k = PEER_TOPK
    runs = [(0, 0, 1, 0, k), (16, 1, 1, 0, 8), (24, 2, 1, 0, 8), (32, 3, 1, 0, 8),
            (40, 8, 8, 0, 1), (48, 4, 4, 0, 1), (52, 4, 4, 1, 1), (56, 4, 4, 2, 1)]
    covered = set()
    for _, a0, na, b0, nb in runs:
        for a in range(a0, a0 + na):
            for b in range(b0, b0 + nb):
                assert (a, b) not in covered
                covered.add((a, b))
    assert all((a, b) in covered for a in range(k) for b in range(k) if (a + 1) * (b + 1) <= k)
    return runs


def _pair_flat_index():
    k = PEER_TOPK
    r = lax.broadcasted_iota(jnp.int32, (PAIR_ROWS, LANES), 0)
    out = (ROW_SENTINEL / 2 + r).astype(F32)
    for row0, a0, na, b0, nb in _pair_candidates():
        i = r - row0
        flat = (a0 * k + b0 + i) if na == 1 else ((a0 + i) * k + b0)
        out = jnp.where((i >= 0) & (i < na * nb), flat.astype(F32), out)
    return out


def _take_rows(table_ref, idx, n):
    out = jnp.zeros_like(idx)
    for r in range(n):
        out = jnp.where(idx == float(r), table_ref[r:r + 1, :], out)
    return out


def _peer_select_kernel(x_ref, shift_ref, scale_ref, xn_ref, shiftn_ref, scalen_ref, g_ref, wqt_ref, sk_ref,
                        h_ref, e_ref, gw_ref, qt_sc, qn_sc, hc_sc, hn_sc, val_sc, idx_sc, cs_sc, top_sc, pick_sc):
    tg, _, d = x_ref.shape
    n_sides, n_keys, half = sk_ref.shape
    n_groups = n_sides // 2 // PEER_HEAD_GROUP
    q_rows = qt_sc.shape[0] // n_groups

    def normed(x, shift, scale):
        return _rms_mod(x[...], g_ref[...], shift[...], scale[...]).reshape(tg * SUBLANES, d).astype(BF16)

    @pl.when(pl.program_id(0) == 0)
    def _():
        h0 = normed(x_ref, shift_ref, scale_ref)
        hc_sc[...] = h0
        qt_sc[...] = _dot_nt(wqt_ref[...], h0).astype(BF16)

    h_ref[...] = hc_sc[...]
    hn_sc[...] = normed(xn_ref, shiftn_ref, scalen_ref)
    k = PEER_TOPK
    n_chunks = tg * SUBLANES // LANES
    flat = _pair_flat_index()
    runs = _pair_candidates()
    n_used = max(row0 + na * nb for row0, _, na, _, nb in runs)
    units = [(hg, c) for hg in range(PEER_HEAD_GROUP) for c in range(n_chunks)]
    for z in range(len(units)):
        cs_sc[z, n_used:, :] = jnp.full((PAIR_ROWS - n_used, LANES), -jnp.inf, F32)

    def group_body(group, carry):
        first = []
        for z, (hg, c) in enumerate(units):
            for side in range(2):
                hx = 2 * (group * PEER_HEAD_GROUP + hg) + side
                r0 = pl.multiple_of(hx * half, half)
                s = _dot(sk_ref[hx], qt_sc[pl.ds(r0, half), pl.ds(c * LANES, LANES)])
                first.append((s, val_sc.at[z, side], idx_sc.at[z, side]))
        _topk_rows(first, k)
        second = []
        for z in range(len(units)):
            for row0, a0, na, b0, nb in runs:
                cs_sc[z, row0:row0 + na * nb, :] = val_sc[z, 0, a0:a0 + na, :] + val_sc[z, 1, b0:b0 + nb, :]
            second.append((cs_sc[z], top_sc.at[z], pick_sc.at[z]))
        _topk_rows(second, k, flat)
        for z, (hg, c) in enumerate(units):
            r_out = pl.multiple_of((group * PEER_HEAD_GROUP + hg) * k, k)
            cols = pl.ds(c * LANES, LANES)
            ts = top_sc[z]
            p = jnp.exp(ts - ts[0:1, :])
            gw_ref[pl.ds(r_out, k), cols] = p / jnp.sum(p, axis=0, keepdims=True)
            pick = pick_sc[z]
            a = jnp.floor(pick * (1.0 / k))
            first_key = _take_rows(idx_sc.at[z, 0], a, k)
            second_key = _take_rows(idx_sc.at[z, 1], pick - a * float(k), k)
            e_ref[pl.ds(r_out, k), cols] = (first_key * float(n_keys) + second_key).astype(jnp.int32)
        q_slab = pl.ds(pl.multiple_of(group * q_rows, q_rows), q_rows)
        qn_sc[q_slab, :] = _dot_nt(wqt_ref[q_slab, :], hn_sc[...]).astype(BF16)
        return carry

    lax.fori_loop(0, n_groups, group_body, 0)
    qt_sc[...] = qn_sc[...]
    hc_sc[...] = hn_sc[...]


def _peer_select(xg, shift, scale, g, w_q_t, subkeys):
    ng, _, d = xg.shape
    t = ng * SUBLANES
    tg = TOK_BLOCK // SUBLANES
    steps = ng // tg
    n_sides = subkeys.shape[0]
    k = PEER_TOPK
    rows = (n_sides // 2) * k
    nch = PEER_HEAD_GROUP * TOK_BLOCK // LANES
    assert (n_sides // 2) % PEER_HEAD_GROUP == 0
    mod_spec = _mod_spec(shift.shape[0], steps, tg, d)
    following = lambda i: jnp.minimum(i + 1, steps - 1)
    mod_next = _mod_spec(shift.shape[0], steps, tg, d, following)
    x_spec = pl.BlockSpec((tg, SUBLANES, d), lambda i: (i, 0, 0))
    x_next = pl.BlockSpec((tg, SUBLANES, d), lambda i: (following(i), 0, 0))
    q_tile = pltpu.VMEM((w_q_t.shape[0], TOK_BLOCK), BF16)
    h_tile = pltpu.VMEM((TOK_BLOCK, d), BF16)
    return pl.pallas_call(
        _peer_select_kernel,
        grid=(steps,),
        in_specs=[x_spec, mod_spec, mod_spec, x_next, mod_next, mod_next, _const_spec((1, d)),
                  _const_spec(w_q_t.shape), _const_spec(subkeys.shape)],
        out_specs=[pl.BlockSpec((TOK_BLOCK, d), lambda i: (i, 0)), pl.BlockSpec((rows, TOK_BLOCK), lambda i: (0, i)),
                   pl.BlockSpec((rows, TOK_BLOCK), lambda i: (0, i))],
        out_shape=[jax.ShapeDtypeStruct((t, d), BF16), jax.ShapeDtypeStruct((rows, t), jnp.int32),
                   jax.ShapeDtypeStruct((rows, t), F32)],
        scratch_shapes=[q_tile, q_tile, h_tile, h_tile,
                        pltpu.VMEM((nch, 2, k, LANES), F32), pltpu.VMEM((nch, 2, k, LANES), F32),
                        pltpu.VMEM((nch, PAIR_ROWS, LANES), F32),
                        pltpu.VMEM((nch, k, LANES), F32), pltpu.VMEM((nch, k, LANES), F32)],
        compiler_params=_cparams("arbitrary"),
        name="peer_select",
    )(xg, shift, scale, xg, shift, scale, g, w_q_t, subkeys)


def _peer_mask_kernel(e_ref, g_ref, o_ref, et_sc, gt_sc):
    n_tiles = o_ref.shape[0]
    n2 = o_ref.shape[2]
    n1 = n_tiles * SUBLANES
    picks = e_ref.shape[0]
    shift = n2.bit_length() - 1
    et_sc[...] = e_ref[...].T
    gt_sc[...] = g_ref[...].T
    as_bf16 = lambda i: i.astype(F32).astype(BF16)
    sub1 = as_bf16(lax.broadcasted_iota(jnp.int32, (n1, picks), 0))
    sub2 = as_bf16(lax.broadcasted_iota(jnp.int32, (n2, picks), 0))
    zero = jnp.zeros((), BF16)

    def body(t, _):
        e = et_sc[pl.ds(t, 1), :]
        g_hi, g_lo = _split2(gt_sc[pl.ds(t, 1), :])
        hit1 = sub1 == as_bf16(jnp.right_shift(e, shift))
        hit2 = sub2 == as_bf16(jnp.bitwise_and(e, n2 - 1))
        r = jnp.where(hit2, jnp.ones((), BF16), zero)
        lhs = jnp.concatenate([jnp.where(hit1, g_hi, zero), jnp.where(hit1, g_lo, zero)], axis=1)
        m = _dot_nt(lhs, jnp.concatenate([r, r], axis=1))
        r0 = pl.multiple_of(t * SUBLANES, SUBLANES)
        for jb in range(n_tiles):
            o_ref[jb, pl.ds(r0, SUBLANES), :] = m[jb * SUBLANES:(jb + 1) * SUBLANES, :]
        return 0

    lax.fori_loop(0, et_sc.shape[0], body, 0, unroll=32)


def _peer_mask(e_pick, g_pick):
    picks, t = e_pick.shape
    n = PEER_N_KEYS
    assert n & (n - 1) == 0 and n <= 256
    tb = MASK_TOK_BLOCK
    return pl.pallas_call(
        _peer_mask_kernel,
        grid=(t // tb,),
        in_specs=[pl.BlockSpec((picks, tb), lambda i: (0, i)), pl.BlockSpec((picks, tb), lambda i: (0, i))],
        out_specs=pl.BlockSpec((n // SUBLANES, tb * SUBLANES, n), lambda i: (0, i, 0)),
        out_shape=jax.ShapeDtypeStruct((n // SUBLANES, t * SUBLANES, n), F32),
        scratch_shapes=[pltpu.VMEM((tb, picks), jnp.int32), pltpu.VMEM((tb, picks), F32)],
        compiler_params=_cparams("arbitrary"),
        name="peer_mask",
    )(e_pick, g_pick)


def _peer_ffn_kernel(h_ref, u_ref, v_ref, m_ref, acc_ref):
    j = pl.program_id(1)

    @pl.when(j == 0)
    def _():
        acc_ref[...] = jnp.zeros_like(acc_ref)

    tm = h_ref.shape[0]
    n2 = m_ref.shape[2]
    a = _dot_nt(h_ref[...], u_ref[...])
    m2 = m_ref.at[0]
    parts = []
    for i1 in range(SUBLANES):
        gate = m2[pl.ds(i1, tm, stride=SUBLANES), :]
        parts.append((gate * jax.nn.gelu(a[:, i1 * n2:(i1 + 1) * n2])).astype(BF16))
    acc_ref[...] += _dot(jnp.concatenate(parts, axis=1), v_ref[...])


def _peer_final_kernel(x_ref, gate_ref, acc_ref, gf_ref, o_ref):
    tg, _, d = x_ref.shape
    x2 = x_ref[...] + gate_ref[...] * acc_ref[...].reshape(tg, SUBLANES, d)
    ms = jnp.mean(x2 * x2, axis=-1, keepdims=True)
    o_ref[...] = x2 * lax.rsqrt(ms + EPS) * gf_ref[...]


def _peer_ffn(h, exp_u, exp_v, mask, xg, gate, g_final):
    t, d = h.shape
    ne = exp_u.shape[0]
    n = PEER_N_KEYS
    tm = PEER_TOK_BLOCK
    te = PEER_EXPERT_TILE
    assert te == SUBLANES * n and mask.shape == (ne // te, t * SUBLANES, n)
    acc = pl.pallas_call(
        _peer_ffn_kernel,
        grid=(t // tm, ne // te),
        in_specs=[pl.BlockSpec((tm, d), lambda i, j: (i, 0)),
                  pl.BlockSpec((te, d), lambda i, j: (j, 0)), pl.BlockSpec((te, d), lambda i, j: (j, 0)),
                  pl.BlockSpec((1, tm * SUBLANES, n), lambda i, j: (j, i, 0))],
        out_specs=pl.BlockSpec((tm, d), lambda i, j: (i, 0)),
        out_shape=jax.ShapeDtypeStruct((t, d), F32),
        compiler_params=_cparams("arbitrary", "arbitrary"),
        name="peer_ffn",
    )(h, exp_u, exp_v, mask)
    tg = TOK_BLOCK // SUBLANES
    steps = t // TOK_BLOCK
    return pl.pallas_call(
        _peer_final_kernel,
        grid=(steps,),
        in_specs=[pl.BlockSpec((tg, SUBLANES, d), lambda i: (i, 0, 0)), _mod_spec(gate.shape[0], steps, tg, d),
                  pl.BlockSpec((TOK_BLOCK, d), lambda i: (i, 0)), _const_spec((1, d))],
        out_specs=pl.BlockSpec((tg, SUBLANES, d), lambda i: (i, 0, 0)),
        out_shape=jax.ShapeDtypeStruct(xg.shape, F32),
        compiler_params=_cparams("arbitrary"),
        name="peer_final",
    )(xg, gate, acc, g_final)


def _token_stages_pre(xg, mods, lw, emit_bf16):
    shift1, scale1 = mods[0], mods[1]
    return _in_projection(xg, shift1, scale1, lw["g_norm1"], lw["w_in"], lw["w_u"], lw["w_fg"], lw["bf_pad"], emit_bf16)


def _token_stages_post(xg, mods, att, ssm, lw, g_final):
    gate1, shift2, scale2, gate2 = mods[2], mods[3], mods[4], mods[5]
    x1 = _out_projection(att, ssm, lw["g_attn_out"], lw["g_ssm_out"], lw["w_out_att"], lw["w_out_ssm"], xg, gate1)
    h2, e_t, g_t = _peer_select(x1, shift2, scale2, lw["g_norm2"], lw["peer_w_q_t"], lw["peer_subkeys"])
    mask = _peer_mask(e_t, g_t)
    return _peer_ffn(h2, lw["peer_u"], lw["peer_v"], mask, x1, gate2, g_final)


def kernel(x_prompt, x_sample, cache_k, cache_v, cache_logf, state_ssm_re, state_ssm_im, page_table, c_prompt, c_sample, w_ada, b_ada, g_norm1, g_norm2, w_in, b_forget, ssm_a_re, ssm_a_im, ssm_log_dt, ssm_b_re, ssm_b_im, ssm_c_re, ssm_c_im, ssm_d, w_glu, b_glu, g_attn_out, g_ssm_out, w_out, peer_w_q, peer_subkeys, peer_u, peer_v, g_final):
    depth = w_ada.shape[0]
    assert depth == 1, "single trunk layer"
    nb, seq, d = x_prompt.shape
    nd, dseq, _ = x_sample.shape
    assert dseq == SUBLANES
    n_pool, page, n_heads, hd = cache_k.shape[1:]
    fox_w = n_heads * hd
    n_groups, n_state = ssm_a_re.shape[1:]
    ssm_w = n_groups * SSM_CH_PER_GROUP
    assert fox_w == ssm_w
    n_pages = page_table.shape[1]
    layer = 0

    w_in_l = w_in[layer]
    w_fg = jnp.pad(w_in_l[:, 3 * fox_w:3 * fox_w + n_heads], ((0, 0), (0, LANES - n_heads))).astype(BF16)
    bf_pad = jnp.concatenate([b_forget[layer], jnp.zeros((LANES - n_heads,), F32)]).reshape(1, LANES)
    w_out_l = w_out[layer].astype(BF16)
    lw = dict(
        w_in=w_in_l.astype(BF16), w_u=w_in_l[:, 3 * fox_w + n_heads:].astype(BF16), w_fg=w_fg, bf_pad=bf_pad,
        g_norm1=g_norm1[layer].reshape(1, d), g_norm2=g_norm2[layer].reshape(1, d),
        g_attn_out=g_attn_out[layer].reshape(1, fox_w), g_ssm_out=g_ssm_out[layer].reshape(1, ssm_w),
        w_out_att=w_out_l[:fox_w], w_out_ssm=w_out_l[fox_w:],
        peer_w_q_t=peer_w_q[layer].T.astype(BF16),
        peer_subkeys=peer_subkeys[layer].reshape((-1,) + peer_subkeys.shape[-2:]).astype(BF16),
    )
    gf = g_final.reshape(1, d)

    n_c = nb + nd
    n_c_pad = -(-n_c // SUBLANES) * SUBLANES
    c_all = jnp.concatenate([c_prompt, c_sample, jnp.zeros((n_c_pad - n_c, d), F32)], axis=0)
    mod = _ada_modulation(c_all, w_ada[layer], b_ada[layer])
    mods_p = [mod[:nb, i * d:(i + 1) * d].reshape(nb, 1, d) for i in range(N_MOD)]
    mods_s = [mod[nb:n_c, i * d:(i + 1) * d].reshape(nd, 1, d) for i in range(N_MOD)]

    pw_re, pw_im, coef_re, coef_im = _s5_tables(ssm_a_re[layer], ssm_a_im[layer], ssm_log_dt[layer])
    bbar_re, bbar_im = _s5_bbar(coef_re.reshape(n_groups, n_state), coef_im.reshape(n_groups, n_state),
                                ssm_b_re[layer], ssm_b_im[layer])
    b_blk = _block_diag_b(bbar_re, bbar_im)
    b_hi = b_blk.astype(BF16)
    b_lo = (b_blk - b_hi.astype(F32)).astype(BF16)
    c_blk = _block_diag_c(ssm_c_re[layer], ssm_c_im[layer]).astype(BF16)
    pw = _lane_order(pw_re.reshape(SUBLANES, n_groups, n_state), pw_im.reshape(SUBLANES, n_groups, n_state))
    s5w = (b_hi, b_lo, pw, c_blk, ssm_d[layer].reshape(1, ssm_w), w_glu[layer].astype(BF16), b_glu[layer].reshape(1, ssm_w))

    xg_p = x_prompt.reshape(nb * seq // SUBLANES, SUBLANES, d)
    q_p, k_p, v_p, u_p, lf_p, qb_p, kb_p, vb_p = _token_stages_pre(xg_p, mods_p, lw, True)
    fcum = _cumsum_time(lf_p, nb).reshape(nb, seq, LANES)[:, :, :n_heads].transpose(0, 2, 1)
    fq = jnp.broadcast_to(fcum[..., None], (nb, n_heads, seq, LANES))
    fk = fcum.reshape(nb, n_heads, seq // ATT_BLOCK, ATT_BLOCK)
    as3 = lambda a: a.reshape(nb, seq, fox_w)
    att_p, lw["peer_u"], lw["peer_v"] = _prompt_attention(as3(qb_p), as3(kb_p), as3(vb_p), fq, fk, n_heads,
                                                          (peer_u[layer], peer_v[layer]))
    att_p = att_p.reshape(nb * seq, fox_w)
    ssm_p, st_p = _s5_prompt(u_p, nb, *s5w)
    y_p = _token_stages_post(xg_p, mods_p, att_p, ssm_p, lw, gf)
    re_p, im_p = _lane_unorder(st_p.reshape(nb, -1), n_groups, n_state)

    xg_s = x_sample.reshape(nd, SUBLANES, d)
    q_s, k_s, v_s, u_s, lf_s = _token_stages_pre(xg_s, mods_s, lw, False)
    rows = dseq * n_heads
    heads_rows = lambda a: a.reshape(nd, dseq, n_heads, hd).reshape(nd, rows, hd)
    logf_pages = cache_logf[layer].reshape(n_pool, page * n_heads)
    dloc, tot = _page_sums(logf_pages, n_heads)
    c_new = _cumsum_short(lf_s.reshape(nd, dseq, LANES))[:, :, :n_heads].reshape(nd, rows)
    cq_col = jnp.broadcast_to(c_new[:, :, None], (nd, rows, LANES))
    cq_row = c_new.reshape(nd, 1, rows)
    att_s = _sample_attention(page_table, heads_rows(q_s), heads_rows(k_s), heads_rows(v_s),
                              cache_k[layer].reshape(n_pool, page * n_heads, hd),
                              cache_v[layer].reshape(n_pool, page * n_heads, hd),
                              dloc, tot, cq_col, cq_row, n_heads)
    att_s = att_s.reshape(nd, dseq, n_heads, hd).reshape(nd * dseq, fox_w)
    h0 = _lane_order(state_ssm_re[layer], state_ssm_im[layer])
    ssm_s, st_s = _s5_sample(u_s, h0, *s5w)
    y_s = _token_stages_post(xg_s, mods_s, att_s, ssm_s, lw, gf)
    re_s, im_s = _lane_unorder(st_s, n_groups, n_state)

    kv5 = lambda a, b_, l_: a.reshape(1, b_, l_, n_heads, hd)
    return (y_p.reshape(nb, seq, d), y_s.reshape(nd, dseq, d),
            kv5(k_p, nb, seq), kv5(v_p, nb, seq), lf_p[:, :n_heads].reshape(1, nb, seq, n_heads),
            re_p[None], im_p[None],
            kv5(k_s, nd, dseq), kv5(v_s, nd, dseq), lf_s[:, :n_heads].reshape(1, nd, dseq, n_heads),
            re_s[None], im_s[None])
```
